```python
import jax, jax.numpy as jnp
from jax import lax
import numpy as np

D_MODEL = 1024
BATCH = 8
SEQ = 2048
DEPTH = 1
DEC_BATCH = 128
DEC_SEQ = 1
PAST_LEN = 16384
PAGE_SIZE = 128

RET_DK = 256
RET_HEADS = D_MODEL // 256
RET_DV = 2 * RET_DK
RET_QK = RET_HEADS * RET_DK
RET_V = RET_HEADS * RET_DV
RET_CHUNK = 128
ROPE_BASE = 10000.0
GM_GROUPS = 4
GM_WIDTH = D_MODEL
GM_CG = GM_WIDTH // GM_GROUPS
GM_CHUNK = 128
D_FF = ((8 * D_MODEL // 3 + 255) // 256) * 256
EPS = 1e-6
D_IN = 2 * RET_QK + 2 * RET_V + 2 * GM_WIDTH + 2 * D_MODEL
SPLIT_IDX = (
    RET_QK,
    2 * RET_QK,
    2 * RET_QK + RET_V,
    2 * RET_QK + 2 * RET_V,
    2 * RET_QK + 2 * RET_V + GM_WIDTH,
    2 * RET_QK + 2 * RET_V + 2 * GM_WIDTH,
    2 * RET_QK + 2 * RET_V + 2 * GM_WIDTH + D_MODEL,
)

kernel_name = "retention_gmlp_hybrid_step"


def rms_norm(x, g):
    xf = x.astype(jnp.float32)
    y = xf * lax.rsqrt(jnp.mean(xf * xf, axis=-1, keepdims=True) + EPS)
    return (y * g.astype(jnp.float32)).astype(x.dtype)


def head_rms(x):
    xf = x.astype(jnp.float32)
    return xf * lax.rsqrt(jnp.mean(xf * xf, axis=-1, keepdims=True) + EPS)


def layer_norm(x, g, b):
    xf = x.astype(jnp.float32)
    mu = jnp.mean(xf, axis=-1, keepdims=True)
    xc = xf - mu
    y = xc * lax.rsqrt(jnp.mean(xc * xc, axis=-1, keepdims=True) + EPS)
    return (y * g.astype(jnp.float32) + b.astype(jnp.float32)).astype(x.dtype)


def rotary(x, pos):
    half = x.shape[-1] // 2
    inv = ROPE_BASE ** (-jnp.arange(half, dtype=jnp.float32) / half)
    ang = pos[:, None] * inv[None, :]
    cos = jnp.cos(ang)[None, :, None, :]
    sin = jnp.sin(ang)[None, :, None, :]
    x1 = x[..., :half].astype(jnp.float32)
    x2 = x[..., half:].astype(jnp.float32)
    return jnp.concatenate([x1 * cos - x2 * sin, x1 * sin + x2 * cos], axis=-1)


def retention(q, k, v, state0):
    B, T, H, DK = q.shape
    DV = v.shape[-1]
    L = min(T, RET_CHUNK)
    nc = T // L
    lg = jnp.log1p(-jnp.exp2(-5.0 - jnp.arange(H, dtype=jnp.float32)))
    n = jnp.arange(L, dtype=jnp.float32)
    diff = n[:, None] - n[None, :]
    intra_decay = jnp.where(diff[None] >= 0,
                            jnp.exp(jnp.maximum(diff, 0.0)[None] * lg[:, None, None]), 0.0)
    q_decay = jnp.exp((n + 1.0)[:, None] * lg[None, :])[None, :, :, None]
    k_decay = jnp.exp((L - 1.0 - n)[:, None] * lg[None, :])[None, :, :, None]
    chunk_decay = jnp.exp(L * lg)[None, :, None, None]

    def to_chunks(a):
        return a.reshape(B, nc, L, H, a.shape[-1]).transpose(1, 0, 2, 3, 4)

    def step(state, qkv):
        qc, kc, vc = qkv
        scores = jnp.einsum('bthd,bshd->bhts', qc, kc) * intra_decay[None]
        o = (jnp.einsum('bhts,bshv->bthv', scores, vc)
             + jnp.einsum('bthd,bhdv->bthv', qc, state) * q_decay)
        new_state = chunk_decay * state + jnp.einsum('bthd,bthv->bhdv', kc * k_decay, vc)
        return new_state, o

    state, o = lax.scan(step, state0, (to_chunks(q), to_chunks(k), to_chunks(v)))
    o = o.transpose(1, 0, 2, 3, 4).reshape(B, T, H, DV)
    return o, state


def chunk_gmlp(u, v, gm_ws, gm_bs):
    B, T, _ = v.shape
    L = min(T, GM_CHUNK)
    nc = T // L
    vc = v.reshape(B, nc, L, GM_GROUPS, GM_CG)
    ws = jnp.tril(gm_ws[:, :L, :L])
    bias = gm_bs[:, :L].T[None, None, :, :, None]
    mixed = jnp.einsum('gts,bnsgc->bntgc', ws, vc) + bias
    return u * mixed.reshape(B, T, GM_WIDTH)


def hybrid_layer(x, ret_state0, pos, norm_mix_g, w_in, w_ret_o, gm_ln_g, gm_ln_b,
                 gm_ws, gm_bs, w_gm_o, w_o, norm_ffn_g, w_ffn_in, w_ffn_down):
    B, T, _ = x.shape
    xn = rms_norm(x, norm_mix_g)
    z = xn @ w_in
    q, k, v, g, gu, gv, a_ret, a_gm = jnp.split(z, SPLIT_IDX, axis=-1)

    q = rotary(q.reshape(B, T, RET_HEADS, RET_DK), pos)
    k = rotary(k.reshape(B, T, RET_HEADS, RET_DK), pos) * (RET_DK ** -0.5)
    v = v.reshape(B, T, RET_HEADS, RET_DV).astype(jnp.float32)
    o, new_state = retention(q, k, v, ret_state0.astype(jnp.float32))
    o = head_rms(o).reshape(B, T, RET_V)
    branch_ret = (jax.nn.silu(g.astype(jnp.float32)) * o).astype(x.dtype) @ w_ret_o

    gu = jax.nn.gelu(gu)
    gv = layer_norm(jax.nn.gelu(gv), gm_ln_g, gm_ln_b)
    branch_gm = chunk_gmlp(gu, gv, gm_ws, gm_bs) @ w_gm_o

    m = jax.nn.sigmoid(a_ret) * branch_ret + jax.nn.sigmoid(a_gm) * branch_gm
    h = x + (m @ w_o).astype(x.dtype)

    hn = rms_norm(h, norm_ffn_g)
    f_gate, f_up = jnp.split(hn @ w_ffn_in, 2, axis=-1)
    h = h + ((jax.nn.silu(f_gate) * f_up) @ w_ffn_down).astype(x.dtype)
    return h, new_state, gv


def setup_inputs(seed: int = 0) -> dict:
    key = jax.random.key(seed)
    ks = jax.random.split(key, 20)
    nrm = lambda k, shape, s: jax.random.normal(k, shape, jnp.float32) * s
    return {
        "x_prompt": nrm(ks[0], (BATCH, SEQ, D_MODEL), 1.0),
        "x_sample": nrm(ks[1], (DEC_BATCH, DEC_SEQ, D_MODEL), 1.0),
        "state_ret": nrm(ks[2], (DEPTH, DEC_BATCH, RET_HEADS, RET_DK, RET_DV), 0.5),
        "norm_mix_g": 1.0 + nrm(ks[3], (DEPTH, D_MODEL), 0.05),
        "w_in": nrm(ks[4], (DEPTH, D_MODEL, D_IN), D_MODEL ** -0.5),
        "w_ret_o": nrm(ks[5], (DEPTH, RET_V, D_MODEL), RET_V ** -0.5),
        "gm_ln_g": 1.0 + nrm(ks[6], (DEPTH, GM_WIDTH), 0.05),
        "gm_ln_b": nrm(ks[7], (DEPTH, GM_WIDTH), 0.02),
        "gm_ws": nrm(ks[8], (DEPTH, GM_GROUPS, GM_CHUNK, GM_CHUNK), GM_CHUNK ** -0.5),
        "gm_bs": 1.0 + nrm(ks[9], (DEPTH, GM_GROUPS, GM_CHUNK), 0.1),
        "w_gm_o": nrm(ks[10], (DEPTH, GM_WIDTH, D_MODEL), GM_WIDTH ** -0.5),
        "w_o": nrm(ks[11], (DEPTH, D_MODEL, D_MODEL), D_MODEL ** -0.5),
        "norm_ffn_g": 1.0 + nrm(ks[12], (DEPTH, D_MODEL), 0.05),
        "w_ffn_in": nrm(ks[13], (DEPTH, D_MODEL, 2 * D_FF), D_MODEL ** -0.5),
        "w_ffn_down": nrm(ks[14], (DEPTH, D_FF, D_MODEL), D_FF ** -0.5),
        "norm_final_g": 1.0 + nrm(ks[15], (D_MODEL,), 0.05),
    }


def reference(x_prompt, x_sample, state_ret, norm_mix_g, w_in, w_ret_o, gm_ln_g, gm_ln_b,
              gm_ws, gm_bs, w_gm_o, w_o, norm_ffn_g, w_ffn_in, w_ffn_down, norm_final_g):
    pos_prompt = jnp.arange(SEQ, dtype=jnp.float32)
    pos_sample = PAST_LEN + jnp.arange(DEC_SEQ, dtype=jnp.float32)
    hp, hs = x_prompt, x_sample
    ret_p, ret_s, gmv_s = [], [], []
    for l in range(DEPTH):
        w = (norm_mix_g[l], w_in[l], w_ret_o[l], gm_ln_g[l], gm_ln_b[l], gm_ws[l], gm_bs[l],
             w_gm_o[l], w_o[l], norm_ffn_g[l], w_ffn_in[l], w_ffn_down[l])
        zero_state = jnp.zeros((BATCH, RET_HEADS, RET_DK, RET_DV), jnp.float32)
        hp, sp, _ = hybrid_layer(hp, zero_state, pos_prompt, *w)
        hs, ss, vs = hybrid_layer(hs, state_ret[l], pos_sample, *w)
        ret_p.append(sp)
        ret_s.append(ss)
        gmv_s.append(vs)
    y_prompt = rms_norm(hp, norm_final_g)
    y_sample = rms_norm(hs, norm_final_g)
    ret_state_prompt = jnp.stack(ret_p)
    ret_state_sample = jnp.stack(ret_s)
    gm_v_sample = jnp.stack(gmv_s)
    return (y_prompt, y_sample, ret_state_prompt, ret_state_sample, gm_v_sample)
```

```python
import functools
import math

import numpy as np
import jax
import jax.numpy as jnp
from jax import lax
from jax.experimental import pallas as pl
from jax.experimental.pallas import tpu as pltpu

D_MODEL = 1024
PAST_LEN = 16384
RET_DK = 256
RET_HEADS = D_MODEL // 256
RET_DV = 2 * RET_DK
RET_QK = RET_HEADS * RET_DK
RET_V = RET_HEADS * RET_DV
RET_CHUNK = 128
ROPE_BASE = 10000.0
ROPE_HALF = RET_DK // 2
GM_GROUPS = 4
GM_WIDTH = D_MODEL
GM_CG = GM_WIDTH // GM_GROUPS
GM_CHUNK = 128
D_FF = ((8 * D_MODEL // 3 + 255) // 256) * 256
EPS = 1e-6
D_IN = 2 * RET_QK + 2 * RET_V + 2 * GM_WIDTH + 2 * D_MODEL

COL_Q = 0
COL_K = RET_QK
COL_V = 2 * RET_QK
COL_G = 2 * RET_QK + RET_V
COL_GU = 2 * RET_QK + 2 * RET_V
COL_GV = COL_GU + GM_WIDTH
COL_AR = COL_GV + GM_WIDTH
COL_AG = COL_AR + D_MODEL

LOG_GAMMA = tuple(float(np.log1p(-np.exp2(np.float32(-5.0 - h)))) for h in range(RET_HEADS))

VMEM_LIMIT_BYTES = 56 * 1024 * 1024

F32 = jnp.float32
BF16 = jnp.bfloat16


def _dot(a, b):
    return jnp.dot(a, b, preferred_element_type=F32)


def _rms(x):
    return x * lax.rsqrt(jnp.mean(x * x, axis=-1, keepdims=True) + EPS)


def _params(*semantics):
    return pltpu.CompilerParams(dimension_semantics=semantics, vmem_limit_bytes=VMEM_LIMIT_BYTES)


def _in_proj_kernel(x_ref, g_ref, w_ref, z_ref, xn_ref):
    @pl.when(pl.program_id(1) == 0)
    def _():
        xn_ref[...] = (_rms(x_ref[...]) * g_ref[...]).astype(BF16)

    z_ref[...] = _dot(xn_ref[...], w_ref[...]).astype(z_ref.dtype)


def _in_proj(x, g, w_bf16, *, tm, tn, out_dtype):
    m = x.shape[0]
    return pl.pallas_call(
        _in_proj_kernel,
        grid=(m // tm, D_IN // tn),
        in_specs=[
            pl.BlockSpec((tm, D_MODEL), lambda i, j: (i, 0)),
            pl.BlockSpec((1, D_MODEL), lambda i, j: (0, 0)),
            pl.BlockSpec((D_MODEL, tn), lambda i, j: (0, j)),
        ],
        out_specs=pl.BlockSpec((tm, tn), lambda i, j: (i, j)),
        out_shape=jax.ShapeDtypeStruct((m, D_IN), out_dtype),
        scratch_shapes=[pltpu.VMEM((tm, D_MODEL), BF16)],
        compiler_params=_params("parallel", "arbitrary"),
        name="in_proj",
    )(x, g.reshape(1, D_MODEL), w_bf16)


def _rotate(x, cos, sin):
    x1 = x[:, :ROPE_HALF]
    x2 = x[:, ROPE_HALF:]
    return jnp.concatenate([x1 * cos - x2 * sin, x1 * sin + x2 * cos], axis=-1)


def _retention_kernel(q_ref, k_ref, v_ref, g_ref, cos_ref, sin_ref, og_ref, st_ref, *, n_chunks):
    L = RET_CHUNK

    @pl.when(pl.program_id(1) == 0)
    def _():
        st_ref[...] = jnp.zeros_like(st_ref)

    n = lax.broadcasted_iota(jnp.int32, (L, 1), 0).astype(F32)
    diff = (lax.broadcasted_iota(jnp.int32, (L, L), 0)
            - lax.broadcasted_iota(jnp.int32, (L, L), 1)).astype(F32)
    intra = [jnp.where(diff >= 0, jnp.exp(jnp.maximum(diff, 0.0) * lg), 0.0) for lg in LOG_GAMMA]
    q_decay = [jnp.exp((n + 1.0) * lg) for lg in LOG_GAMMA]
    k_decay = [jnp.exp((L - 1.0 - n) * lg) for lg in LOG_GAMMA]
    chunk_decay = [math.exp(L * lg) for lg in LOG_GAMMA]

    def chunk(ci, carry):
        rows = pl.ds(pl.multiple_of(ci * L, L), L)
        cos = cos_ref[rows, :]
        sin = sin_ref[rows, :]
        for h in range(RET_HEADS):
            qk_cols = slice(h * RET_DK, (h + 1) * RET_DK)
            v_cols = slice(h * RET_DV, (h + 1) * RET_DV)
            q = _rotate(q_ref[rows, qk_cols].astype(F32), cos, sin)
            k = _rotate(k_ref[rows, qk_cols].astype(F32), cos, sin) * (RET_DK ** -0.5)
            qb = q.astype(BF16)
            v = v_ref[rows, v_cols]
            scores = lax.dot_general(qb, k.astype(BF16), (((1,), (1,)), ((), ())),
                                     preferred_element_type=F32) * intra[h]
            state = st_ref[0, h]
            o = _dot(scores.astype(BF16), v) + _dot(qb, state.astype(BF16)) * q_decay[h]
            st_ref[0, h] = chunk_decay[h] * state + lax.dot_general(
                (k * k_decay[h]).astype(BF16), v, (((0,), (0,)), ((), ())),
                preferred_element_type=F32)
            gate = g_ref[rows, v_cols].astype(F32)
            og_ref[rows, v_cols] = (gate * jax.nn.sigmoid(gate) * _rms(o)).astype(og_ref.dtype)
        return carry

    lax.fori_loop(0, n_chunks, chunk, 0)


def _retention_prompt(z, cos, sin, *, batch, seq, tb):
    nb = seq // tb
    row = lambda b, c: b * nb + c
    return pl.pallas_call(
        functools.partial(_retention_kernel, n_chunks=tb // RET_CHUNK),
        grid=(batch, nb),
        in_specs=[
            pl.BlockSpec((tb, RET_QK), lambda b, c: (row(b, c), COL_Q // RET_QK)),
            pl.BlockSpec((tb, RET_QK), lambda b, c: (row(b, c), COL_K // RET_QK)),
            pl.BlockSpec((tb, RET_V), lambda b, c: (row(b, c), COL_V // RET_V)),
            pl.BlockSpec((tb, RET_V), lambda b, c: (row(b, c), COL_G // RET_V)),
            pl.BlockSpec((tb, ROPE_HALF), lambda b, c: (c, 0)),
            pl.BlockSpec((tb, ROPE_HALF), lambda b, c: (c, 0)),
        ],
        out_specs=[
            pl.BlockSpec((tb, RET_V), lambda b, c: (row(b, c), 0)),
            pl.BlockSpec((1, RET_HEADS, RET_DK, RET_DV), lambda b, c: (b, 0, 0, 0)),
        ],
        out_shape=[
            jax.ShapeDtypeStruct((batch * seq, RET_V), BF16),
            jax.ShapeDtypeStruct((batch, RET_HEADS, RET_DK, RET_DV), F32),
        ],
        compiler_params=_params("parallel", "arbitrary"),
        name="retention_prompt",
    )(z, z, z, z, cos, sin)


def _row_to_col(row):
    n = row.shape[1]
    eye = lax.broadcasted_iota(jnp.int32, (n, n), 0) == lax.broadcasted_iota(jnp.int32, (n, n), 1)
    return jnp.sum(jnp.where(eye, row, 0.0), axis=1, keepdims=True)


def _retention_step_kernel(q_ref, k_ref, v_ref, g_ref, cos_ref, sin_ref, st_ref, og_ref, new_st_ref,
                           *, n_batch):
    cos = cos_ref[...]
    sin = sin_ref[...]
    for b in range(n_batch):
        for h in range(RET_HEADS):
            qk_cols = slice(h * RET_DK, (h + 1) * RET_DK)
            v_cols = slice(h * RET_DV, (h + 1) * RET_DV)
            q = _rotate(q_ref[b, :, qk_cols], cos, sin)
            k = _rotate(k_ref[b, :, qk_cols], cos, sin) * (RET_DK ** -0.5)
            v = v_ref[b, :, v_cols]
            gamma = math.exp(LOG_GAMMA[h])
            new_state = gamma * st_ref[b, h] + _row_to_col(k) * v
            new_st_ref[b, h] = new_state
            o = jnp.sum(_row_to_col(q) * new_state, axis=0, keepdims=True)
            gate = g_ref[b, :, v_cols]
            og_ref[b, :, v_cols] = (gate * jax.nn.sigmoid(gate) * _rms(o)).astype(og_ref.dtype)


def _retention_step(z, cos, sin, state, *, nb):
    batch = z.shape[0]
    z3 = z.reshape(batch, 1, D_IN)
    return pl.pallas_call(
        functools.partial(_retention_step_kernel, n_batch=nb),
        grid=(batch // nb,),
        in_specs=[
            pl.BlockSpec((nb, 1, RET_QK), lambda i: (i, 0, COL_Q // RET_QK)),
            pl.BlockSpec((nb, 1, RET_QK), lambda i: (i, 0, COL_K // RET_QK)),
            pl.BlockSpec((nb, 1, RET_V), lambda i: (i, 0, COL_V // RET_V)),
            pl.BlockSpec((nb, 1, RET_V), lambda i: (i, 0, COL_G // RET_V)),
            pl.BlockSpec((1, ROPE_HALF), lambda i: (0, 0)),
            pl.BlockSpec((1, ROPE_HALF), lambda i: (0, 0)),
            pl.BlockSpec((nb, RET_HEADS, RET_DK, RET_DV), lambda i: (i, 0, 0, 0)),
        ],
        out_specs=[
            pl.BlockSpec((nb, 1, RET_V), lambda i: (i, 0, 0)),
            pl.BlockSpec((nb, RET_HEADS, RET_DK, RET_DV), lambda i: (i, 0, 0, 0)),
        ],
        out_shape=[
            jax.ShapeDtypeStruct((batch, 1, RET_V), F32),
            jax.ShapeDtypeStruct(state.shape, F32),
        ],
        compiler_params=_params("parallel"),
        name="retention_step",
    )(z3, z3, z3, z3, cos, sin, state)


def _gelu_tanh(x):
    return 0.5 * x * (1.0 + jnp.tanh(math.sqrt(2.0 / math.pi) * (x + 0.044715 * (x * x * x))))


def _layer_norm(x, g, b):
    xc = x - jnp.mean(x, axis=-1, keepdims=True)
    return xc * lax.rsqrt(jnp.mean(xc * xc, axis=-1, keepdims=True) + EPS) * g + b


def _merge(x_ref, ar_ref, ag_ref, og_ref, gm, wret_ref, wgm_ref, wo_ref, h_ref):
    branch_ret = _dot(og_ref[...].astype(BF16), wret_ref[...])
    branch_gm = _dot(gm, wgm_ref[...])
    m = (jax.nn.sigmoid(ar_ref[...].astype(F32)) * branch_ret
         + jax.nn.sigmoid(ag_ref[...].astype(F32)) * branch_gm)
    h_ref[...] = x_ref[...] + _dot(m.astype(BF16), wo_ref[...])


def _mix_kernel(x_ref, gu_ref, gv_ref, ar_ref, ag_ref, og_ref, lng_ref, lnb_ref, ws_ref, bs_ref,
                wret_ref, wgm_ref, wo_ref, h_ref, gm_ref):
    L = GM_CHUNK
    causal = lax.broadcasted_iota(jnp.int32, (L, L), 0) >= lax.broadcasted_iota(jnp.int32, (L, L), 1)
    for c in range(x_ref.shape[0] // L):
        rows = slice(c * L, (c + 1) * L)
        u = _gelu_tanh(gu_ref[rows, :].astype(F32))
        v = _layer_norm(_gelu_tanh(gv_ref[rows, :].astype(F32)), lng_ref[...], lnb_ref[...]).astype(BF16)
        for g in range(GM_GROUPS):
            cols = slice(g * GM_CG, (g + 1) * GM_CG)
            w = jnp.where(causal, ws_ref[g], 0.0).astype(BF16)
            mixed = _dot(w, v[:, cols]) + bs_ref[:, g:g + 1]
            gm_ref[rows, cols] = (u[:, cols] * mixed).astype(BF16)
    _merge(x_ref, ar_ref, ag_ref, og_ref, gm_ref[...], wret_ref, wgm_ref, wo_ref, h_ref)


def _mix_step_kernel(x_ref, gu_ref, gv_ref, ar_ref, ag_ref, og_ref, lng_ref, lnb_ref, ws_ref, bs_ref,
                     wret_ref, wgm_ref, wo_ref, h_ref, v_ref):
    u = _gelu_tanh(gu_ref[...])
    v = _layer_norm(_gelu_tanh(gv_ref[...]), lng_ref[...], lnb_ref[...])
    v_ref[...] = v
    gm = (u * (ws_ref[...] * v + bs_ref[...])).astype(BF16)
    _merge(x_ref, ar_ref, ag_ref, og_ref, gm, wret_ref, wgm_ref, wo_ref, h_ref)


def _mix(x, z, og, ln_g, ln_b, ws, bs, wret, wgm, wo, *, tm, single_position):
    m = x.shape[0]
    zcol = lambda off: pl.BlockSpec((tm, D_MODEL), lambda i: (i, off // D_MODEL))
    full = lambda a: pl.BlockSpec(a.shape, lambda i: (0,) * a.ndim)
    ln_g = ln_g.reshape(1, GM_WIDTH)
    ln_b = ln_b.reshape(1, GM_WIDTH)
    if single_position:
        ws = jnp.repeat(ws[:, 0, 0], GM_CG).reshape(1, GM_WIDTH)
        bs = jnp.repeat(bs[:, 0], GM_CG).reshape(1, GM_WIDTH)
        kernel = _mix_step_kernel
        out_specs = [pl.BlockSpec((tm, D_MODEL), lambda i: (i, 0)),
                     pl.BlockSpec((tm, GM_WIDTH), lambda i: (i, 0))]
        out_shape = [jax.ShapeDtypeStruct((m, D_MODEL), F32), jax.ShapeDtypeStruct((m, GM_WIDTH), F32)]
        scratch = []
    else:
        bs = bs.T
        kernel = _mix_kernel
        out_specs = pl.BlockSpec((tm, D_MODEL), lambda i: (i, 0))
        out_shape = jax.ShapeDtypeStruct((m, D_MODEL), F32)
        scratch = [pltpu.VMEM((tm, GM_WIDTH), BF16)]
    return pl.pallas_call(
        kernel,
        grid=(m // tm,),
        in_specs=[
            pl.BlockSpec((tm, D_MODEL), lambda i: (i, 0)),
            zcol(COL_GU), zcol(COL_GV), zcol(COL_AR), zcol(COL_AG),
            pl.BlockSpec((tm, RET_V), lambda i: (i, 0)),
            full(ln_g), full(ln_b), full(ws), full(bs), full(wret), full(wgm), full(wo),
        ],
        out_specs=out_specs,
        out_shape=out_shape,
        scratch_shapes=scratch,
        compiler_params=_params("parallel"),
        name="mix_step" if single_position else "mix",
    )(x, z, z, z, z, og, ln_g, ln_b, ws, bs, wret, wgm, wo)


def _ffn_kernel(h_ref, g_ref, win_ref, wdown_ref, gf_ref, y_ref, *, final_norm):
    h = h_ref[...]
    hn = (_rms(h) * g_ref[...]).astype(BF16)
    f = _dot(hn, win_ref[...])
    f_gate = f[:, :D_FF]
    f_up = f[:, D_FF:]
    act = (f_gate * jax.nn.sigmoid(f_gate) * f_up).astype(BF16)
    out = h + _dot(act, wdown_ref[...])
    if final_norm:
        out = _rms(out) * gf_ref[...]
    y_ref[...] = out


def _ffn(h, g, win, wdown, g_final, *, tm, final_norm):
    m = h.shape[0]
    full = lambda a: pl.BlockSpec(a.shape, lambda i: (0,) * a.ndim)
    g = g.reshape(1, D_MODEL)
    g_final = g_final.reshape(1, D_MODEL)
    return pl.pallas_call(
        functools.partial(_ffn_kernel, final_norm=final_norm),
        grid=(m // tm,),
        in_specs=[pl.BlockSpec((tm, D_MODEL), lambda i: (i, 0)), full(g), full(win), full(wdown),
                  full(g_final)],
        out_specs=pl.BlockSpec((tm, D_MODEL), lambda i: (i, 0)),
        out_shape=jax.ShapeDtypeStruct((m, D_MODEL), F32),
        compiler_params=_params("parallel"),
        name="ffn",
    )(h, g, win, wdown, g_final)


def _rope_tables(pos):
    inv = ROPE_BASE ** (-jnp.arange(ROPE_HALF, dtype=F32) / ROPE_HALF)
    ang = pos[:, None] * inv[None, :]
    return jnp.cos(ang), jnp.sin(ang)


def kernel(x_prompt, x_sample, state_ret, norm_mix_g, w_in, w_ret_o, gm_ln_g, gm_ln_b, gm_ws, gm_bs,
           w_gm_o, w_o, norm_ffn_g, w_ffn_in, w_ffn_down, norm_final_g):
    batch, seq, _ = x_prompt.shape
    dec_batch, dec_seq, _ = x_sample.shape
    depth = w_in.shape[0]
    assert dec_seq == 1 and seq % 512 == 0

    cos_p, sin_p = _rope_tables(jnp.arange(seq, dtype=F32))
    cos_s, sin_s = _rope_tables(PAST_LEN + jnp.arange(dec_seq, dtype=F32))

    hp = x_prompt.reshape(batch * seq, D_MODEL)
    hs = x_sample.reshape(dec_batch, D_MODEL)
    ret_p, ret_s, gmv_s = [], [], []
    for l in range(depth):
        last = l == depth - 1
        win = w_in[l].astype(BF16)
        wret = w_ret_o[l].astype(BF16)
        wgm = w_gm_o[l].astype(BF16)
        wo = w_o[l].astype(BF16)
        wfin = w_ffn_in[l].astype(BF16)
        wfdown = w_ffn_down[l].astype(BF16)

        z = _in_proj(hp, norm_mix_g[l], win, tm=1024, tn=1024, out_dtype=BF16)
        og, sp = _retention_prompt(z, cos_p, sin_p, batch=batch, seq=seq, tb=512)
        h = _mix(hp, z, og, gm_ln_g[l], gm_ln_b[l], gm_ws[l], gm_bs[l], wret, wgm, wo,
                 tm=512, single_position=False)
        hp = _ffn(h, norm_ffn_g[l], wfin, wfdown, norm_final_g, tm=512, final_norm=last)
        ret_p.append(sp)

        zs = _in_proj(hs, norm_mix_g[l], win, tm=dec_batch, tn=1024, out_dtype=F32)
        ogs, ss = _retention_step(zs, cos_s, sin_s, state_ret[l], nb=2)
        h, vs = _mix(hs, zs, ogs.reshape(dec_batch, RET_V), gm_ln_g[l], gm_ln_b[l], gm_ws[l], gm_bs[l],
                     wret, wgm, wo, tm=dec_batch, single_position=True)
        hs = _ffn(h, norm_ffn_g[l], wfin, wfdown, norm_final_g, tm=dec_batch, final_norm=last)
        ret_s.append(ss)
        gmv_s.append(vs.reshape(dec_batch, dec_seq, GM_WIDTH))

    return (hp.reshape(batch, seq, D_MODEL), hs.reshape(dec_batch, dec_seq, D_MODEL),
            jnp.stack(ret_p), jnp.stack(ret_s), jnp.stack(gmv_s))
```

```python
import functools
import math

import numpy as np
import jax
import jax.numpy as jnp
from jax import lax
from jax.experimental import pallas as pl
from jax.experimental.pallas import tpu as pltpu

D_MODEL = 1024
PAST_LEN = 16384
RET_DK = 256
RET_HEADS = D_MODEL // 256
RET_DV = 2 * RET_DK
RET_QK = RET_HEADS * RET_DK
RET_V = RET_HEADS * RET_DV
RET_CHUNK = 128
ROPE_BASE = 10000.0
ROPE_HALF = RET_DK // 2
GM_GROUPS = 4
GM_WIDTH = D_MODEL
GM_CG = GM_WIDTH // GM_GROUPS
GM_CHUNK = 128
D_FF = ((8 * D_MODEL // 3 + 255) // 256) * 256
EPS = 1e-6
D_IN = 2 * RET_QK + 2 * RET_V + 2 * GM_WIDTH + 2 * D_MODEL

COL_Q = 0
COL_K = RET_QK
COL_V = 2 * RET_QK
COL_G = 2 * RET_QK + RET_V
COL_GU = 2 * RET_QK + 2 * RET_V
COL_GV = COL_GU + GM_WIDTH
COL_AR = COL_GV + GM_WIDTH
COL_AG = COL_AR + D_MODEL

LOG_GAMMA = tuple(float(np.log1p(-np.exp2(np.float32(-5.0 - h)))) for h in range(RET_HEADS))

VMEM_LIMIT_BYTES = 56 * 1024 * 1024

F32 = jnp.float32
BF16 = jnp.bfloat16


def _dot(a, b):
    return jnp.dot(a, b, preferred_element_type=F32)


def _rms(x):
    return x * lax.rsqrt(jnp.mean(x * x, axis=-1, keepdims=True) + EPS)


def _params(*semantics):
    return pltpu.CompilerParams(dimension_semantics=semantics, vmem_limit_bytes=VMEM_LIMIT_BYTES)


def _in_proj_kernel(x_ref, g_ref, w_ref, z_ref, xn_ref):
    @pl.when(pl.program_id(1) == 0)
    def _():
        xn_ref[...] = (_rms(x_ref[...]) * g_ref[...]).astype(BF16)

    z_ref[...] = _dot(xn_ref[...], w_ref[...]).astype(z_ref.dtype)


def _in_proj(x, g, w_bf16, *, tm, tn, out_dtype):
    m = x.shape[0]
    return pl.pallas_call(
        _in_proj_kernel,
        grid=(m // tm, D_IN // tn),
        in_specs=[
            pl.BlockSpec((tm, D_MODEL), lambda i, j: (i, 0)),
            pl.BlockSpec((1, D_MODEL), lambda i, j: (0, 0)),
            pl.BlockSpec((D_MODEL, tn), lambda i, j: (0, j)),
        ],
        out_specs=pl.BlockSpec((tm, tn), lambda i, j: (i, j)),
        out_shape=jax.ShapeDtypeStruct((m, D_IN), out_dtype),
        scratch_shapes=[pltpu.VMEM((tm, D_MODEL), BF16)],
        compiler_params=_params("parallel", "arbitrary"),
        name="in_proj",
    )(x, g.reshape(1, D_MODEL), w_bf16)


def _rotate(x, cos, sin):
    x1 = x[:, :ROPE_HALF]
    x2 = x[:, ROPE_HALF:]
    return jnp.concatenate([x1 * cos - x2 * sin, x1 * sin + x2 * cos], axis=-1)


def _retention_kernel(q_ref, k_ref, v_ref, g_ref, cos_ref, sin_ref, og_ref, st_ref, *, n_chunks):
    L = RET_CHUNK

    @pl.when(pl.program_id(1) == 0)
    def _():
        st_ref[...] = jnp.zeros_like(st_ref)

    n = lax.broadcasted_iota(jnp.int32, (L, 1), 0).astype(F32)
    diff = (lax.broadcasted_iota(jnp.int32, (L, L), 0)
            - lax.broadcasted_iota(jnp.int32, (L, L), 1)).astype(F32)
    intra = [jnp.where(diff >= 0, jnp.exp(jnp.maximum(diff, 0.0) * lg), 0.0) for lg in LOG_GAMMA]
    q_decay = [jnp.exp((n + 1.0) * lg) for lg in LOG_GAMMA]
    k_decay = [jnp.exp((L - 1.0 - n) * lg) for lg in LOG_GAMMA]
    chunk_decay = [math.exp(L * lg) for lg in LOG_GAMMA]

    def chunk(ci, carry):
        rows = pl.ds(pl.multiple_of(ci * L, L), L)
        cos = cos_ref[rows, :]
        sin = sin_ref[rows, :]
        for h in range(RET_HEADS):
            qk_cols = slice(h * RET_DK, (h + 1) * RET_DK)
            v_cols = slice(h * RET_DV, (h + 1) * RET_DV)
            q = _rotate(q_ref[rows, qk_cols].astype(F32), cos, sin)
            k = _rotate(k_ref[rows, qk_cols].astype(F32), cos, sin) * (RET_DK ** -0.5)
            qb = q.astype(BF16)
            v = v_ref[rows, v_cols]
            scores = lax.dot_general(qb, k.astype(BF16), (((1,), (1,)), ((), ())),
                                     preferred_element_type=F32) * intra[h]
            state = st_ref[0, h]
            o = _dot(scores.astype(BF16), v) + _dot(qb, state.astype(BF16)) * q_decay[h]
            st_ref[0, h] = chunk_decay[h] * state + lax.dot_general(
                (k * k_decay[h]).astype(BF16), v, (((0,), (0,)), ((), ())),
                preferred_element_type=F32)
            gate = g_ref[rows, v_cols].astype(F32)
            og_ref[rows, v_cols] = (gate * jax.nn.sigmoid(gate) * _rms(o)).astype(og_ref.dtype)
        return carry

    lax.fori_loop(0, n_chunks, chunk, 0)


def _retention_prompt(z, cos, sin, *, batch, seq, tb):
    nb = seq // tb
    row = lambda b, c: b * nb + c
    return pl.pallas_call(
        functools.partial(_retention_kernel, n_chunks=tb // RET_CHUNK),
        grid=(batch, nb),
        in_specs=[
            pl.BlockSpec((tb, RET_QK), lambda b, c: (row(b, c), COL_Q // RET_QK)),
            pl.BlockSpec((tb, RET_QK), lambda b, c: (row(b, c), COL_K // RET_QK)),
            pl.BlockSpec((tb, RET_V), lambda b, c: (row(b, c), COL_V // RET_V)),
            pl.BlockSpec((tb, RET_V), lambda b, c: (row(b, c), COL_G // RET_V)),
            pl.BlockSpec((tb, ROPE_HALF), lambda b, c: (c, 0)),
            pl.BlockSpec((tb, ROPE_HALF), lambda b, c: (c, 0)),
        ],
        out_specs=[
            pl.BlockSpec((tb, RET_V), lambda b, c: (row(b, c), 0)),
            pl.BlockSpec((1, RET_HEADS, RET_DK, RET_DV), lambda b, c: (b, 0, 0, 0)),
        ],
        out_shape=[
            jax.ShapeDtypeStruct((batch * seq, RET_V), BF16),
            jax.ShapeDtypeStruct((batch, RET_HEADS, RET_DK, RET_DV), F32),
        ],
        compiler_params=_params("parallel", "arbitrary"),
        name="retention_prompt",
    )(z, z, z, z, cos, sin)


def _row_to_col(row):
    n = row.shape[1]
    eye = lax.broadcasted_iota(jnp.int32, (n, n), 0) == lax.broadcasted_iota(jnp.int32, (n, n), 1)
    return jnp.sum(jnp.where(eye, row, 0.0), axis=1, keepdims=True)


def _retention_step_kernel(q_ref, k_ref, v_ref, g_ref, cos_ref, sin_ref, st_ref, og_ref, new_st_ref,
                           *, n_batch):
    cos = cos_ref[...]
    sin = sin_ref[...]
    for b in range(n_batch):
        for h in range(RET_HEADS):
            qk_cols = slice(h * RET_DK, (h + 1) * RET_DK)
            v_cols = slice(h * RET_DV, (h + 1) * RET_DV)
            q = _rotate(q_ref[b, :, qk_cols], cos, sin)
            k = _rotate(k_ref[b, :, qk_cols], cos, sin) * (RET_DK ** -0.5)
            v = v_ref[b, :, v_cols]
            gamma = math.exp(LOG_GAMMA[h])
            new_state = gamma * st_ref[b, h] + _row_to_col(k) * v
            new_st_ref[b, h] = new_state
            o = jnp.sum(_row_to_col(q) * new_state, axis=0, keepdims=True)
            gate = g_ref[b, :, v_cols]
            og_ref[b, :, v_cols] = (gate * jax.nn.sigmoid(gate) * _rms(o)).astype(og_ref.dtype)


def _retention_step(z, cos, sin, state, *, nb):
    batch = z.shape[0]
    z3 = z.reshape(batch, 1, D_IN)
    return pl.pallas_call(
        functools.partial(_retention_step_kernel, n_batch=nb),
        grid=(batch // nb,),
        in_specs=[
            pl.BlockSpec((nb, 1, RET_QK), lambda i: (i, 0, COL_Q // RET_QK)),
            pl.BlockSpec((nb, 1, RET_QK), lambda i: (i, 0, COL_K // RET_QK)),
            pl.BlockSpec((nb, 1, RET_V), lambda i: (i, 0, COL_V // RET_V)),
            pl.BlockSpec((nb, 1, RET_V), lambda i: (i, 0, COL_G // RET_V)),
            pl.BlockSpec((1, ROPE_HALF), lambda i: (0, 0)),
            pl.BlockSpec((1, ROPE_HALF), lambda i: (0, 0)),
            pl.BlockSpec((nb, RET_HEADS, RET_DK, RET_DV), lambda i: (i, 0, 0, 0)),
        ],
        out_specs=[
            pl.BlockSpec((nb, 1, RET_V), lambda i: (i, 0, 0)),
            pl.BlockSpec((nb, RET_HEADS, RET_DK, RET_DV), lambda i: (i, 0, 0, 0)),
        ],
        out_shape=[
            jax.ShapeDtypeStruct((batch, 1, RET_V), F32),
            jax.ShapeDtypeStruct(state.shape, F32),
        ],
        compiler_params=_params("parallel"),
        name="retention_step",
    )(z3, z3, z3, z3, cos, sin, state)


def _gelu_tanh(x):
    return 0.5 * x * (1.0 + jnp.tanh(math.sqrt(2.0 / math.pi) * (x + 0.044715 * (x * x * x))))


def _layer_norm(x, g, b):
    xc = x - jnp.mean(x, axis=-1, keepdims=True)
    return xc * lax.rsqrt(jnp.mean(xc * xc, axis=-1, keepdims=True) + EPS) * g + b


def _merge(x_ref, ar_ref, ag_ref, og_ref, gm, wret_ref, wgm_ref, wo_ref, h_ref):
    branch_ret = _dot(og_ref[...].astype(BF16), wret_ref[...])
    branch_gm = _dot(gm, wgm_ref[...])
    m = (jax.nn.sigmoid(ar_ref[...].astype(F32)) * branch_ret
         + jax.nn.sigmoid(ag_ref[...].astype(F32)) * branch_gm)
    h_ref[...] = x_ref[...] + _dot(m.astype(BF16), wo_ref[...])


def _mix_kernel(x_ref, gu_ref, gv_ref, ar_ref, ag_ref, og_ref, lng_ref, lnb_ref, ws_ref, bs_ref,
                wret_ref, wgm_ref, wo_ref, h_ref, gm_ref):
    L = GM_CHUNK
    causal = lax.broadcasted_iota(jnp.int32, (L, L), 0) >= lax.broadcasted_iota(jnp.int32, (L, L), 1)
    for c in range(x_ref.shape[0] // L):
        rows = slice(c * L, (c + 1) * L)
        u = _gelu_tanh(gu_ref[rows, :].astype(F32))
        v = _layer_norm(_gelu_tanh(gv_ref[rows, :].astype(F32)), lng_ref[...], lnb_ref[...]).astype(BF16)
        for g in range(GM_GROUPS):
            cols = slice(g * GM_CG, (g + 1) * GM_CG)
            w = jnp.where(causal, ws_ref[g], 0.0).astype(BF16)
            mixed = _dot(w, v[:, cols]) + bs_ref[:, g:g + 1]
            gm_ref[rows, cols] = (u[:, cols] * mixed).astype(BF16)
    _merge(x_ref, ar_ref, ag_ref, og_ref, gm_ref[...], wret_ref, wgm_ref, wo_ref, h_ref)


def _mix_step_kernel(x_ref, gu_ref, gv_ref, ar_ref, ag_ref, og_ref, lng_ref, lnb_ref, ws_ref, bs_ref,
                     wret_ref, wgm_ref, wo_ref, h_ref, v_ref):
    u = _gelu_tanh(gu_ref[...])
    v = _layer_norm(_gelu_tanh(gv_ref[...]), lng_ref[...], lnb_ref[...])
    v_ref[...] = v
    gm = (u * (ws_ref[...] * v + bs_ref[...])).astype(BF16)
    _merge(x_ref, ar_ref, ag_ref, og_ref, gm, wret_ref, wgm_ref, wo_ref, h_ref)


def _mix(x, z, og, ln_g, ln_b, ws, bs, wret, wgm, wo, *, tm, single_position):
    m = x.shape[0]
    zcol = lambda off: pl.BlockSpec((tm, D_MODEL), lambda i: (i, off // D_MODEL))
    full = lambda a: pl.BlockSpec(a.shape, lambda i: (0,) * a.ndim)
    ln_g = ln_g.reshape(1, GM_WIDTH)
    ln_b = ln_b.reshape(1, GM_WIDTH)
    if single_position:
        ws = jnp.repeat(ws[:, 0, 0], GM_CG).reshape(1, GM_WIDTH)
        bs = jnp.repeat(bs[:, 0], GM_CG).reshape(1, GM_WIDTH)
        kernel = _mix_step_kernel
        out_specs = [pl.BlockSpec((tm, D_MODEL), lambda i: (i, 0)),
                     pl.BlockSpec((tm, GM_WIDTH), lambda i: (i, 0))]
        out_shape = [jax.ShapeDtypeStruct((m, D_MODEL), F32), jax.ShapeDtypeStruct((m, GM_WIDTH), F32)]
        scratch = []
    else:
        bs = bs.T
        kernel = _mix_kernel
        out_specs = pl.BlockSpec((tm, D_MODEL), lambda i: (i, 0))
        out_shape = jax.ShapeDtypeStruct((m, D_MODEL), F32)
        scratch = [pltpu.VMEM((tm, GM_WIDTH), BF16)]
    return pl.pallas_call(
        kernel,
        grid=(m // tm,),
        in_specs=[
            pl.BlockSpec((tm, D_MODEL), lambda i: (i, 0)),
            zcol(COL_GU), zcol(COL_GV), zcol(COL_AR), zcol(COL_AG),
            pl.BlockSpec((tm, RET_V), lambda i: (i, 0)),
            full(ln_g), full(ln_b), full(ws), full(bs), full(wret), full(wgm), full(wo),
        ],
        out_specs=out_specs,
        out_shape=out_shape,
        scratch_shapes=scratch,
        compiler_params=_params("parallel"),
        name="mix_step" if single_position else "mix",
    )(x, z, z, z, z, og, ln_g, ln_b, ws, bs, wret, wgm, wo)


def _ffn_kernel(h_ref, g_ref, win_ref, wdown_ref, gf_ref, y_ref, *, final_norm):
    h = h_ref[...]
    hn = (_rms(h) * g_ref[...]).astype(BF16)
    f = _dot(hn, win_ref[...])
    f_gate = f[:, :D_FF]
    f_up = f[:, D_FF:]
    act = (f_gate * jax.nn.sigmoid(f_gate) * f_up).astype(BF16)
    out = h + _dot(act, wdown_ref[...])
    if final_norm:
        out = _rms(out) * gf_ref[...]
    y_ref[...] = out


def _ffn(h, g, win, wdown, g_final, *, tm, final_norm):
    m = h.shape[0]
    full = lambda a: pl.BlockSpec(a.shape, lambda i: (0,) * a.ndim)
    g = g.reshape(1, D_MODEL)
    g_final = g_final.reshape(1, D_MODEL)
    return pl.pallas_call(
        functools.partial(_ffn_kernel, final_norm=final_norm),
        grid=(m // tm,),
        in_specs=[pl.BlockSpec((tm, D_MODEL), lambda i: (i, 0)), full(g), full(win), full(wdown),
                  full(g_final)],
        out_specs=pl.BlockSpec((tm, D_MODEL), lambda i: (i, 0)),
        out_shape=jax.ShapeDtypeStruct((m, D_MODEL), F32),
        compiler_params=_params("parallel"),
        name="ffn",
    )(h, g, win, wdown, g_final)


def _rope_tables(pos):
    inv = ROPE_BASE ** (-jnp.arange(ROPE_HALF, dtype=F32) / ROPE_HALF)
    ang = pos[:, None] * inv[None, :]
    return jnp.cos(ang), jnp.sin(ang)


def kernel(x_prompt, x_sample, state_ret, norm_mix_g, w_in, w_ret_o, gm_ln_g, gm_ln_b, gm_ws, gm_bs,
           w_gm_o, w_o, norm_ffn_g, w_ffn_in, w_ffn_down, norm_final_g):
    batch, seq, _ = x_prompt.shape
    dec_batch, dec_seq, _ = x_sample.shape
    depth = w_in.shape[0]
    assert dec_seq == 1 and seq % 512 == 0

    cos_p, sin_p = _rope_tables(jnp.arange(seq, dtype=F32))
    cos_s, sin_s = _rope_tables(PAST_LEN + jnp.arange(dec_seq, dtype=F32))

    hp = x_prompt.reshape(batch * seq, D_MODEL)
    hs = x_sample.reshape(dec_batch, D_MODEL)
    ret_p, ret_s, gmv_s = [], [], []
    for l in range(depth):
        last = l == depth - 1
        win = w_in[l].astype(BF16)
        wret = w_ret_o[l].astype(BF16)
        wgm = w_gm_o[l].astype(BF16)
        wo = w_o[l].astype(BF16)
        wfin = w_ffn_in[l].astype(BF16)
        wfdown = w_ffn_down[l].astype(BF16)

        z = _in_proj(hp, norm_mix_g[l], win, tm=2048, tn=2048, out_dtype=BF16)
        og, sp = _retention_prompt(z, cos_p, sin_p, batch=batch, seq=seq, tb=1024)
        h = _mix(hp, z, og, gm_ln_g[l], gm_ln_b[l], gm_ws[l], gm_bs[l], wret, wgm, wo,
                 tm=512, single_position=False)
        hp = _ffn(h, norm_ffn_g[l], wfin, wfdown, norm_final_g, tm=512, final_norm=last)
        ret_p.append(sp)

        zs = _in_proj(hs, norm_mix_g[l], win, tm=dec_batch, tn=1024, out_dtype=F32)
        ogs, ss = _retention_step(zs, cos_s, sin_s, state_ret[l], nb=2)
        h, vs = _mix(hs, zs, ogs.reshape(dec_batch, RET_V), gm_ln_g[l], gm_ln_b[l], gm_ws[l], gm_bs[l],
                     wret, wgm, wo, tm=dec_batch, single_position=True)
        hs = _ffn(h, norm_ffn_g[l], wfin, wfdown, norm_final_g, tm=dec_batch, final_norm=last)
        ret_s.append(ss)
        gmv_s.append(vs.reshape(dec_batch, dec_seq, GM_WIDTH))

    return (hp.reshape(batch, seq, D_MODEL), hs.reshape(dec_batch, dec_seq, D_MODEL),
            jnp.stack(ret_p), jnp.stack(ret_s), jnp.stack(gmv_s))
```

```python
import functools
import math

import numpy as np
import jax
import jax.numpy as jnp
from jax import lax
from jax.experimental import pallas as pl
from jax.experimental.pallas import tpu as pltpu

D_MODEL = 1024
PAST_LEN = 16384
RET_DK = 256
RET_HEADS = D_MODEL // 256
RET_DV = 2 * RET_DK
RET_QK = RET_HEADS * RET_DK
RET_V = RET_HEADS * RET_DV
RET_CHUNK = 128
ROPE_BASE = 10000.0
ROPE_HALF = RET_DK // 2
GM_GROUPS = 4
GM_WIDTH = D_MODEL
GM_CG = GM_WIDTH // GM_GROUPS
GM_CHUNK = 128
D_FF = ((8 * D_MODEL // 3 + 255) // 256) * 256
EPS = 1e-6
D_IN = 2 * RET_QK + 2 * RET_V + 2 * GM_WIDTH + 2 * D_MODEL

COL_Q = 0
COL_K = RET_QK
COL_V = 2 * RET_QK
COL_G = 2 * RET_QK + RET_V
COL_GU = 2 * RET_QK + 2 * RET_V
COL_GV = COL_GU + GM_WIDTH
COL_AR = COL_GV + GM_WIDTH
COL_AG = COL_AR + D_MODEL

LOG_GAMMA = tuple(float(np.log1p(-np.exp2(np.float32(-5.0 - h)))) for h in range(RET_HEADS))

VMEM_LIMIT_BYTES = 56 * 1024 * 1024

F32 = jnp.float32
BF16 = jnp.bfloat16


def _dot(a, b):
    return jnp.dot(a, b, preferred_element_type=F32)


def _rms(x):
    return x * lax.rsqrt(jnp.mean(x * x, axis=-1, keepdims=True) + EPS)


def _params(*semantics):
    return pltpu.CompilerParams(dimension_semantics=semantics, vmem_limit_bytes=VMEM_LIMIT_BYTES)


def _in_proj_kernel(x_ref, g_ref, w_ref, z_ref, xn_ref):
    @pl.when(pl.program_id(1) == 0)
    def _():
        xn_ref[...] = (_rms(x_ref[...]) * g_ref[...]).astype(BF16)

    z_ref[...] = _dot(xn_ref[...], w_ref[...]).astype(z_ref.dtype)


def _in_proj(x, g, w_bf16, *, col0, ncols, tm, tn, out_dtype):
    m = x.shape[0]
    tile0 = col0 // tn
    assert col0 % tn == 0 and ncols % tn == 0
    return pl.pallas_call(
        _in_proj_kernel,
        grid=(m // tm, ncols // tn),
        in_specs=[
            pl.BlockSpec((tm, D_MODEL), lambda i, j: (i, 0)),
            pl.BlockSpec((1, D_MODEL), lambda i, j: (0, 0)),
            pl.BlockSpec((D_MODEL, tn), lambda i, j: (0, tile0 + j)),
        ],
        out_specs=pl.BlockSpec((tm, tn), lambda i, j: (i, j)),
        out_shape=jax.ShapeDtypeStruct((m, ncols), out_dtype),
        scratch_shapes=[pltpu.VMEM((tm, D_MODEL), BF16)],
        compiler_params=_params("parallel", "arbitrary"),
        name="in_proj",
    )(x, g.reshape(1, D_MODEL), w_bf16)


def _rotate(x, cos, sin):
    x1 = x[:, :ROPE_HALF]
    x2 = x[:, ROPE_HALF:]
    return jnp.concatenate([x1 * cos - x2 * sin, x1 * sin + x2 * cos], axis=-1)


RET_COLS = COL_GU
PR_BLOCK = 512
PR_CHUNKS = PR_BLOCK // RET_CHUNK
PR_TILE = RET_COLS // PR_CHUNKS
assert PR_TILE % RET_DV == 0


def _decay_tables():
    L = RET_CHUNK
    lg = np.array(LOG_GAMMA, np.float32)
    n = np.arange(L, dtype=np.float32)
    diff = n[:, None] - n[None, :]
    intra = np.where(diff[None] >= 0, np.exp(np.maximum(diff, 0.0)[None] * lg[:, None, None]), 0.0)
    q_decay = np.exp((n + 1.0)[None, :, None] * lg[:, None, None])
    k_decay = np.exp((L - 1.0 - n)[None, :, None] * lg[:, None, None])
    return (jnp.asarray(intra, F32), jnp.asarray(np.broadcast_to(q_decay, (RET_HEADS, L, RET_DV)), F32),
            jnp.asarray(np.broadcast_to(k_decay, (RET_HEADS, L, RET_DK)), F32))


def _retention_chunk(z_ref, rows, cos, sin, intra_ref, qd_ref, kd_ref, st_ref, og_ref):
    def zcols(start, width):
        tile, off = divmod(start, PR_TILE)
        return z_ref[tile, rows, off:off + width]

    for h in range(RET_HEADS):
        q = _rotate(zcols(COL_Q + h * RET_DK, RET_DK).astype(F32), cos, sin)
        k = _rotate(zcols(COL_K + h * RET_DK, RET_DK).astype(F32), cos, sin) * (RET_DK ** -0.5)
        v = zcols(COL_V + h * RET_DV, RET_DV)
        qb = q.astype(BF16)
        scores = lax.dot_general(qb, k.astype(BF16), (((1,), (1,)), ((), ())),
                                 preferred_element_type=F32) * intra_ref[h]
        state = st_ref[0, h]
        o = _dot(scores.astype(BF16), v) + _dot(qb, state.astype(BF16)) * qd_ref[h]
        st_ref[0, h] = math.exp(RET_CHUNK * LOG_GAMMA[h]) * state + lax.dot_general(
            (k * kd_ref[h]).astype(BF16), v, (((0,), (0,)), ((), ())), preferred_element_type=F32)
        gate = zcols(COL_G + h * RET_DV, RET_DV).astype(F32)
        og_ref[:, h * RET_DV:(h + 1) * RET_DV] = (gate * jax.nn.sigmoid(gate) * _rms(o)).astype(og_ref.dtype)


def _proj_ret_kernel(x_ref, g_ref, w_ref, cos_ref, sin_ref, intra_ref, qd_ref, kd_ref, og_ref, st_ref,
                     xn_ref, za_ref, zb_ref, *, blocks_per_seq):
    t = pl.program_id(0)
    j = pl.program_id(1)
    ret_block = jnp.maximum(t - 1, 0)

    @pl.when(j == 0)
    def _():
        xn_ref[...] = (_rms(x_ref[...]) * g_ref[...]).astype(BF16)

    @pl.when(jnp.logical_and(t == 0, j == 0))
    def _():
        zb_ref[...] = jnp.zeros_like(zb_ref)

    @pl.when(jnp.logical_and(ret_block % blocks_per_seq == 0, j == 0))
    def _():
        st_ref[...] = jnp.zeros_like(st_ref)

    rows = pl.ds(pl.multiple_of(j * RET_CHUNK, RET_CHUNK), RET_CHUNK)

    def step(z_write, z_read):
        z_write[j] = _dot(xn_ref[...], w_ref[...]).astype(BF16)
        _retention_chunk(z_read, rows, cos_ref[...], sin_ref[...], intra_ref, qd_ref, kd_ref, st_ref, og_ref)

    @pl.when(t % 2 == 0)
    def _():
        step(za_ref, zb_ref)

    @pl.when(t % 2 == 1)
    def _():
        step(zb_ref, za_ref)


def _proj_ret(x, g, w_bf16, cos, sin, *, batch, seq):
    n_blocks = batch * seq // PR_BLOCK
    blocks_per_seq = seq // PR_BLOCK
    intra, q_decay, k_decay = _decay_tables()
    full = lambda a: pl.BlockSpec(a.shape, lambda t, j: (0,) * a.ndim)
    proj_block = lambda t: jnp.minimum(t, n_blocks - 1)
    ret_block = lambda t: jnp.maximum(t - 1, 0)
    chunk_of_seq = lambda t, j: (ret_block(t) % blocks_per_seq) * PR_CHUNKS + j
    return pl.pallas_call(
        functools.partial(_proj_ret_kernel, blocks_per_seq=blocks_per_seq),
        grid=(n_blocks + 1, PR_CHUNKS),
        in_specs=[
            pl.BlockSpec((PR_BLOCK, D_MODEL), lambda t, j: (proj_block(t), 0)),
            pl.BlockSpec((1, D_MODEL), lambda t, j: (0, 0)),
            pl.BlockSpec((D_MODEL, PR_TILE), lambda t, j: (0, j)),
            pl.BlockSpec((RET_CHUNK, ROPE_HALF), lambda t, j: (chunk_of_seq(t, j), 0)),
            pl.BlockSpec((RET_CHUNK, ROPE_HALF), lambda t, j: (chunk_of_seq(t, j), 0)),
            full(intra), full(q_decay), full(k_decay),
        ],
        out_specs=[
            pl.BlockSpec((RET_CHUNK, RET_V), lambda t, j: (ret_block(t) * PR_CHUNKS + j, 0)),
            pl.BlockSpec((1, RET_HEADS, RET_DK, RET_DV), lambda t, j: (ret_block(t) // blocks_per_seq, 0, 0, 0)),
        ],
        out_shape=[
            jax.ShapeDtypeStruct((batch * seq, RET_V), BF16),
            jax.ShapeDtypeStruct((batch, RET_HEADS, RET_DK, RET_DV), F32),
        ],
        scratch_shapes=[
            pltpu.VMEM((PR_BLOCK, D_MODEL), BF16),
            pltpu.VMEM((PR_CHUNKS, PR_BLOCK, PR_TILE), BF16),
            pltpu.VMEM((PR_CHUNKS, PR_BLOCK, PR_TILE), BF16),
        ],
        compiler_params=_params("arbitrary", "arbitrary"),
        name="proj_ret",
    )(x, g.reshape(1, D_MODEL), w_bf16, cos, sin, intra, q_decay, k_decay)


def _row_to_col(row):
    n = row.shape[1]
    eye = lax.broadcasted_iota(jnp.int32, (n, n), 0) == lax.broadcasted_iota(jnp.int32, (n, n), 1)
    return jnp.sum(jnp.where(eye, row, 0.0), axis=1, keepdims=True)


def _retention_step_kernel(q_ref, k_ref, v_ref, g_ref, cos_ref, sin_ref, st_ref, og_ref, new_st_ref,
                           *, n_batch):
    cos = cos_ref[...]
    sin = sin_ref[...]
    for b in range(n_batch):
        for h in range(RET_HEADS):
            qk_cols = slice(h * RET_DK, (h + 1) * RET_DK)
            v_cols = slice(h * RET_DV, (h + 1) * RET_DV)
            q = _rotate(q_ref[b, :, qk_cols], cos, sin)
            k = _rotate(k_ref[b, :, qk_cols], cos, sin) * (RET_DK ** -0.5)
            v = v_ref[b, :, v_cols]
            gamma = math.exp(LOG_GAMMA[h])
            new_state = gamma * st_ref[b, h] + _row_to_col(k) * v
            new_st_ref[b, h] = new_state
            o = jnp.sum(_row_to_col(q) * new_state, axis=0, keepdims=True)
            gate = g_ref[b, :, v_cols]
            og_ref[b, :, v_cols] = (gate * jax.nn.sigmoid(gate) * _rms(o)).astype(og_ref.dtype)


def _retention_step(z, cos, sin, state, *, nb):
    batch = z.shape[0]
    z3 = z.reshape(batch, 1, D_IN)
    return pl.pallas_call(
        functools.partial(_retention_step_kernel, n_batch=nb),
        grid=(batch // nb,),
        in_specs=[
            pl.BlockSpec((nb, 1, RET_QK), lambda i: (i, 0, COL_Q // RET_QK)),
            pl.BlockSpec((nb, 1, RET_QK), lambda i: (i, 0, COL_K // RET_QK)),
            pl.BlockSpec((nb, 1, RET_V), lambda i: (i, 0, COL_V // RET_V)),
            pl.BlockSpec((nb, 1, RET_V), lambda i: (i, 0, COL_G // RET_V)),
            pl.BlockSpec((1, ROPE_HALF), lambda i: (0, 0)),
            pl.BlockSpec((1, ROPE_HALF), lambda i: (0, 0)),
            pl.BlockSpec((nb, RET_HEADS, RET_DK, RET_DV), lambda i: (i, 0, 0, 0)),
        ],
        out_specs=[
            pl.BlockSpec((nb, 1, RET_V), lambda i: (i, 0, 0)),
            pl.BlockSpec((nb, RET_HEADS, RET_DK, RET_DV), lambda i: (i, 0, 0, 0)),
        ],
        out_shape=[
            jax.ShapeDtypeStruct((batch, 1, RET_V), F32),
            jax.ShapeDtypeStruct(state.shape, F32),
        ],
        compiler_params=_params("parallel"),
        name="retention_step",
    )(z3, z3, z3, z3, cos, sin, state)


def _gelu_tanh(x):
    return 0.5 * x * (1.0 + jnp.tanh(math.sqrt(2.0 / math.pi) * (x + 0.044715 * (x * x * x))))


def _layer_norm(x, g, b):
    xc = x - jnp.mean(x, axis=-1, keepdims=True)
    return xc * lax.rsqrt(jnp.mean(xc * xc, axis=-1, keepdims=True) + EPS) * g + b


def _merge(x_ref, ar_ref, ag_ref, og_ref, gm, wret_ref, wgm_ref, wo_ref, h_ref):
    branch_ret = _dot(og_ref[...].astype(BF16), wret_ref[...])
    branch_gm = _dot(gm, wgm_ref[...])
    m = (jax.nn.sigmoid(ar_ref[...].astype(F32)) * branch_ret
         + jax.nn.sigmoid(ag_ref[...].astype(F32)) * branch_gm)
    h_ref[...] = x_ref[...] + _dot(m.astype(BF16), wo_ref[...])


def _mix_kernel(x_ref, gu_ref, gv_ref, ar_ref, ag_ref, og_ref, lng_ref, lnb_ref, ws_ref, bs_ref,
                wret_ref, wgm_ref, wo_ref, h_ref, gm_ref):
    L = GM_CHUNK
    causal = lax.broadcasted_iota(jnp.int32, (L, L), 0) >= lax.broadcasted_iota(jnp.int32, (L, L), 1)
    for c in range(x_ref.shape[0] // L):
        rows = slice(c * L, (c + 1) * L)
        u = _gelu_tanh(gu_ref[rows, :].astype(F32))
        v = _layer_norm(_gelu_tanh(gv_ref[rows, :].astype(F32)), lng_ref[...], lnb_ref[...]).astype(BF16)
        for g in range(GM_GROUPS):
            cols = slice(g * GM_CG, (g + 1) * GM_CG)
            w = jnp.where(causal, ws_ref[g], 0.0).astype(BF16)
            mixed = _dot(w, v[:, cols]) + bs_ref[:, g:g + 1]
            gm_ref[rows, cols] = (u[:, cols] * mixed).astype(BF16)
    _merge(x_ref, ar_ref, ag_ref, og_ref, gm_ref[...], wret_ref, wgm_ref, wo_ref, h_ref)


def _mix_step_kernel(x_ref, gu_ref, gv_ref, ar_ref, ag_ref, og_ref, lng_ref, lnb_ref, ws_ref, bs_ref,
                     wret_ref, wgm_ref, wo_ref, h_ref, v_ref):
    u = _gelu_tanh(gu_ref[...])
    v = _layer_norm(_gelu_tanh(gv_ref[...]), lng_ref[...], lnb_ref[...])
    v_ref[...] = v
    gm = (u * (ws_ref[...] * v + bs_ref[...])).astype(BF16)
    _merge(x_ref, ar_ref, ag_ref, og_ref, gm, wret_ref, wgm_ref, wo_ref, h_ref)


def _mix(x, z, og, ln_g, ln_b, ws, bs, wret, wgm, wo, *, z_col0, tm, single_position):
    m = x.shape[0]
    zcol = lambda off: pl.BlockSpec((tm, D_MODEL), lambda i: (i, (z_col0 + off - COL_GU) // D_MODEL))
    full = lambda a: pl.BlockSpec(a.shape, lambda i: (0,) * a.ndim)
    ln_g = ln_g.reshape(1, GM_WIDTH)
    ln_b = ln_b.reshape(1, GM_WIDTH)
    if single_position:
        ws = jnp.repeat(ws[:, 0, 0], GM_CG).reshape(1, GM_WIDTH)
        bs = jnp.repeat(bs[:, 0], GM_CG).reshape(1, GM_WIDTH)
        kernel = _mix_step_kernel
        out_specs = [pl.BlockSpec((tm, D_MODEL), lambda i: (i, 0)),
                     pl.BlockSpec((tm, GM_WIDTH), lambda i: (i, 0))]
        out_shape = [jax.ShapeDtypeStruct((m, D_MODEL), F32), jax.ShapeDtypeStruct((m, GM_WIDTH), F32)]
        scratch = []
    else:
        bs = bs.T
        kernel = _mix_kernel
        out_specs = pl.BlockSpec((tm, D_MODEL), lambda i: (i, 0))
        out_shape = jax.ShapeDtypeStruct((m, D_MODEL), F32)
        scratch = [pltpu.VMEM((tm, GM_WIDTH), BF16)]
    return pl.pallas_call(
        kernel,
        grid=(m // tm,),
        in_specs=[
            pl.BlockSpec((tm, D_MODEL), lambda i: (i, 0)),
            zcol(COL_GU), zcol(COL_GV), zcol(COL_AR), zcol(COL_AG),
            pl.BlockSpec((tm, RET_V), lambda i: (i, 0)),
            full(ln_g), full(ln_b), full(ws), full(bs), full(wret), full(wgm), full(wo),
        ],
        out_specs=out_specs,
        out_shape=out_shape,
        scratch_shapes=scratch,
        compiler_params=_params("parallel"),
        name="mix_step" if single_position else "mix",
    )(x, z, z, z, z, og, ln_g, ln_b, ws, bs, wret, wgm, wo)


def _ffn_kernel(h_ref, g_ref, win_ref, wdown_ref, gf_ref, y_ref, *, final_norm):
    h = h_ref[...]
    hn = (_rms(h) * g_ref[...]).astype(BF16)
    f = _dot(hn, win_ref[...])
    f_gate = f[:, :D_FF]
    f_up = f[:, D_FF:]
    act = (f_gate * jax.nn.sigmoid(f_gate) * f_up).astype(BF16)
    out = h + _dot(act, wdown_ref[...])
    if final_norm:
        out = _rms(out) * gf_ref[...]
    y_ref[...] = out


def _ffn(h, g, win, wdown, g_final, *, tm, final_norm):
    m = h.shape[0]
    full = lambda a: pl.BlockSpec(a.shape, lambda i: (0,) * a.ndim)
    g = g.reshape(1, D_MODEL)
    g_final = g_final.reshape(1, D_MODEL)
    return pl.pallas_call(
        functools.partial(_ffn_kernel, final_norm=final_norm),
        grid=(m // tm,),
        in_specs=[pl.BlockSpec((tm, D_MODEL), lambda i: (i, 0)), full(g), full(win), full(wdown),
                  full(g_final)],
        out_specs=pl.BlockSpec((tm, D_MODEL), lambda i: (i, 0)),
        out_shape=jax.ShapeDtypeStruct((m, D_MODEL), F32),
        compiler_params=_params("parallel"),
        name="ffn",
    )(h, g, win, wdown, g_final)


def _rope_tables(pos):
    inv = ROPE_BASE ** (-jnp.arange(ROPE_HALF, dtype=F32) / ROPE_HALF)
    ang = pos[:, None] * inv[None, :]
    return jnp.cos(ang), jnp.sin(ang)


def kernel(x_prompt, x_sample, state_ret, norm_mix_g, w_in, w_ret_o, gm_ln_g, gm_ln_b, gm_ws, gm_bs,
           w_gm_o, w_o, norm_ffn_g, w_ffn_in, w_ffn_down, norm_final_g):
    batch, seq, _ = x_prompt.shape
    dec_batch, dec_seq, _ = x_sample.shape
    depth = w_in.shape[0]
    assert dec_seq == 1 and seq % PR_BLOCK == 0

    cos_p, sin_p = _rope_tables(jnp.arange(seq, dtype=F32))
    cos_s, sin_s = _rope_tables(PAST_LEN + jnp.arange(dec_seq, dtype=F32))

    hp = x_prompt.reshape(batch * seq, D_MODEL)
    hs = x_sample.reshape(dec_batch, D_MODEL)
    ret_p, ret_s, gmv_s = [], [], []
    for l in range(depth):
        last = l == depth - 1
        win = w_in[l].astype(BF16)
        wret = w_ret_o[l].astype(BF16)
        wgm = w_gm_o[l].astype(BF16)
        wo = w_o[l].astype(BF16)
        wfin = w_ffn_in[l].astype(BF16)
        wfdown = w_ffn_down[l].astype(BF16)

        og, sp = _proj_ret(hp, norm_mix_g[l], win, cos_p, sin_p, batch=batch, seq=seq)
        z = _in_proj(hp, norm_mix_g[l], win, col0=RET_COLS, ncols=D_IN - RET_COLS, tm=2048, tn=2048,
                     out_dtype=BF16)
        h = _mix(hp, z, og, gm_ln_g[l], gm_ln_b[l], gm_ws[l], gm_bs[l], wret, wgm, wo,
                 z_col0=0, tm=512, single_position=False)
        hp = _ffn(h, norm_ffn_g[l], wfin, wfdown, norm_final_g, tm=512, final_norm=last)
        ret_p.append(sp)

        zs = _in_proj(hs, norm_mix_g[l], win, col0=0, ncols=D_IN, tm=dec_batch, tn=1024, out_dtype=F32)
        ogs, ss = _retention_step(zs, cos_s, sin_s, state_ret[l], nb=2)
        h, vs = _mix(hs, zs, ogs.reshape(dec_batch, RET_V), gm_ln_g[l], gm_ln_b[l], gm_ws[l], gm_bs[l],
                     wret, wgm, wo, z_col0=COL_GU, tm=dec_batch, single_position=True)
        hs = _ffn(h, norm_ffn_g[l], wfin, wfdown, norm_final_g, tm=dec_batch, final_norm=last)
        ret_s.append(ss)
        gmv_s.append(vs.reshape(dec_batch, dec_seq, GM_WIDTH))

    return (hp.reshape(batch, seq, D_MODEL), hs.reshape(dec_batch, dec_seq, D_MODEL),
            jnp.stack(ret_p), jnp.stack(ret_s), jnp.stack(gmv_s))
```

```python
import functools
import math

import numpy as np
import jax
import jax.numpy as jnp
from jax import lax
from jax.experimental import pallas as pl
from jax.experimental.pallas import tpu as pltpu

D_MODEL = 1024
PAST_LEN = 16384
RET_DK = 256
RET_HEADS = D_MODEL // 256
RET_DV = 2 * RET_DK
RET_QK = RET_HEADS * RET_DK
RET_V = RET_HEADS * RET_DV
RET_CHUNK = 128
ROPE_BASE = 10000.0
ROPE_HALF = RET_DK // 2
GM_GROUPS = 4
GM_WIDTH = D_MODEL
GM_CG = GM_WIDTH // GM_GROUPS
GM_CHUNK = 128
D_FF = ((8 * D_MODEL // 3 + 255) // 256) * 256
EPS = 1e-6
D_IN = 2 * RET_QK + 2 * RET_V + 2 * GM_WIDTH + 2 * D_MODEL

COL_Q = 0
COL_K = RET_QK
COL_V = 2 * RET_QK
COL_G = 2 * RET_QK + RET_V
COL_GU = 2 * RET_QK + 2 * RET_V
COL_GV = COL_GU + GM_WIDTH
COL_AR = COL_GV + GM_WIDTH
COL_AG = COL_AR + D_MODEL

LOG_GAMMA = tuple(float(np.log1p(-np.exp2(np.float32(-5.0 - h)))) for h in range(RET_HEADS))

VMEM_LIMIT_BYTES = 56 * 1024 * 1024
MXU_COLS = 256

F32 = jnp.float32
BF16 = jnp.bfloat16


def _dot(a, b):
    return jnp.dot(a, b, preferred_element_type=F32)


def _rms(x):
    return x * lax.rsqrt(jnp.mean(x * x, axis=-1, keepdims=True) + EPS)


def _params(*semantics):
    return pltpu.CompilerParams(dimension_semantics=semantics, vmem_limit_bytes=VMEM_LIMIT_BYTES)


def _in_proj_kernel(x_ref, g_ref, w_ref, z_ref, xn_ref):
    @pl.when(pl.program_id(1) == 0)
    def _():
        xn_ref[...] = (_rms(x_ref[...]) * g_ref[...]).astype(BF16)

    z_ref[...] = _dot(xn_ref[...], w_ref[...]).astype(z_ref.dtype)


def _in_proj(x, g, w_bf16, *, col0, ncols, tm, tn, out_dtype):
    m = x.shape[0]
    tile0 = col0 // tn
    assert col0 % tn == 0 and ncols % tn == 0
    return pl.pallas_call(
        _in_proj_kernel,
        grid=(m // tm, ncols // tn),
        in_specs=[
            pl.BlockSpec((tm, D_MODEL), lambda i, j: (i, 0)),
            pl.BlockSpec((1, D_MODEL), lambda i, j: (0, 0)),
            pl.BlockSpec((D_MODEL, tn), lambda i, j: (0, tile0 + j)),
        ],
        out_specs=pl.BlockSpec((tm, tn), lambda i, j: (i, j)),
        out_shape=jax.ShapeDtypeStruct((m, ncols), out_dtype),
        scratch_shapes=[pltpu.VMEM((tm, D_MODEL), BF16)],
        compiler_params=_params("parallel", "arbitrary"),
        name="in_proj",
    )(x, g.reshape(1, D_MODEL), w_bf16)


def _rotate(x, cos, sin):
    x1 = x[:, :ROPE_HALF]
    x2 = x[:, ROPE_HALF:]
    return jnp.concatenate([x1 * cos - x2 * sin, x1 * sin + x2 * cos], axis=-1)


RET_COLS = COL_GU
PR_BLOCK = 512
PR_CHUNKS = PR_BLOCK // RET_CHUNK
PR_TILE = RET_COLS // PR_CHUNKS
assert PR_TILE % RET_DV == 0


def _decay_tables():
    L = RET_CHUNK
    lg = np.array(LOG_GAMMA, np.float32)
    n = np.arange(L, dtype=np.float32)
    diff = n[:, None] - n[None, :]
    intra = np.where(diff[None] >= 0, np.exp(np.maximum(diff, 0.0)[None] * lg[:, None, None]), 0.0)
    q_decay = np.exp((n + 1.0)[None, :, None] * lg[:, None, None])
    k_decay = np.exp((L - 1.0 - n)[None, :, None] * lg[:, None, None])
    return (jnp.asarray(intra, F32), jnp.asarray(np.broadcast_to(q_decay, (RET_HEADS, L, RET_DV)), F32),
            jnp.asarray(np.broadcast_to(k_decay, (RET_HEADS, L, RET_DK)), F32))


def _retention_chunk(z_ref, rows, cos, sin, intra_ref, qd_ref, kd_ref, st_ref, og_ref, before_head):
    def zcols(start, width):
        tile, off = divmod(start, PR_TILE)
        return z_ref[tile, rows, off:off + width]

    for h in range(RET_HEADS):
        before_head(h)
        q = _rotate(zcols(COL_Q + h * RET_DK, RET_DK).astype(F32), cos, sin)
        k = _rotate(zcols(COL_K + h * RET_DK, RET_DK).astype(F32), cos, sin) * (RET_DK ** -0.5)
        v = zcols(COL_V + h * RET_DV, RET_DV)
        qb = q.astype(BF16)
        scores = lax.dot_general(qb, k.astype(BF16), (((1,), (1,)), ((), ())),
                                 preferred_element_type=F32) * intra_ref[h]
        state = st_ref[0, h]
        o = _dot(scores.astype(BF16), v) + _dot(qb, state.astype(BF16)) * qd_ref[h]
        st_ref[0, h] = math.exp(RET_CHUNK * LOG_GAMMA[h]) * state + lax.dot_general(
            (k * kd_ref[h]).astype(BF16), v, (((0,), (0,)), ((), ())), preferred_element_type=F32)
        gate = zcols(COL_G + h * RET_DV, RET_DV).astype(F32)
        og_ref[:, h * RET_DV:(h + 1) * RET_DV] = (gate * jax.nn.sigmoid(gate) * _rms(o)).astype(og_ref.dtype)


def _proj_ret_kernel(x_ref, g_ref, w_ref, cos_ref, sin_ref, intra_ref, qd_ref, kd_ref, og_ref, st_ref,
                     xn_ref, za_ref, zb_ref, *, blocks_per_seq):
    t = pl.program_id(0)
    j = pl.program_id(1)
    ret_block = jnp.maximum(t - 1, 0)

    @pl.when(j == 0)
    def _():
        xn_ref[...] = (_rms(x_ref[...]) * g_ref[...]).astype(BF16)

    @pl.when(jnp.logical_and(t == 0, j == 0))
    def _():
        zb_ref[...] = jnp.zeros_like(zb_ref)

    @pl.when(jnp.logical_and(ret_block % blocks_per_seq == 0, j == 0))
    def _():
        st_ref[...] = jnp.zeros_like(st_ref)

    rows = pl.ds(pl.multiple_of(j * RET_CHUNK, RET_CHUNK), RET_CHUNK)

    def step(z_write, z_read):
        n_pieces = PR_TILE // MXU_COLS

        def project_pieces(h):
            for p in range(h * n_pieces // RET_HEADS, (h + 1) * n_pieces // RET_HEADS):
                cols = slice(p * MXU_COLS, (p + 1) * MXU_COLS)
                z_write[j, :, cols] = _dot(xn_ref[...], w_ref[:, cols]).astype(BF16)

        _retention_chunk(z_read, rows, cos_ref[...], sin_ref[...], intra_ref, qd_ref, kd_ref, st_ref, og_ref,
                         project_pieces)

    @pl.when(t % 2 == 0)
    def _():
        step(za_ref, zb_ref)

    @pl.when(t % 2 == 1)
    def _():
        step(zb_ref, za_ref)


def _proj_ret(x, g, w_bf16, cos, sin, *, batch, seq):
    n_blocks = batch * seq // PR_BLOCK
    blocks_per_seq = seq // PR_BLOCK
    intra, q_decay, k_decay = _decay_tables()
    full = lambda a: pl.BlockSpec(a.shape, lambda t, j: (0,) * a.ndim)
    proj_block = lambda t: jnp.minimum(t, n_blocks - 1)
    ret_block = lambda t: jnp.maximum(t - 1, 0)
    chunk_of_seq = lambda t, j: (ret_block(t) % blocks_per_seq) * PR_CHUNKS + j
    og_block = lambda t: jnp.where(t == 0, n_blocks, t - 1)
    return pl.pallas_call(
        functools.partial(_proj_ret_kernel, blocks_per_seq=blocks_per_seq),
        grid=(n_blocks + 1, PR_CHUNKS),
        in_specs=[
            pl.BlockSpec((PR_BLOCK, D_MODEL), lambda t, j: (proj_block(t), 0)),
            pl.BlockSpec((1, D_MODEL), lambda t, j: (0, 0)),
            pl.BlockSpec((D_MODEL, PR_TILE), lambda t, j: (0, j)),
            pl.BlockSpec((RET_CHUNK, ROPE_HALF), lambda t, j: (chunk_of_seq(t, j), 0)),
            pl.BlockSpec((RET_CHUNK, ROPE_HALF), lambda t, j: (chunk_of_seq(t, j), 0)),
            full(intra), full(q_decay), full(k_decay),
        ],
        out_specs=[
            pl.BlockSpec((RET_CHUNK, RET_V), lambda t, j: (og_block(t) * PR_CHUNKS + j, 0)),
            pl.BlockSpec((1, RET_HEADS, RET_DK, RET_DV), lambda t, j: (ret_block(t) // blocks_per_seq, 0, 0, 0)),
        ],
        out_shape=[
            jax.ShapeDtypeStruct(((n_blocks + 1) * PR_BLOCK, RET_V), BF16),
            jax.ShapeDtypeStruct((batch, RET_HEADS, RET_DK, RET_DV), F32),
        ],
        scratch_shapes=[
            pltpu.VMEM((PR_BLOCK, D_MODEL), BF16),
            pltpu.VMEM((PR_CHUNKS, PR_BLOCK, PR_TILE), BF16),
            pltpu.VMEM((PR_CHUNKS, PR_BLOCK, PR_TILE), BF16),
        ],
        compiler_params=_params("arbitrary", "arbitrary"),
        name="proj_ret",
    )(x, g.reshape(1, D_MODEL), w_bf16, cos, sin, intra, q_decay, k_decay)


def _row_to_col(row):
    n = row.shape[1]
    eye = lax.broadcasted_iota(jnp.int32, (n, n), 0) == lax.broadcasted_iota(jnp.int32, (n, n), 1)
    return jnp.sum(jnp.where(eye, row, 0.0), axis=1, keepdims=True)


def _retention_step_kernel(q_ref, k_ref, v_ref, g_ref, cos_ref, sin_ref, st_ref, og_ref, new_st_ref,
                           *, n_batch):
    cos = cos_ref[...]
    sin = sin_ref[...]
    for b in range(n_batch):
        for h in range(RET_HEADS):
            qk_cols = slice(h * RET_DK, (h + 1) * RET_DK)
            v_cols = slice(h * RET_DV, (h + 1) * RET_DV)
            q = _rotate(q_ref[b, :, qk_cols], cos, sin)
            k = _rotate(k_ref[b, :, qk_cols], cos, sin) * (RET_DK ** -0.5)
            v = v_ref[b, :, v_cols]
            gamma = math.exp(LOG_GAMMA[h])
            new_state = gamma * st_ref[b, h] + _row_to_col(k) * v
            new_st_ref[b, h] = new_state
            o = jnp.sum(_row_to_col(q) * new_state, axis=0, keepdims=True)
            gate = g_ref[b, :, v_cols]
            og_ref[b, :, v_cols] = (gate * jax.nn.sigmoid(gate) * _rms(o)).astype(og_ref.dtype)


def _retention_step(z, cos, sin, state, *, nb):
    batch = z.shape[0]
    z3 = z.reshape(batch, 1, D_IN)
    return pl.pallas_call(
        functools.partial(_retention_step_kernel, n_batch=nb),
        grid=(batch // nb,),
        in_specs=[
            pl.BlockSpec((nb, 1, RET_QK), lambda i: (i, 0, COL_Q // RET_QK)),
            pl.BlockSpec((nb, 1, RET_QK), lambda i: (i, 0, COL_K // RET_QK)),
            pl.BlockSpec((nb, 1, RET_V), lambda i: (i, 0, COL_V // RET_V)),
            pl.BlockSpec((nb, 1, RET_V), lambda i: (i, 0, COL_G // RET_V)),
            pl.BlockSpec((1, ROPE_HALF), lambda i: (0, 0)),
            pl.BlockSpec((1, ROPE_HALF), lambda i: (0, 0)),
            pl.BlockSpec((nb, RET_HEADS, RET_DK, RET_DV), lambda i: (i, 0, 0, 0)),
        ],
        out_specs=[
            pl.BlockSpec((nb, 1, RET_V), lambda i: (i, 0, 0)),
            pl.BlockSpec((nb, RET_HEADS, RET_DK, RET_DV), lambda i: (i, 0, 0, 0)),
        ],
        out_shape=[
            jax.ShapeDtypeStruct((batch, 1, RET_V), F32),
            jax.ShapeDtypeStruct(state.shape, F32),
        ],
        compiler_params=_params("parallel"),
        name="retention_step",
    )(z3, z3, z3, z3, cos, sin, state)


def _gelu_tanh(x):
    return 0.5 * x * (1.0 + jnp.tanh(math.sqrt(2.0 / math.pi) * (x + 0.044715 * (x * x * x))))


def _layer_norm(x, g, b):
    xc = x - jnp.mean(x, axis=-1, keepdims=True)
    return xc * lax.rsqrt(jnp.mean(xc * xc, axis=-1, keepdims=True) + EPS) * g + b


def _merge(x_ref, ar_ref, ag_ref, og_ref, gm, wret_ref, wgm_ref, wo_ref, h_ref):
    branch_ret = _dot(og_ref[...].astype(BF16), wret_ref[...])
    branch_gm = _dot(gm, wgm_ref[...])
    m = (jax.nn.sigmoid(ar_ref[...].astype(F32)) * branch_ret
         + jax.nn.sigmoid(ag_ref[...].astype(F32)) * branch_gm)
    h_ref[...] = x_ref[...] + _dot(m.astype(BF16), wo_ref[...])


def _mix_kernel(x_ref, gu_ref, gv_ref, ar_ref, ag_ref, og_ref, lng_ref, lnb_ref, ws_ref, bs_ref,
                wret_ref, wgm_ref, wo_ref, h_ref, gm_ref):
    L = GM_CHUNK
    causal = lax.broadcasted_iota(jnp.int32, (L, L), 0) >= lax.broadcasted_iota(jnp.int32, (L, L), 1)
    for c in range(x_ref.shape[0] // L):
        rows = slice(c * L, (c + 1) * L)
        u = _gelu_tanh(gu_ref[rows, :].astype(F32))
        v = _layer_norm(_gelu_tanh(gv_ref[rows, :].astype(F32)), lng_ref[...], lnb_ref[...]).astype(BF16)
        for g in range(GM_GROUPS):
            cols = slice(g * GM_CG, (g + 1) * GM_CG)
            w = jnp.where(causal, ws_ref[g], 0.0).astype(BF16)
            mixed = _dot(w, v[:, cols]) + bs_ref[:, g:g + 1]
            gm_ref[rows, cols] = (u[:, cols] * mixed).astype(BF16)
    _merge(x_ref, ar_ref, ag_ref, og_ref, gm_ref[...], wret_ref, wgm_ref, wo_ref, h_ref)


def _mix_step_kernel(x_ref, gu_ref, gv_ref, ar_ref, ag_ref, og_ref, lng_ref, lnb_ref, ws_ref, bs_ref,
                     wret_ref, wgm_ref, wo_ref, h_ref, v_ref):
    u = _gelu_tanh(gu_ref[...])
    v = _layer_norm(_gelu_tanh(gv_ref[...]), lng_ref[...], lnb_ref[...])
    v_ref[...] = v
    gm = (u * (ws_ref[...] * v + bs_ref[...])).astype(BF16)
    _merge(x_ref, ar_ref, ag_ref, og_ref, gm, wret_ref, wgm_ref, wo_ref, h_ref)


def _mix(x, z, og, ln_g, ln_b, ws, bs, wret, wgm, wo, *, z_col0, tm, single_position):
    m = x.shape[0]
    zcol = lambda off: pl.BlockSpec((tm, D_MODEL), lambda i: (i, (z_col0 + off - COL_GU) // D_MODEL))
    full = lambda a: pl.BlockSpec(a.shape, lambda i: (0,) * a.ndim)
    ln_g = ln_g.reshape(1, GM_WIDTH)
    ln_b = ln_b.reshape(1, GM_WIDTH)
    if single_position:
        ws = jnp.repeat(ws[:, 0, 0], GM_CG).reshape(1, GM_WIDTH)
        bs = jnp.repeat(bs[:, 0], GM_CG).reshape(1, GM_WIDTH)
        kernel = _mix_step_kernel
        out_specs = [pl.BlockSpec((tm, D_MODEL), lambda i: (i, 0)),
                     pl.BlockSpec((tm, GM_WIDTH), lambda i: (i, 0))]
        out_shape = [jax.ShapeDtypeStruct((m, D_MODEL), F32), jax.ShapeDtypeStruct((m, GM_WIDTH), F32)]
        scratch = []
    else:
        bs = bs.T
        kernel = _mix_kernel
        out_specs = pl.BlockSpec((tm, D_MODEL), lambda i: (i, 0))
        out_shape = jax.ShapeDtypeStruct((m, D_MODEL), F32)
        scratch = [pltpu.VMEM((tm, GM_WIDTH), BF16)]
    return pl.pallas_call(
        kernel,
        grid=(m // tm,),
        in_specs=[
            pl.BlockSpec((tm, D_MODEL), lambda i: (i, 0)),
            zcol(COL_GU), zcol(COL_GV), zcol(COL_AR), zcol(COL_AG),
            pl.BlockSpec((tm, RET_V), lambda i: (i, 0)),
            full(ln_g), full(ln_b), full(ws), full(bs), full(wret), full(wgm), full(wo),
        ],
        out_specs=out_specs,
        out_shape=out_shape,
        scratch_shapes=scratch,
        compiler_params=_params("parallel"),
        name="mix_step" if single_position else "mix",
    )(x, z, z, z, z, og, ln_g, ln_b, ws, bs, wret, wgm, wo)


def _ffn_kernel(h_ref, g_ref, win_ref, wdown_ref, gf_ref, y_ref, *, final_norm):
    h = h_ref[...]
    hn = (_rms(h) * g_ref[...]).astype(BF16)
    f = _dot(hn, win_ref[...])
    f_gate = f[:, :D_FF]
    f_up = f[:, D_FF:]
    act = (f_gate * jax.nn.sigmoid(f_gate) * f_up).astype(BF16)
    out = h + _dot(act, wdown_ref[...])
    if final_norm:
        out = _rms(out) * gf_ref[...]
    y_ref[...] = out


def _ffn(h, g, win, wdown, g_final, *, tm, final_norm):
    m = h.shape[0]
    full = lambda a: pl.BlockSpec(a.shape, lambda i: (0,) * a.ndim)
    g = g.reshape(1, D_MODEL)
    g_final = g_final.reshape(1, D_MODEL)
    return pl.pallas_call(
        functools.partial(_ffn_kernel, final_norm=final_norm),
        grid=(m // tm,),
        in_specs=[pl.BlockSpec((tm, D_MODEL), lambda i: (i, 0)), full(g), full(win), full(wdown),
                  full(g_final)],
        out_specs=pl.BlockSpec((tm, D_MODEL), lambda i: (i, 0)),
        out_shape=jax.ShapeDtypeStruct((m, D_MODEL), F32),
        compiler_params=_params("parallel"),
        name="ffn",
    )(h, g, win, wdown, g_final)


def _rope_tables(pos):
    inv = ROPE_BASE ** (-jnp.arange(ROPE_HALF, dtype=F32) / ROPE_HALF)
    ang = pos[:, None] * inv[None, :]
    return jnp.cos(ang), jnp.sin(ang)


def kernel(x_prompt, x_sample, state_ret, norm_mix_g, w_in, w_ret_o, gm_ln_g, gm_ln_b, gm_ws, gm_bs,
           w_gm_o, w_o, norm_ffn_g, w_ffn_in, w_ffn_down, norm_final_g):
    batch, seq, _ = x_prompt.shape
    dec_batch, dec_seq, _ = x_sample.shape
    depth = w_in.shape[0]
    assert dec_seq == 1 and seq % PR_BLOCK == 0

    cos_p, sin_p = _rope_tables(jnp.arange(seq, dtype=F32))
    cos_s, sin_s = _rope_tables(PAST_LEN + jnp.arange(dec_seq, dtype=F32))

    hp = x_prompt.reshape(batch * seq, D_MODEL)
    hs = x_sample.reshape(dec_batch, D_MODEL)
    ret_p, ret_s, gmv_s = [], [], []
    for l in range(depth):
        last = l == depth - 1
        win = w_in[l].astype(BF16)
        wret = w_ret_o[l].astype(BF16)
        wgm = w_gm_o[l].astype(BF16)
        wo = w_o[l].astype(BF16)
        wfin = w_ffn_in[l].astype(BF16)
        wfdown = w_ffn_down[l].astype(BF16)

        og, sp = _proj_ret(hp, norm_mix_g[l], win, cos_p, sin_p, batch=batch, seq=seq)
        z = _in_proj(hp, norm_mix_g[l], win, col0=RET_COLS, ncols=D_IN - RET_COLS, tm=2048, tn=2048,
                     out_dtype=BF16)
        h = _mix(hp, z, og, gm_ln_g[l], gm_ln_b[l], gm_ws[l], gm_bs[l], wret, wgm, wo,
                 z_col0=0, tm=512, single_position=False)
        hp = _ffn(h, norm_ffn_g[l], wfin, wfdown, norm_final_g, tm=512, final_norm=last)
        ret_p.append(sp)

        zs = _in_proj(hs, norm_mix_g[l], win, col0=0, ncols=D_IN, tm=dec_batch, tn=1024, out_dtype=F32)
        ogs, ss = _retention_step(zs, cos_s, sin_s, state_ret[l], nb=2)
        h, vs = _mix(hs, zs, ogs.reshape(dec_batch, RET_V), gm_ln_g[l], gm_ln_b[l], gm_ws[l], gm_bs[l],
                     wret, wgm, wo, z_col0=COL_GU, tm=dec_batch, single_position=True)
        hs = _ffn(h, norm_ffn_g[l], wfin, wfdown, norm_final_g, tm=dec_batch, final_norm=last)
        ret_s.append(ss)
        gmv_s.append(vs.reshape(dec_batch, dec_seq, GM_WIDTH))

    return (hp.reshape(batch, seq, D_MODEL), hs.reshape(dec_batch, dec_seq, D_MODEL),
            jnp.stack(ret_p), jnp.stack(ret_s), jnp.stack(gmv_s))
```

```python
import functools
import math

import numpy as np
import jax
import jax.numpy as jnp
from jax import lax
from jax.experimental import pallas as pl
from jax.experimental.pallas import tpu as pltpu

D_MODEL = 1024
PAST_LEN = 16384
RET_DK = 256
RET_HEADS = D_MODEL // 256
RET_DV = 2 * RET_DK
RET_QK = RET_HEADS * RET_DK
RET_V = RET_HEADS * RET_DV
RET_CHUNK = 128
ROPE_BASE = 10000.0
ROPE_HALF = RET_DK // 2
GM_GROUPS = 4
GM_WIDTH = D_MODEL
GM_CG = GM_WIDTH // GM_GROUPS
GM_CHUNK = 128
D_FF = ((8 * D_MODEL // 3 + 255) // 256) * 256
EPS = 1e-6
D_IN = 2 * RET_QK + 2 * RET_V + 2 * GM_WIDTH + 2 * D_MODEL

COL_Q = 0
COL_K = RET_QK
COL_V = 2 * RET_QK
COL_G = 2 * RET_QK + RET_V
COL_GU = 2 * RET_QK + 2 * RET_V
COL_GV = COL_GU + GM_WIDTH
COL_AR = COL_GV + GM_WIDTH
COL_AG = COL_AR + D_MODEL

LOG_GAMMA = tuple(float(np.log1p(-np.exp2(np.float32(-5.0 - h)))) for h in range(RET_HEADS))

VMEM_LIMIT_BYTES = 56 * 1024 * 1024
MXU_COLS = 256

F32 = jnp.float32
BF16 = jnp.bfloat16


def _dot(a, b):
    return jnp.dot(a, b, preferred_element_type=F32)


def _rms(x):
    return x * lax.rsqrt(jnp.mean(x * x, axis=-1, keepdims=True) + EPS)


def _params(*semantics):
    return pltpu.CompilerParams(dimension_semantics=semantics, vmem_limit_bytes=VMEM_LIMIT_BYTES)


def _in_proj_kernel(x_ref, g_ref, w_ref, z_ref, xn_ref):
    @pl.when(pl.program_id(1) == 0)
    def _():
        xn_ref[...] = (_rms(x_ref[...]) * g_ref[...]).astype(BF16)

    z_ref[...] = _dot(xn_ref[...], w_ref[...]).astype(z_ref.dtype)


def _in_proj(x, g, w_bf16, *, col0, ncols, tm, tn, out_dtype):
    m = x.shape[0]
    tile0 = col0 // tn
    assert col0 % tn == 0 and ncols % tn == 0
    return pl.pallas_call(
        _in_proj_kernel,
        grid=(m // tm, ncols // tn),
        in_specs=[
            pl.BlockSpec((tm, D_MODEL), lambda i, j: (i, 0)),
            pl.BlockSpec((1, D_MODEL), lambda i, j: (0, 0)),
            pl.BlockSpec((D_MODEL, tn), lambda i, j: (0, tile0 + j)),
        ],
        out_specs=pl.BlockSpec((tm, tn), lambda i, j: (i, j)),
        out_shape=jax.ShapeDtypeStruct((m, ncols), out_dtype),
        scratch_shapes=[pltpu.VMEM((tm, D_MODEL), BF16)],
        compiler_params=_params("parallel", "arbitrary"),
        name="in_proj",
    )(x, g.reshape(1, D_MODEL), w_bf16)


def _rotate(x, cos, sin):
    x1 = x[:, :ROPE_HALF]
    x2 = x[:, ROPE_HALF:]
    return jnp.concatenate([x1 * cos - x2 * sin, x1 * sin + x2 * cos], axis=-1)


RET_COLS = COL_GU
PR_BLOCK = 512
PR_CHUNKS = PR_BLOCK // RET_CHUNK
PR_TILE = RET_COLS // PR_CHUNKS
assert PR_TILE % RET_DV == 0


def _decay_tables():
    L = RET_CHUNK
    lg = np.array(LOG_GAMMA, np.float32)
    n = np.arange(L, dtype=np.float32)
    diff = n[:, None] - n[None, :]
    intra = np.where(diff[None] >= 0, np.exp(np.maximum(diff, 0.0)[None] * lg[:, None, None]), 0.0)
    q_decay = np.exp((n + 1.0)[None, :, None] * lg[:, None, None])
    k_decay = np.exp((L - 1.0 - n)[None, :, None] * lg[:, None, None])
    return (jnp.asarray(intra, F32), jnp.asarray(np.broadcast_to(q_decay, (RET_HEADS, L, RET_DV)), F32),
            jnp.asarray(np.broadcast_to(k_decay, (RET_HEADS, L, RET_DK)), F32))


def _retention_chunk(z_ref, rows, cos, sin, intra_ref, qd_ref, kd_ref, st_ref, og_ref, before_head):
    def zcols(start, width):
        tile, off = divmod(start, PR_TILE)
        return z_ref[tile, rows, off:off + width]

    for h in range(RET_HEADS):
        before_head(h)
        q = _rotate(zcols(COL_Q + h * RET_DK, RET_DK).astype(F32), cos, sin)
        k = _rotate(zcols(COL_K + h * RET_DK, RET_DK).astype(F32), cos, sin) * (RET_DK ** -0.5)
        v = zcols(COL_V + h * RET_DV, RET_DV)
        qb = q.astype(BF16)
        scores = lax.dot_general(qb, k.astype(BF16), (((1,), (1,)), ((), ())),
                                 preferred_element_type=F32) * intra_ref[h]
        state = st_ref[0, h]
        o = _dot(scores.astype(BF16), v) + _dot(qb, state.astype(BF16)) * qd_ref[h]
        st_ref[0, h] = math.exp(RET_CHUNK * LOG_GAMMA[h]) * state + lax.dot_general(
            (k * kd_ref[h]).astype(BF16), v, (((0,), (0,)), ((), ())), preferred_element_type=F32)
        gate = zcols(COL_G + h * RET_DV, RET_DV).astype(F32)
        og_ref[:, h * RET_DV:(h + 1) * RET_DV] = (gate * jax.nn.sigmoid(gate) * _rms(o)).astype(og_ref.dtype)


def _row_to_col(row):
    n = row.shape[1]
    eye = lax.broadcasted_iota(jnp.int32, (n, n), 0) == lax.broadcasted_iota(jnp.int32, (n, n), 1)
    return jnp.sum(jnp.where(eye, row, 0.0), axis=1, keepdims=True)


def _retention_step_head(h, z_ref, cos, sin, st_ref, og_ref, new_st_ref):
    q = _rotate(z_ref[0, :, COL_Q + h * RET_DK:COL_Q + (h + 1) * RET_DK], cos, sin)
    k = _rotate(z_ref[0, :, COL_K + h * RET_DK:COL_K + (h + 1) * RET_DK], cos, sin) * (RET_DK ** -0.5)
    v = z_ref[0, :, COL_V + h * RET_DV:COL_V + (h + 1) * RET_DV]
    new_state = math.exp(LOG_GAMMA[h]) * st_ref[0, h] + _row_to_col(k) * v
    new_st_ref[0, h] = new_state
    o = jnp.sum(_row_to_col(q) * new_state, axis=0, keepdims=True)
    gate = z_ref[0, :, COL_G + h * RET_DV:COL_G + (h + 1) * RET_DV]
    og_ref[0, :, h * RET_DV:(h + 1) * RET_DV] = gate * jax.nn.sigmoid(gate) * _rms(o)


def _proj_ret_kernel(x_ref, g_ref, w_ref, cos_ref, sin_ref, intra_ref, qd_ref, kd_ref,
                     zs_ref, cos_s_ref, sin_s_ref, sst_ref,
                     og_ref, st_ref, ogs_ref, new_sst_ref,
                     xn_ref, za_ref, zb_ref, *, blocks_per_seq):
    t = pl.program_id(0)
    j = pl.program_id(1)
    ret_block = jnp.maximum(t - 1, 0)

    @pl.when(j == 0)
    def _():
        xn_ref[...] = (_rms(x_ref[...]) * g_ref[...]).astype(BF16)

    @pl.when(jnp.logical_and(t == 0, j == 0))
    def _():
        zb_ref[...] = jnp.zeros_like(zb_ref)

    @pl.when(jnp.logical_and(ret_block % blocks_per_seq == 0, j == 0))
    def _():
        st_ref[...] = jnp.zeros_like(st_ref)

    rows = pl.ds(pl.multiple_of(j * RET_CHUNK, RET_CHUNK), RET_CHUNK)

    def step(z_write, z_read):
        n_pieces = PR_TILE // MXU_COLS

        def before_head(h):
            for p in range(h * n_pieces // RET_HEADS, (h + 1) * n_pieces // RET_HEADS):
                cols = slice(p * MXU_COLS, (p + 1) * MXU_COLS)
                z_write[j, :, cols] = _dot(xn_ref[...], w_ref[:, cols]).astype(BF16)
            _retention_step_head(h, zs_ref, cos_s_ref[...], sin_s_ref[...], sst_ref, ogs_ref, new_sst_ref)

        _retention_chunk(z_read, rows, cos_ref[...], sin_ref[...], intra_ref, qd_ref, kd_ref, st_ref, og_ref,
                         before_head)

    @pl.when(t % 2 == 0)
    def _():
        step(za_ref, zb_ref)

    @pl.when(t % 2 == 1)
    def _():
        step(zb_ref, za_ref)


def _proj_ret(x, g, w_bf16, cos, sin, z_sample, cos_s, sin_s, state_sample, *, batch, seq):
    n_blocks = batch * seq // PR_BLOCK
    blocks_per_seq = seq // PR_BLOCK
    dec_batch = z_sample.shape[0]
    assert dec_batch <= (n_blocks + 1) * PR_CHUNKS
    intra, q_decay, k_decay = _decay_tables()
    full = lambda a: pl.BlockSpec(a.shape, lambda t, j: (0,) * a.ndim)
    proj_block = lambda t: jnp.minimum(t, n_blocks - 1)
    ret_block = lambda t: jnp.maximum(t - 1, 0)
    chunk_of_seq = lambda t, j: (ret_block(t) % blocks_per_seq) * PR_CHUNKS + j
    og_block = lambda t: jnp.where(t == 0, n_blocks, t - 1)
    sample = lambda t, j: jnp.minimum(t * PR_CHUNKS + j, dec_batch - 1)
    state_spec = pl.BlockSpec((1, RET_HEADS, RET_DK, RET_DV), lambda t, j: (sample(t, j), 0, 0, 0))
    return pl.pallas_call(
        functools.partial(_proj_ret_kernel, blocks_per_seq=blocks_per_seq),
        grid=(n_blocks + 1, PR_CHUNKS),
        in_specs=[
            pl.BlockSpec((PR_BLOCK, D_MODEL), lambda t, j: (proj_block(t), 0)),
            pl.BlockSpec((1, D_MODEL), lambda t, j: (0, 0)),
            pl.BlockSpec((D_MODEL, PR_TILE), lambda t, j: (0, j)),
            pl.BlockSpec((RET_CHUNK, ROPE_HALF), lambda t, j: (chunk_of_seq(t, j), 0)),
            pl.BlockSpec((RET_CHUNK, ROPE_HALF), lambda t, j: (chunk_of_seq(t, j), 0)),
            full(intra), full(q_decay), full(k_decay),
            pl.BlockSpec((1, 1, RET_COLS), lambda t, j: (sample(t, j), 0, 0)),
            full(cos_s), full(sin_s),
            state_spec,
        ],
        out_specs=[
            pl.BlockSpec((RET_CHUNK, RET_V), lambda t, j: (og_block(t) * PR_CHUNKS + j, 0)),
            pl.BlockSpec((1, RET_HEADS, RET_DK, RET_DV), lambda t, j: (ret_block(t) // blocks_per_seq, 0, 0, 0)),
            pl.BlockSpec((1, 1, RET_V), lambda t, j: (sample(t, j), 0, 0)),
            state_spec,
        ],
        out_shape=[
            jax.ShapeDtypeStruct(((n_blocks + 1) * PR_BLOCK, RET_V), BF16),
            jax.ShapeDtypeStruct((batch, RET_HEADS, RET_DK, RET_DV), F32),
            jax.ShapeDtypeStruct((dec_batch, 1, RET_V), F32),
            jax.ShapeDtypeStruct(state_sample.shape, F32),
        ],
        scratch_shapes=[
            pltpu.VMEM((PR_BLOCK, D_MODEL), BF16),
            pltpu.VMEM((PR_CHUNKS, PR_BLOCK, PR_TILE), BF16),
            pltpu.VMEM((PR_CHUNKS, PR_BLOCK, PR_TILE), BF16),
        ],
        compiler_params=_params("arbitrary", "arbitrary"),
        name="proj_ret",
    )(x, g.reshape(1, D_MODEL), w_bf16, cos, sin, intra, q_decay, k_decay,
      z_sample.reshape(dec_batch, 1, D_IN), cos_s, sin_s, state_sample)


def _gelu_tanh(x):
    c = math.sqrt(2.0 / math.pi)
    return x * (0.5 + 0.5 * jnp.tanh(x * (c + (c * 0.044715) * (x * x))))


def _layer_norm(x, g, b):
    xc = x - jnp.mean(x, axis=-1, keepdims=True)
    return xc * lax.rsqrt(jnp.mean(xc * xc, axis=-1, keepdims=True) + EPS) * g + b


def _merge(x_ref, ar_ref, ag_ref, og_ref, gm, wret_ref, wgm_ref, wo_ref, h_ref):
    branch_ret = _dot(og_ref[...].astype(BF16), wret_ref[...])
    branch_gm = _dot(gm, wgm_ref[...])
    m = (jax.nn.sigmoid(ar_ref[...].astype(F32)) * branch_ret
         + jax.nn.sigmoid(ag_ref[...].astype(F32)) * branch_gm)
    h_ref[...] = x_ref[...] + _dot(m.astype(BF16), wo_ref[...])


def _mix_kernel(x_ref, gu_ref, gv_ref, ar_ref, ag_ref, og_ref, lng_ref, lnb_ref, ws_ref, bs_ref,
                wret_ref, wgm_ref, wo_ref, h_ref, gm_ref, br_ref, m_ref):
    L = GM_CHUNK
    n_chunks = x_ref.shape[0] // L
    n_pieces = D_MODEL // MXU_COLS
    piece = lambda p: slice(p * MXU_COLS, (p + 1) * MXU_COLS)
    causal = lax.broadcasted_iota(jnp.int32, (L, L), 0) >= lax.broadcasted_iota(jnp.int32, (L, L), 1)
    w_causal = [jnp.where(causal, ws_ref[g], 0.0).astype(BF16) for g in range(GM_GROUPS)]
    for c in range(n_chunks):
        for p in range(c * n_pieces // n_chunks, (c + 1) * n_pieces // n_chunks):
            br_ref[:, piece(p)] = _dot(og_ref[...], wret_ref[:, piece(p)])
        rows = slice(c * L, (c + 1) * L)
        u = _gelu_tanh(gu_ref[rows, :].astype(F32))
        v = _layer_norm(_gelu_tanh(gv_ref[rows, :].astype(F32)), lng_ref[...], lnb_ref[...]).astype(BF16)
        for g in range(GM_GROUPS):
            cols = slice(g * GM_CG, (g + 1) * GM_CG)
            mixed = _dot(w_causal[g], v[:, cols]) + bs_ref[:, g:g + 1]
            gm_ref[rows, cols] = (u[:, cols] * mixed).astype(BF16)
    for p in range(n_pieces):
        branch_gm = _dot(gm_ref[...], wgm_ref[:, piece(p)])
        m_ref[:, piece(p)] = (jax.nn.sigmoid(ar_ref[:, piece(p)].astype(F32)) * br_ref[:, piece(p)]
                              + jax.nn.sigmoid(ag_ref[:, piece(p)].astype(F32)) * branch_gm).astype(BF16)
    for p in range(n_pieces):
        h_ref[:, piece(p)] = x_ref[:, piece(p)] + _dot(m_ref[...], wo_ref[:, piece(p)])


def _mix_step_kernel(x_ref, gu_ref, gv_ref, ar_ref, ag_ref, og_ref, lng_ref, lnb_ref, ws_ref, bs_ref,
                     wret_ref, wgm_ref, wo_ref, h_ref, v_ref):
    u = _gelu_tanh(gu_ref[...])
    v = _layer_norm(_gelu_tanh(gv_ref[...]), lng_ref[...], lnb_ref[...])
    v_ref[...] = v
    gm = (u * (ws_ref[...] * v + bs_ref[...])).astype(BF16)
    _merge(x_ref, ar_ref, ag_ref, og_ref, gm, wret_ref, wgm_ref, wo_ref, h_ref)


def _mix(x, z, og, ln_g, ln_b, ws, bs, wret, wgm, wo, *, z_col0, tm, single_position):
    m = x.shape[0]
    zcol = lambda off: pl.BlockSpec((tm, D_MODEL), lambda i: (i, (z_col0 + off - COL_GU) // D_MODEL))
    full = lambda a: pl.BlockSpec(a.shape, lambda i: (0,) * a.ndim)
    ln_g = ln_g.reshape(1, GM_WIDTH)
    ln_b = ln_b.reshape(1, GM_WIDTH)
    if single_position:
        ws = jnp.repeat(ws[:, 0, 0], GM_CG).reshape(1, GM_WIDTH)
        bs = jnp.repeat(bs[:, 0], GM_CG).reshape(1, GM_WIDTH)
        kernel = _mix_step_kernel
        out_specs = [pl.BlockSpec((tm, D_MODEL), lambda i: (i, 0)),
                     pl.BlockSpec((tm, GM_WIDTH), lambda i: (i, 0))]
        out_shape = [jax.ShapeDtypeStruct((m, D_MODEL), F32), jax.ShapeDtypeStruct((m, GM_WIDTH), F32)]
        scratch = []
    else:
        bs = bs.T
        kernel = _mix_kernel
        out_specs = pl.BlockSpec((tm, D_MODEL), lambda i: (i, 0))
        out_shape = jax.ShapeDtypeStruct((m, D_MODEL), F32)
        scratch = [pltpu.VMEM((tm, GM_WIDTH), BF16), pltpu.VMEM((tm, D_MODEL), F32),
                   pltpu.VMEM((tm, D_MODEL), BF16)]
    return pl.pallas_call(
        kernel,
        grid=(m // tm,),
        in_specs=[
            pl.BlockSpec((tm, D_MODEL), lambda i: (i, 0)),
            zcol(COL_GU), zcol(COL_GV), zcol(COL_AR), zcol(COL_AG),
            pl.BlockSpec((tm, RET_V), lambda i: (i, 0)),
            full(ln_g), full(ln_b), full(ws), full(bs), full(wret), full(wgm), full(wo),
        ],
        out_specs=out_specs,
        out_shape=out_shape,
        scratch_shapes=scratch,
        compiler_params=_params("parallel"),
        name="mix_step" if single_position else "mix",
    )(x, z, z, z, z, og, ln_g, ln_b, ws, bs, wret, wgm, wo)


def _ffn_kernel(h_ref, g_ref, win_ref, wdown_ref, gf_ref, y_ref, *, final_norm):
    h = h_ref[...]
    hn = (_rms(h) * g_ref[...]).astype(BF16)
    f = _dot(hn, win_ref[...])
    f_gate = f[:, :D_FF]
    f_up = f[:, D_FF:]
    act = (f_gate * jax.nn.sigmoid(f_gate) * f_up).astype(BF16)
    out = h + _dot(act, wdown_ref[...])
    if final_norm:
        out = _rms(out) * gf_ref[...]
    y_ref[...] = out


def _ffn(h, g, win, wdown, g_final, *, tm, final_norm):
    m = h.shape[0]
    full = lambda a: pl.BlockSpec(a.shape, lambda i: (0,) * a.ndim)
    g = g.reshape(1, D_MODEL)
    g_final = g_final.reshape(1, D_MODEL)
    return pl.pallas_call(
        functools.partial(_ffn_kernel, final_norm=final_norm),
        grid=(m // tm,),
        in_specs=[pl.BlockSpec((tm, D_MODEL), lambda i: (i, 0)), full(g), full(win), full(wdown),
                  full(g_final)],
        out_specs=pl.BlockSpec((tm, D_MODEL), lambda i: (i, 0)),
        out_shape=jax.ShapeDtypeStruct((m, D_MODEL), F32),
        compiler_params=_params("parallel"),
        name="ffn",
    )(h, g, win, wdown, g_final)


def _rope_tables(pos):
    inv = ROPE_BASE ** (-jnp.arange(ROPE_HALF, dtype=F32) / ROPE_HALF)
    ang = pos[:, None] * inv[None, :]
    return jnp.cos(ang), jnp.sin(ang)


def kernel(x_prompt, x_sample, state_ret, norm_mix_g, w_in, w_ret_o, gm_ln_g, gm_ln_b, gm_ws, gm_bs,
           w_gm_o, w_o, norm_ffn_g, w_ffn_in, w_ffn_down, norm_final_g):
    batch, seq, _ = x_prompt.shape
    dec_batch, dec_seq, _ = x_sample.shape
    depth = w_in.shape[0]
    assert dec_seq == 1 and seq % PR_BLOCK == 0

    cos_p, sin_p = _rope_tables(jnp.arange(seq, dtype=F32))
    cos_s, sin_s = _rope_tables(PAST_LEN + jnp.arange(dec_seq, dtype=F32))

    hp = x_prompt.reshape(batch * seq, D_MODEL)
    hs = x_sample.reshape(dec_batch, D_MODEL)
    ret_p, ret_s, gmv_s = [], [], []
    for l in range(depth):
        last = l == depth - 1
        win = w_in[l].astype(BF16)
        wret = w_ret_o[l].astype(BF16)
        wgm = w_gm_o[l].astype(BF16)
        wo = w_o[l].astype(BF16)
        wfin = w_ffn_in[l].astype(BF16)
        wfdown = w_ffn_down[l].astype(BF16)

        zs = _in_proj(hs, norm_mix_g[l], win, col0=0, ncols=D_IN, tm=dec_batch, tn=1024, out_dtype=F32)
        og, sp, ogs, ss = _proj_ret(hp, norm_mix_g[l], win, cos_p, sin_p, zs, cos_s, sin_s, state_ret[l],
                                    batch=batch, seq=seq)

        z = _in_proj(hp, norm_mix_g[l], win, col0=RET_COLS, ncols=D_IN - RET_COLS, tm=2048, tn=2048,
                     out_dtype=BF16)
        h = _mix(hp, z, og, gm_ln_g[l], gm_ln_b[l], gm_ws[l], gm_bs[l], wret, wgm, wo,
                 z_col0=0, tm=512, single_position=False)
        hp = _ffn(h, norm_ffn_g[l], wfin, wfdown, norm_final_g, tm=512, final_norm=last)
        ret_p.append(sp)

        h, vs = _mix(hs, zs, ogs.reshape(dec_batch, RET_V), gm_ln_g[l], gm_ln_b[l], gm_ws[l], gm_bs[l],
                     wret, wgm, wo, z_col0=COL_GU, tm=dec_batch, single_position=True)
        hs = _ffn(h, norm_ffn_g[l], wfin, wfdown, norm_final_g, tm=dec_batch, final_norm=last)
        ret_s.append(ss)
        gmv_s.append(vs.reshape(dec_batch, dec_seq, GM_WIDTH))

    return (hp.reshape(batch, seq, D_MODEL), hs.reshape(dec_batch, dec_seq, D_MODEL),
            jnp.stack(ret_p), jnp.stack(ret_s), jnp.stack(gmv_s))
```

```python
import functools
import math

import numpy as np
import jax
import jax.numpy as jnp
from jax import lax
from jax.experimental import pallas as pl
from jax.experimental.pallas import tpu as pltpu

D_MODEL = 1024
PAST_LEN = 16384
RET_DK = 256
RET_HEADS = D_MODEL // 256
RET_DV = 2 * RET_DK
RET_QK = RET_HEADS * RET_DK
RET_V = RET_HEADS * RET_DV
RET_CHUNK = 128
ROPE_BASE = 10000.0
ROPE_HALF = RET_DK // 2
GM_GROUPS = 4
GM_WIDTH = D_MODEL
GM_CG = GM_WIDTH // GM_GROUPS
GM_CHUNK = 128
D_FF = ((8 * D_MODEL // 3 + 255) // 256) * 256
EPS = 1e-6
D_IN = 2 * RET_QK + 2 * RET_V + 2 * GM_WIDTH + 2 * D_MODEL

COL_Q = 0
COL_K = RET_QK
COL_V = 2 * RET_QK
COL_G = 2 * RET_QK + RET_V
COL_GU = 2 * RET_QK + 2 * RET_V
COL_GV = COL_GU + GM_WIDTH
COL_AR = COL_GV + GM_WIDTH
COL_AG = COL_AR + D_MODEL

LOG_GAMMA = tuple(float(np.log1p(-np.exp2(np.float32(-5.0 - h)))) for h in range(RET_HEADS))

VMEM_LIMIT_BYTES = 56 * 1024 * 1024
MXU_COLS = 256

F32 = jnp.float32
BF16 = jnp.bfloat16


def _dot(a, b):
    return jnp.dot(a, b, preferred_element_type=F32)


def _rms(x):
    return x * lax.rsqrt(jnp.mean(x * x, axis=-1, keepdims=True) + EPS)


def _params(*semantics):
    return pltpu.CompilerParams(dimension_semantics=semantics, vmem_limit_bytes=VMEM_LIMIT_BYTES)


def _in_proj_kernel(x_ref, g_ref, w_ref, z_ref, xn_ref):
    @pl.when(pl.program_id(1) == 0)
    def _():
        xn_ref[...] = (_rms(x_ref[...]) * g_ref[...]).astype(BF16)

    z_ref[...] = _dot(xn_ref[...], w_ref[...]).astype(z_ref.dtype)


def _in_proj(x, g, w_bf16, *, col0, ncols, tm, tn, out_dtype):
    m = x.shape[0]
    tile0 = col0 // tn
    assert col0 % tn == 0 and ncols % tn == 0
    return pl.pallas_call(
        _in_proj_kernel,
        grid=(m // tm, ncols // tn),
        in_specs=[
            pl.BlockSpec((tm, D_MODEL), lambda i, j: (i, 0)),
            pl.BlockSpec((1, D_MODEL), lambda i, j: (0, 0)),
            pl.BlockSpec((D_MODEL, tn), lambda i, j: (0, tile0 + j)),
        ],
        out_specs=pl.BlockSpec((tm, tn), lambda i, j: (i, j)),
        out_shape=jax.ShapeDtypeStruct((m, ncols), out_dtype),
        scratch_shapes=[pltpu.VMEM((tm, D_MODEL), BF16)],
        compiler_params=_params("parallel", "arbitrary"),
        name="in_proj",
    )(x, g.reshape(1, D_MODEL), w_bf16)


def _rotate(x, cos, sin):
    x1 = x[:, :ROPE_HALF]
    x2 = x[:, ROPE_HALF:]
    return jnp.concatenate([x1 * cos - x2 * sin, x1 * sin + x2 * cos], axis=-1)


RET_COLS = COL_GU
PR_BLOCK = 512
PR_CHUNKS = PR_BLOCK // RET_CHUNK
PR_TILE = RET_COLS // PR_CHUNKS
assert PR_TILE % RET_DV == 0


def _decay_tables():
    L = RET_CHUNK
    lg = np.array(LOG_GAMMA, np.float32)
    n = np.arange(L, dtype=np.float32)
    diff = n[:, None] - n[None, :]
    intra = np.where(diff[None] >= 0, np.exp(np.maximum(diff, 0.0)[None] * lg[:, None, None]), 0.0)
    q_decay = np.exp((n + 1.0)[None, :, None] * lg[:, None, None])
    k_decay = np.exp((L - 1.0 - n)[None, :, None] * lg[:, None, None])
    return (jnp.asarray(intra, F32), jnp.asarray(np.broadcast_to(q_decay, (RET_HEADS, L, RET_DV)), F32),
            jnp.asarray(np.broadcast_to(k_decay, (RET_HEADS, L, RET_DK)), F32))


def _retention_chunk(z_ref, rows, cos, sin, intra_ref, qd_ref, kd_ref, st_ref, og_ref, before_head):
    def zcols(start, width):
        tile, off = divmod(start, PR_TILE)
        return z_ref[tile, rows, off:off + width]

    for h in range(RET_HEADS):
        before_head(h)
        q = _rotate(zcols(COL_Q + h * RET_DK, RET_DK).astype(F32), cos, sin)
        k = _rotate(zcols(COL_K + h * RET_DK, RET_DK).astype(F32), cos, sin) * (RET_DK ** -0.5)
        v = zcols(COL_V + h * RET_DV, RET_DV)
        qb = q.astype(BF16)
        scores = lax.dot_general(qb, k.astype(BF16), (((1,), (1,)), ((), ())),
                                 preferred_element_type=F32) * intra_ref[h]
        state = st_ref[0, h]
        o = _dot(scores.astype(BF16), v) + _dot(qb, state.astype(BF16)) * qd_ref[h]
        st_ref[0, h] = math.exp(RET_CHUNK * LOG_GAMMA[h]) * state + lax.dot_general(
            (k * kd_ref[h]).astype(BF16), v, (((0,), (0,)), ((), ())), preferred_element_type=F32)
        gate = zcols(COL_G + h * RET_DV, RET_DV).astype(F32)
        og_ref[:, h * RET_DV:(h + 1) * RET_DV] = (gate * jax.nn.sigmoid(gate) * _rms(o)).astype(og_ref.dtype)


def _row_to_col(row):
    n = row.shape[1]
    eye = lax.broadcasted_iota(jnp.int32, (n, n), 0) == lax.broadcasted_iota(jnp.int32, (n, n), 1)
    return jnp.sum(jnp.where(eye, row, 0.0), axis=1, keepdims=True)


def _retention_step_head(h, z_ref, cos, sin, st_ref, og_ref, new_st_ref):
    q = _rotate(z_ref[0, :, COL_Q + h * RET_DK:COL_Q + (h + 1) * RET_DK], cos, sin)
    k = _rotate(z_ref[0, :, COL_K + h * RET_DK:COL_K + (h + 1) * RET_DK], cos, sin) * (RET_DK ** -0.5)
    v = z_ref[0, :, COL_V + h * RET_DV:COL_V + (h + 1) * RET_DV]
    new_state = math.exp(LOG_GAMMA[h]) * st_ref[0, h] + _row_to_col(k) * v
    new_st_ref[0, h] = new_state
    o = jnp.sum(_row_to_col(q) * new_state, axis=0, keepdims=True)
    gate = z_ref[0, :, COL_G + h * RET_DV:COL_G + (h + 1) * RET_DV]
    og_ref[0, :, h * RET_DV:(h + 1) * RET_DV] = gate * jax.nn.sigmoid(gate) * _rms(o)


def _proj_ret_kernel(x_ref, g_ref, w_ref, cos_ref, sin_ref, intra_ref, qd_ref, kd_ref,
                     zs_ref, cos_s_ref, sin_s_ref, sst_ref,
                     og_ref, st_ref, ogs_ref, new_sst_ref,
                     xn_ref, za_ref, zb_ref, *, blocks_per_seq):
    t = pl.program_id(0)
    j = pl.program_id(1)
    ret_block = jnp.maximum(t - 1, 0)

    @pl.when(j == 0)
    def _():
        xn_ref[...] = (_rms(x_ref[...]) * g_ref[...]).astype(BF16)

    @pl.when(jnp.logical_and(t == 0, j == 0))
    def _():
        zb_ref[...] = jnp.zeros_like(zb_ref)

    @pl.when(jnp.logical_and(ret_block % blocks_per_seq == 0, j == 0))
    def _():
        st_ref[...] = jnp.zeros_like(st_ref)

    rows = pl.ds(pl.multiple_of(j * RET_CHUNK, RET_CHUNK), RET_CHUNK)

    def step(z_write, z_read):
        n_pieces = PR_TILE // MXU_COLS

        def before_head(h):
            for p in range(h * n_pieces // RET_HEADS, (h + 1) * n_pieces // RET_HEADS):
                cols = slice(p * MXU_COLS, (p + 1) * MXU_COLS)
                w_cols = pl.ds(pl.multiple_of(j * PR_TILE + p * MXU_COLS, MXU_COLS), MXU_COLS)
                z_write[j, :, cols] = _dot(xn_ref[...], w_ref[:, w_cols]).astype(BF16)
            _retention_step_head(h, zs_ref, cos_s_ref[...], sin_s_ref[...], sst_ref, ogs_ref, new_sst_ref)

        _retention_chunk(z_read, rows, cos_ref[...], sin_ref[...], intra_ref, qd_ref, kd_ref, st_ref, og_ref,
                         before_head)

    @pl.when(t % 2 == 0)
    def _():
        step(za_ref, zb_ref)

    @pl.when(t % 2 == 1)
    def _():
        step(zb_ref, za_ref)


def _proj_ret(x, g, w_bf16, cos, sin, z_sample, cos_s, sin_s, state_sample, *, batch, seq):
    n_blocks = batch * seq // PR_BLOCK
    blocks_per_seq = seq // PR_BLOCK
    dec_batch = z_sample.shape[0]
    assert dec_batch <= (n_blocks + 1) * PR_CHUNKS
    intra, q_decay, k_decay = _decay_tables()
    full = lambda a: pl.BlockSpec(a.shape, lambda t, j: (0,) * a.ndim, pipeline_mode=pl.Buffered(1))
    proj_block = lambda t: jnp.minimum(t, n_blocks - 1)
    ret_block = lambda t: jnp.maximum(t - 1, 0)
    chunk_of_seq = lambda t, j: (ret_block(t) % blocks_per_seq) * PR_CHUNKS + j
    og_block = lambda t: jnp.where(t == 0, n_blocks, t - 1)
    sample = lambda t, j: jnp.minimum(t * PR_CHUNKS + j, dec_batch - 1)
    state_spec = pl.BlockSpec((1, RET_HEADS, RET_DK, RET_DV), lambda t, j: (sample(t, j), 0, 0, 0))
    return pl.pallas_call(
        functools.partial(_proj_ret_kernel, blocks_per_seq=blocks_per_seq),
        grid=(n_blocks + 1, PR_CHUNKS),
        in_specs=[
            pl.BlockSpec((PR_BLOCK, D_MODEL), lambda t, j: (proj_block(t), 0)),
            pl.BlockSpec((1, D_MODEL), lambda t, j: (0, 0)),
            pl.BlockSpec((D_MODEL, RET_COLS), lambda t, j: (0, 0), pipeline_mode=pl.Buffered(1)),
            pl.BlockSpec((RET_CHUNK, ROPE_HALF), lambda t, j: (chunk_of_seq(t, j), 0)),
            pl.BlockSpec((RET_CHUNK, ROPE_HALF), lambda t, j: (chunk_of_seq(t, j), 0)),
            full(intra), full(q_decay), full(k_decay),
            pl.BlockSpec((1, 1, RET_COLS), lambda t, j: (sample(t, j), 0, 0)),
            full(cos_s), full(sin_s),
            state_spec,
        ],
        out_specs=[
            pl.BlockSpec((RET_CHUNK, RET_V), lambda t, j: (og_block(t) * PR_CHUNKS + j, 0)),
            pl.BlockSpec((1, RET_HEADS, RET_DK, RET_DV), lambda t, j: (ret_block(t) // blocks_per_seq, 0, 0, 0)),
            pl.BlockSpec((1, 1, RET_V), lambda t, j: (sample(t, j), 0, 0)),
            state_spec,
        ],
        out_shape=[
            jax.ShapeDtypeStruct(((n_blocks + 1) * PR_BLOCK, RET_V), BF16),
            jax.ShapeDtypeStruct((batch, RET_HEADS, RET_DK, RET_DV), F32),
            jax.ShapeDtypeStruct((dec_batch, 1, RET_V), F32),
            jax.ShapeDtypeStruct(state_sample.shape, F32),
        ],
        scratch_shapes=[
            pltpu.VMEM((PR_BLOCK, D_MODEL), BF16),
            pltpu.VMEM((PR_CHUNKS, PR_BLOCK, PR_TILE), BF16),
            pltpu.VMEM((PR_CHUNKS, PR_BLOCK, PR_TILE), BF16),
        ],
        compiler_params=_params("arbitrary", "arbitrary"),
        name="proj_ret",
    )(x, g.reshape(1, D_MODEL), w_bf16, cos, sin, intra, q_decay, k_decay,
      z_sample.reshape(dec_batch, 1, D_IN), cos_s, sin_s, state_sample)


def _gelu_tanh(x):
    c = math.sqrt(2.0 / math.pi)
    return x * (0.5 + 0.5 * jnp.tanh(x * (c + (c * 0.044715) * (x * x))))


def _layer_norm(x, g, b):
    xc = x - jnp.mean(x, axis=-1, keepdims=True)
    return xc * lax.rsqrt(jnp.mean(xc * xc, axis=-1, keepdims=True) + EPS) * g + b


def _merge(x_ref, ar_ref, ag_ref, og_ref, gm, wret_ref, wgm_ref, wo_ref, h_ref):
    branch_ret = _dot(og_ref[...].astype(BF16), wret_ref[...])
    branch_gm = _dot(gm, wgm_ref[...])
    m = (jax.nn.sigmoid(ar_ref[...].astype(F32)) * branch_ret
         + jax.nn.sigmoid(ag_ref[...].astype(F32)) * branch_gm)
    h_ref[...] = x_ref[...] + _dot(m.astype(BF16), wo_ref[...])


def _mix_kernel(x_ref, gu_ref, gv_ref, ar_ref, ag_ref, og_ref, lng_ref, lnb_ref, ws_ref, bs_ref,
                wret_ref, wgm_ref, wo_ref, h_ref, gm_ref, br_ref, m_ref):
    L = GM_CHUNK
    n_chunks = x_ref.shape[0] // L
    n_pieces = D_MODEL // MXU_COLS
    piece = lambda p: slice(p * MXU_COLS, (p + 1) * MXU_COLS)
    causal = lax.broadcasted_iota(jnp.int32, (L, L), 0) >= lax.broadcasted_iota(jnp.int32, (L, L), 1)
    w_causal = [jnp.where(causal, ws_ref[g], 0.0).astype(BF16) for g in range(GM_GROUPS)]
    for c in range(n_chunks):
        for p in range(c * n_pieces // n_chunks, (c + 1) * n_pieces // n_chunks):
            br_ref[:, piece(p)] = _dot(og_ref[...], wret_ref[:, piece(p)])
        rows = slice(c * L, (c + 1) * L)
        u = _gelu_tanh(gu_ref[rows, :].astype(F32))
        v = _layer_norm(_gelu_tanh(gv_ref[rows, :].astype(F32)), lng_ref[...], lnb_ref[...]).astype(BF16)
        for g in range(GM_GROUPS):
            cols = slice(g * GM_CG, (g + 1) * GM_CG)
            mixed = _dot(w_causal[g], v[:, cols]) + bs_ref[:, g:g + 1]
            gm_ref[rows, cols] = (u[:, cols] * mixed).astype(BF16)
    for p in range(n_pieces):
        branch_gm = _dot(gm_ref[...], wgm_ref[:, piece(p)])
        m_ref[:, piece(p)] = (jax.nn.sigmoid(ar_ref[:, piece(p)].astype(F32)) * br_ref[:, piece(p)]
                              + jax.nn.sigmoid(ag_ref[:, piece(p)].astype(F32)) * branch_gm).astype(BF16)
    for p in range(n_pieces):
        h_ref[:, piece(p)] = x_ref[:, piece(p)] + _dot(m_ref[...], wo_ref[:, piece(p)])


def _mix_step_kernel(x_ref, gu_ref, gv_ref, ar_ref, ag_ref, og_ref, lng_ref, lnb_ref, ws_ref, bs_ref,
                     wret_ref, wgm_ref, wo_ref, h_ref, v_ref):
    u = _gelu_tanh(gu_ref[...])
    v = _layer_norm(_gelu_tanh(gv_ref[...]), lng_ref[...], lnb_ref[...])
    v_ref[...] = v
    gm = (u * (ws_ref[...] * v + bs_ref[...])).astype(BF16)
    _merge(x_ref, ar_ref, ag_ref, og_ref, gm, wret_ref, wgm_ref, wo_ref, h_ref)


def _mix(x, z, og, ln_g, ln_b, ws, bs, wret, wgm, wo, *, z_col0, tm, single_position):
    m = x.shape[0]
    zcol = lambda off: pl.BlockSpec((tm, D_MODEL), lambda i: (i, (z_col0 + off - COL_GU) // D_MODEL))
    full = lambda a: pl.BlockSpec(a.shape, lambda i: (0,) * a.ndim)
    ln_g = ln_g.reshape(1, GM_WIDTH)
    ln_b = ln_b.reshape(1, GM_WIDTH)
    if single_position:
        ws = jnp.repeat(ws[:, 0, 0], GM_CG).reshape(1, GM_WIDTH)
        bs = jnp.repeat(bs[:, 0], GM_CG).reshape(1, GM_WIDTH)
        kernel = _mix_step_kernel
        out_specs = [pl.BlockSpec((tm, D_MODEL), lambda i: (i, 0)),
                     pl.BlockSpec((tm, GM_WIDTH), lambda i: (i, 0))]
        out_shape = [jax.ShapeDtypeStruct((m, D_MODEL), F32), jax.ShapeDtypeStruct((m, GM_WIDTH), F32)]
        scratch = []
    else:
        bs = bs.T
        kernel = _mix_kernel
        out_specs = pl.BlockSpec((tm, D_MODEL), lambda i: (i, 0))
        out_shape = jax.ShapeDtypeStruct((m, D_MODEL), F32)
        scratch = [pltpu.VMEM((tm, GM_WIDTH), BF16), pltpu.VMEM((tm, D_MODEL), F32),
                   pltpu.VMEM((tm, D_MODEL), BF16)]
    return pl.pallas_call(
        kernel,
        grid=(m // tm,),
        in_specs=[
            pl.BlockSpec((tm, D_MODEL), lambda i: (i, 0)),
            zcol(COL_GU), zcol(COL_GV), zcol(COL_AR), zcol(COL_AG),
            pl.BlockSpec((tm, RET_V), lambda i: (i, 0)),
            full(ln_g), full(ln_b), full(ws), full(bs), full(wret), full(wgm), full(wo),
        ],
        out_specs=out_specs,
        out_shape=out_shape,
        scratch_shapes=scratch,
        compiler_params=_params("parallel"),
        name="mix_step" if single_position else "mix",
    )(x, z, z, z, z, og, ln_g, ln_b, ws, bs, wret, wgm, wo)


def _ffn_kernel(h_ref, g_ref, win_ref, wdown_ref, gf_ref, y_ref, *, final_norm):
    h = h_ref[...]
    hn = (_rms(h) * g_ref[...]).astype(BF16)
    f = _dot(hn, win_ref[...])
    f_gate = f[:, :D_FF]
    f_up = f[:, D_FF:]
    act = (f_gate * jax.nn.sigmoid(f_gate) * f_up).astype(BF16)
    out = h + _dot(act, wdown_ref[...])
    if final_norm:
        out = _rms(out) * gf_ref[...]
    y_ref[...] = out


def _ffn(h, g, win, wdown, g_final, *, tm, final_norm):
    m = h.shape[0]
    full = lambda a: pl.BlockSpec(a.shape, lambda i: (0,) * a.ndim)
    g = g.reshape(1, D_MODEL)
    g_final = g_final.reshape(1, D_MODEL)
    return pl.pallas_call(
        functools.partial(_ffn_kernel, final_norm=final_norm),
        grid=(m // tm,),
        in_specs=[pl.BlockSpec((tm, D_MODEL), lambda i: (i, 0)), full(g), full(win), full(wdown),
                  full(g_final)],
        out_specs=pl.BlockSpec((tm, D_MODEL), lambda i: (i, 0)),
        out_shape=jax.ShapeDtypeStruct((m, D_MODEL), F32),
        compiler_params=_params("parallel"),
        name="ffn",
    )(h, g, win, wdown, g_final)


def _rope_tables(pos):
    inv = ROPE_BASE ** (-jnp.arange(ROPE_HALF, dtype=F32) / ROPE_HALF)
    ang = pos[:, None] * inv[None, :]
    return jnp.cos(ang), jnp.sin(ang)


def kernel(x_prompt, x_sample, state_ret, norm_mix_g, w_in, w_ret_o, gm_ln_g, gm_ln_b, gm_ws, gm_bs,
           w_gm_o, w_o, norm_ffn_g, w_ffn_in, w_ffn_down, norm_final_g):
    batch, seq, _ = x_prompt.shape
    dec_batch, dec_seq, _ = x_sample.shape
    depth = w_in.shape[0]
    assert dec_seq == 1 and seq % PR_BLOCK == 0

    cos_p, sin_p = _rope_tables(jnp.arange(seq, dtype=F32))
    cos_s, sin_s = _rope_tables(PAST_LEN + jnp.arange(dec_seq, dtype=F32))

    hp = x_prompt.reshape(batch * seq, D_MODEL)
    hs = x_sample.reshape(dec_batch, D_MODEL)
    ret_p, ret_s, gmv_s = [], [], []
    for l in range(depth):
        last = l == depth - 1
        win = w_in[l].astype(BF16)
        wret = w_ret_o[l].astype(BF16)
        wgm = w_gm_o[l].astype(BF16)
        wo = w_o[l].astype(BF16)
        wfin = w_ffn_in[l].astype(BF16)
        wfdown = w_ffn_down[l].astype(BF16)

        zs = _in_proj(hs, norm_mix_g[l], win, col0=0, ncols=D_IN, tm=dec_batch, tn=1024, out_dtype=F32)
        og, sp, ogs, ss = _proj_ret(hp, norm_mix_g[l], win, cos_p, sin_p, zs, cos_s, sin_s, state_ret[l],
                                    batch=batch, seq=seq)

        z = _in_proj(hp, norm_mix_g[l], win, col0=RET_COLS, ncols=D_IN - RET_COLS, tm=2048, tn=2048,
                     out_dtype=BF16)
        h = _mix(hp, z, og, gm_ln_g[l], gm_ln_b[l], gm_ws[l], gm_bs[l], wret, wgm, wo,
                 z_col0=0, tm=512, single_position=False)
        hp = _ffn(h, norm_ffn_g[l], wfin, wfdown, norm_final_g, tm=512, final_norm=last)
        ret_p.append(sp)

        h, vs = _mix(hs, zs, ogs.reshape(dec_batch, RET_V), gm_ln_g[l], gm_ln_b[l], gm_ws[l], gm_bs[l],
                     wret, wgm, wo, z_col0=COL_GU, tm=dec_batch, single_position=True)
        hs = _ffn(h, norm_ffn_g[l], wfin, wfdown, norm_final_g, tm=dec_batch, final_norm=last)
        ret_s.append(ss)
        gmv_s.append(vs.reshape(dec_batch, dec_seq, GM_WIDTH))

    return (hp.reshape(batch, seq, D_MODEL), hs.reshape(dec_batch, dec_seq, D_MODEL),
            jnp.stack(ret_p), jnp.stack(ret_s), jnp.stack(gmv_s))
```

```python
import functools
import math

import numpy as np
import jax
import jax.numpy as jnp
from jax import lax
from jax.experimental import pallas as pl
from jax.experimental.pallas import tpu as pltpu

D_MODEL = 1024
PAST_LEN = 16384
RET_DK = 256
RET_HEADS = D_MODEL // 256
RET_DV = 2 * RET_DK
RET_QK = RET_HEADS * RET_DK
RET_V = RET_HEADS * RET_DV
RET_CHUNK = 128
ROPE_BASE = 10000.0
ROPE_HALF = RET_DK // 2
GM_GROUPS = 4
GM_WIDTH = D_MODEL
GM_CG = GM_WIDTH // GM_GROUPS
GM_CHUNK = 128
D_FF = ((8 * D_MODEL // 3 + 255) // 256) * 256
EPS = 1e-6
D_IN = 2 * RET_QK + 2 * RET_V + 2 * GM_WIDTH + 2 * D_MODEL

COL_Q = 0
COL_K = RET_QK
COL_V = 2 * RET_QK
COL_G = 2 * RET_QK + RET_V
COL_GU = 2 * RET_QK + 2 * RET_V
COL_GV = COL_GU + GM_WIDTH
COL_AR = COL_GV + GM_WIDTH
COL_AG = COL_AR + D_MODEL

LOG_GAMMA = tuple(float(np.log1p(-np.exp2(np.float32(-5.0 - h)))) for h in range(RET_HEADS))

VMEM_LIMIT_BYTES = 56 * 1024 * 1024
MXU_COLS = 256

F32 = jnp.float32
BF16 = jnp.bfloat16


def _dot(a, b):
    return jnp.dot(a, b, preferred_element_type=F32)


def _rms(x):
    return x * lax.rsqrt(jnp.mean(x * x, axis=-1, keepdims=True) + EPS)


def _params(*semantics):
    return pltpu.CompilerParams(dimension_semantics=semantics, vmem_limit_bytes=VMEM_LIMIT_BYTES)


SIDE_CAST_BLOCKS = 16


def _in_proj_kernel(*refs, n_side, emit_w):
    x_ref, g_ref, w_ref = refs[:3]
    side_in = refs[3:3 + n_side]
    z_ref = refs[3 + n_side]
    wb_ref = refs[4 + n_side] if emit_w else None
    side_out = refs[4 + n_side + emit_w:4 + 2 * n_side + emit_w]
    xn_ref = refs[-1]

    @pl.when(pl.program_id(1) == 0)
    def _():
        xn_ref[...] = (_rms(x_ref[...]) * g_ref[...]).astype(BF16)

    w = w_ref[...].astype(BF16)
    if emit_w:
        wb_ref[...] = w
    z_ref[...] = _dot(xn_ref[...], w).astype(z_ref.dtype)
    for src, dst in zip(side_in, side_out):
        dst[...] = src[...].astype(BF16)


def _in_proj(x, g, w, *, col0, ncols, tm, tn, out_dtype, emit_w=False, side_casts=()):
    m = x.shape[0]
    tile0 = col0 // tn
    gi, gj = m // tm, ncols // tn
    assert col0 % tn == 0 and ncols % tn == 0 and m % tm == 0
    assert not emit_w or gi == 1
    assert not side_casts or (gi * gj) % SIDE_CAST_BLOCKS == 0
    side_block = lambda i, j: ((i * gj + j) * SIDE_CAST_BLOCKS // (gi * gj), 0)
    side_specs = [pl.BlockSpec((a.shape[0] // SIDE_CAST_BLOCKS, a.shape[1]), side_block) for a in side_casts]
    out_specs = [pl.BlockSpec((tm, tn), lambda i, j: (i, j))]
    out_shape = [jax.ShapeDtypeStruct((m, ncols), out_dtype)]
    if emit_w:
        out_specs.append(pl.BlockSpec((D_MODEL, tn), lambda i, j: (0, j)))
        out_shape.append(jax.ShapeDtypeStruct((D_MODEL, ncols), BF16))
    return pl.pallas_call(
        functools.partial(_in_proj_kernel, n_side=len(side_casts), emit_w=emit_w),
        grid=(gi, gj),
        in_specs=[
            pl.BlockSpec((tm, D_MODEL), lambda i, j: (i, 0)),
            pl.BlockSpec((1, D_MODEL), lambda i, j: (0, 0)),
            pl.BlockSpec((D_MODEL, tn), lambda i, j: (0, tile0 + j)),
        ] + side_specs,
        out_specs=out_specs + side_specs,
        out_shape=out_shape + [jax.ShapeDtypeStruct(a.shape, BF16) for a in side_casts],
        scratch_shapes=[pltpu.VMEM((tm, D_MODEL), BF16)],
        compiler_params=_params("parallel", "arbitrary"),
        name="in_proj",
    )(x, g.reshape(1, D_MODEL), w, *side_casts)


def _rotate(x, cos, sin):
    x1 = x[:, :ROPE_HALF]
    x2 = x[:, ROPE_HALF:]
    return jnp.concatenate([x1 * cos - x2 * sin, x1 * sin + x2 * cos], axis=-1)


RET_COLS = COL_GU
PR_BLOCK = 512
PR_CHUNKS = PR_BLOCK // RET_CHUNK
PR_TILE = RET_COLS // PR_CHUNKS
assert PR_TILE % RET_DV == 0


def _decay_tables():
    L = RET_CHUNK
    lg = np.array(LOG_GAMMA, np.float32)
    n = np.arange(L, dtype=np.float32)
    diff = n[:, None] - n[None, :]
    intra = np.where(diff[None] >= 0, np.exp(np.maximum(diff, 0.0)[None] * lg[:, None, None]), 0.0)
    q_decay = np.exp((n + 1.0)[None, :, None] * lg[:, None, None])
    k_decay = np.exp((L - 1.0 - n)[None, :, None] * lg[:, None, None])
    return (jnp.asarray(intra, F32), jnp.asarray(np.broadcast_to(q_decay, (RET_HEADS, L, RET_DV)), F32),
            jnp.asarray(np.broadcast_to(k_decay, (RET_HEADS, L, RET_DK)), F32))


def _retention_chunk(z_ref, rows, cos, sin, intra_ref, qd_ref, kd_ref, st_ref, og_ref, before_head):
    def zcols(start, width):
        tile, off = divmod(start, PR_TILE)
        return z_ref[tile, rows, off:off + width]

    for h in range(RET_HEADS):
        before_head(h)
        q = _rotate(zcols(COL_Q + h * RET_DK, RET_DK).astype(F32), cos, sin)
        k = _rotate(zcols(COL_K + h * RET_DK, RET_DK).astype(F32), cos, sin) * (RET_DK ** -0.5)
        v = zcols(COL_V + h * RET_DV, RET_DV)
        qb = q.astype(BF16)
        scores = lax.dot_general(qb, k.astype(BF16), (((1,), (1,)), ((), ())),
                                 preferred_element_type=F32) * intra_ref[h]
        state = st_ref[0, h]
        o = _dot(scores.astype(BF16), v) + _dot(qb, state.astype(BF16)) * qd_ref[h]
        st_ref[0, h] = math.exp(RET_CHUNK * LOG_GAMMA[h]) * state + lax.dot_general(
            (k * kd_ref[h]).astype(BF16), v, (((0,), (0,)), ((), ())), preferred_element_type=F32)
        gate = zcols(COL_G + h * RET_DV, RET_DV).astype(F32)
        og_ref[:, h * RET_DV:(h + 1) * RET_DV] = (gate * jax.nn.sigmoid(gate) * _rms(o)).astype(og_ref.dtype)


def _row_to_col(row):
    n = row.shape[1]
    eye = lax.broadcasted_iota(jnp.int32, (n, n), 0) == lax.broadcasted_iota(jnp.int32, (n, n), 1)
    return jnp.sum(jnp.where(eye, row, 0.0), axis=1, keepdims=True)


def _retention_step_head(h, z_ref, cos, sin, st_ref, og_ref, new_st_ref):
    q = _rotate(z_ref[0, :, COL_Q + h * RET_DK:COL_Q + (h + 1) * RET_DK], cos, sin)
    k = _rotate(z_ref[0, :, COL_K + h * RET_DK:COL_K + (h + 1) * RET_DK], cos, sin) * (RET_DK ** -0.5)
    v = z_ref[0, :, COL_V + h * RET_DV:COL_V + (h + 1) * RET_DV]
    new_state = math.exp(LOG_GAMMA[h]) * st_ref[0, h] + _row_to_col(k) * v
    new_st_ref[0, h] = new_state
    o = jnp.sum(_row_to_col(q) * new_state, axis=0, keepdims=True)
    gate = z_ref[0, :, COL_G + h * RET_DV:COL_G + (h + 1) * RET_DV]
    og_ref[0, :, h * RET_DV:(h + 1) * RET_DV] = gate * jax.nn.sigmoid(gate) * _rms(o)


def _proj_ret_kernel(x_ref, g_ref, w_ref, cos_ref, sin_ref, intra_ref, qd_ref, kd_ref,
                     zs_ref, cos_s_ref, sin_s_ref, sst_ref,
                     og_ref, st_ref, ogs_ref, new_sst_ref,
                     xn_ref, za_ref, zb_ref, *, blocks_per_seq):
    t = pl.program_id(0)
    j = pl.program_id(1)
    ret_block = jnp.maximum(t - 1, 0)

    @pl.when(j == 0)
    def _():
        xn_ref[...] = (_rms(x_ref[...]) * g_ref[...]).astype(BF16)

    @pl.when(jnp.logical_and(t == 0, j == 0))
    def _():
        zb_ref[...] = jnp.zeros_like(zb_ref)

    @pl.when(jnp.logical_and(ret_block % blocks_per_seq == 0, j == 0))
    def _():
        st_ref[...] = jnp.zeros_like(st_ref)

    rows = pl.ds(pl.multiple_of(j * RET_CHUNK, RET_CHUNK), RET_CHUNK)

    def step(z_write, z_read):
        n_pieces = PR_TILE // MXU_COLS

        def before_head(h):
            for p in range(h * n_pieces // RET_HEADS, (h + 1) * n_pieces // RET_HEADS):
                cols = slice(p * MXU_COLS, (p + 1) * MXU_COLS)
                w_cols = pl.ds(pl.multiple_of(j * PR_TILE + p * MXU_COLS, MXU_COLS), MXU_COLS)
                z_write[j, :, cols] = _dot(xn_ref[...], w_ref[:, w_cols]).astype(BF16)
            _retention_step_head(h, zs_ref, cos_s_ref[...], sin_s_ref[...], sst_ref, ogs_ref, new_sst_ref)

        _retention_chunk(z_read, rows, cos_ref[...], sin_ref[...], intra_ref, qd_ref, kd_ref, st_ref, og_ref,
                         before_head)

    @pl.when(t % 2 == 0)
    def _():
        step(za_ref, zb_ref)

    @pl.when(t % 2 == 1)
    def _():
        step(zb_ref, za_ref)


def _proj_ret(x, g, w_bf16, cos, sin, z_sample, cos_s, sin_s, state_sample, *, batch, seq):
    n_blocks = batch * seq // PR_BLOCK
    blocks_per_seq = seq // PR_BLOCK
    dec_batch = z_sample.shape[0]
    assert dec_batch <= (n_blocks + 1) * PR_CHUNKS
    intra, q_decay, k_decay = _decay_tables()
    full = lambda a: pl.BlockSpec(a.shape, lambda t, j: (0,) * a.ndim, pipeline_mode=pl.Buffered(1))
    proj_block = lambda t: jnp.minimum(t, n_blocks - 1)
    ret_block = lambda t: jnp.maximum(t - 1, 0)
    chunk_of_seq = lambda t, j: (ret_block(t) % blocks_per_seq) * PR_CHUNKS + j
    og_block = lambda t: jnp.where(t == 0, n_blocks, t - 1)
    sample = lambda t, j: jnp.minimum(t * PR_CHUNKS + j, dec_batch - 1)
    state_spec = pl.BlockSpec((1, RET_HEADS, RET_DK, RET_DV), lambda t, j: (sample(t, j), 0, 0, 0))
    return pl.pallas_call(
        functools.partial(_proj_ret_kernel, blocks_per_seq=blocks_per_seq),
        grid=(n_blocks + 1, PR_CHUNKS),
        in_specs=[
            pl.BlockSpec((PR_BLOCK, D_MODEL), lambda t, j: (proj_block(t), 0)),
            pl.BlockSpec((1, D_MODEL), lambda t, j: (0, 0)),
            pl.BlockSpec((D_MODEL, RET_COLS), lambda t, j: (0, 0), pipeline_mode=pl.Buffered(1)),
            pl.BlockSpec((RET_CHUNK, ROPE_HALF), lambda t, j: (chunk_of_seq(t, j), 0)),
            pl.BlockSpec((RET_CHUNK, ROPE_HALF), lambda t, j: (chunk_of_seq(t, j), 0)),
            full(intra), full(q_decay), full(k_decay),
            pl.BlockSpec((1, 1, RET_COLS), lambda t, j: (sample(t, j), 0, 0)),
            full(cos_s), full(sin_s),
            state_spec,
        ],
        out_specs=[
            pl.BlockSpec((RET_CHUNK, RET_V), lambda t, j: (og_block(t) * PR_CHUNKS + j, 0)),
            pl.BlockSpec((1, RET_HEADS, RET_DK, RET_DV), lambda t, j: (ret_block(t) // blocks_per_seq, 0, 0, 0)),
            pl.BlockSpec((1, 1, RET_V), lambda t, j: (sample(t, j), 0, 0)),
            state_spec,
        ],
        out_shape=[
            jax.ShapeDtypeStruct(((n_blocks + 1) * PR_BLOCK, RET_V), BF16),
            jax.ShapeDtypeStruct((batch, RET_HEADS, RET_DK, RET_DV), F32),
            jax.ShapeDtypeStruct((dec_batch, 1, RET_V), F32),
            jax.ShapeDtypeStruct(state_sample.shape, F32),
        ],
        scratch_shapes=[
            pltpu.VMEM((PR_BLOCK, D_MODEL), BF16),
            pltpu.VMEM((PR_CHUNKS, PR_BLOCK, PR_TILE), BF16),
            pltpu.VMEM((PR_CHUNKS, PR_BLOCK, PR_TILE), BF16),
        ],
        compiler_params=_params("arbitrary", "arbitrary"),
        name="proj_ret",
    )(x, g.reshape(1, D_MODEL), w_bf16, cos, sin, intra, q_decay, k_decay,
      z_sample.reshape(dec_batch, 1, D_IN), cos_s, sin_s, state_sample)


def _gelu_tanh(x):
    c = math.sqrt(2.0 / math.pi)
    return x * (0.5 + 0.5 * jnp.tanh(x * (c + (c * 0.044715) * (x * x))))


def _layer_norm(x, g, b):
    xc = x - jnp.mean(x, axis=-1, keepdims=True)
    return xc * lax.rsqrt(jnp.mean(xc * xc, axis=-1, keepdims=True) + EPS) * g + b


def _merge(x_ref, ar_ref, ag_ref, og_ref, gm, wret_ref, wgm_ref, wo_ref, h_ref):
    branch_ret = _dot(og_ref[...].astype(BF16), wret_ref[...])
    branch_gm = _dot(gm, wgm_ref[...])
    m = (jax.nn.sigmoid(ar_ref[...].astype(F32)) * branch_ret
         + jax.nn.sigmoid(ag_ref[...].astype(F32)) * branch_gm)
    h_ref[...] = x_ref[...] + _dot(m.astype(BF16), wo_ref[...])


def _mix_kernel(x_ref, gu_ref, gv_ref, ar_ref, ag_ref, og_ref, lng_ref, lnb_ref, ws_ref, bs_ref,
                wret_ref, wgm_ref, wo_ref, h_ref, gm_ref, br_ref, m_ref):
    L = GM_CHUNK
    n_chunks = x_ref.shape[0] // L
    n_pieces = D_MODEL // MXU_COLS
    piece = lambda p: slice(p * MXU_COLS, (p + 1) * MXU_COLS)
    causal = lax.broadcasted_iota(jnp.int32, (L, L), 0) >= lax.broadcasted_iota(jnp.int32, (L, L), 1)
    w_causal = [jnp.where(causal, ws_ref[g], 0.0).astype(BF16) for g in range(GM_GROUPS)]
    for c in range(n_chunks):
        for p in range(c * n_pieces // n_chunks, (c + 1) * n_pieces // n_chunks):
            br_ref[:, piece(p)] = _dot(og_ref[...], wret_ref[:, piece(p)])
        rows = slice(c * L, (c + 1) * L)
        u = _gelu_tanh(gu_ref[rows, :].astype(F32))
        v = _layer_norm(_gelu_tanh(gv_ref[rows, :].astype(F32)), lng_ref[...], lnb_ref[...]).astype(BF16)
        for g in range(GM_GROUPS):
            cols = slice(g * GM_CG, (g + 1) * GM_CG)
            mixed = _dot(w_causal[g], v[:, cols]) + bs_ref[:, g:g + 1]
            gm_ref[rows, cols] = (u[:, cols] * mixed).astype(BF16)
    for p in range(n_pieces):
        branch_gm = _dot(gm_ref[...], wgm_ref[:, piece(p)])
        m_ref[:, piece(p)] = (jax.nn.sigmoid(ar_ref[:, piece(p)].astype(F32)) * br_ref[:, piece(p)]
                              + jax.nn.sigmoid(ag_ref[:, piece(p)].astype(F32)) * branch_gm).astype(BF16)
    for p in range(n_pieces):
        h_ref[:, piece(p)] = x_ref[:, piece(p)] + _dot(m_ref[...], wo_ref[:, piece(p)])


def _mix_step_kernel(x_ref, gu_ref, gv_ref, ar_ref, ag_ref, og_ref, lng_ref, lnb_ref, ws_ref, bs_ref,
                     wret_ref, wgm_ref, wo_ref, h_ref, v_ref):
    u = _gelu_tanh(gu_ref[...])
    v = _layer_norm(_gelu_tanh(gv_ref[...]), lng_ref[...], lnb_ref[...])
    v_ref[...] = v
    gm = (u * (ws_ref[...] * v + bs_ref[...])).astype(BF16)
    _merge(x_ref, ar_ref, ag_ref, og_ref, gm, wret_ref, wgm_ref, wo_ref, h_ref)


def _mix(x, z, og, ln_g, ln_b, ws, bs, wret, wgm, wo, *, z_col0, tm, single_position):
    m = x.shape[0]
    zcol = lambda off: pl.BlockSpec((tm, D_MODEL), lambda i: (i, (z_col0 + off - COL_GU) // D_MODEL))
    full = lambda a: pl.BlockSpec(a.shape, lambda i: (0,) * a.ndim)
    ln_g = ln_g.reshape(1, GM_WIDTH)
    ln_b = ln_b.reshape(1, GM_WIDTH)
    if single_position:
        ws = jnp.repeat(ws[:, 0, 0], GM_CG).reshape(1, GM_WIDTH)
        bs = jnp.repeat(bs[:, 0], GM_CG).reshape(1, GM_WIDTH)
        kernel = _mix_step_kernel
        out_specs = [pl.BlockSpec((tm, D_MODEL), lambda i: (i, 0)),
                     pl.BlockSpec((tm, GM_WIDTH), lambda i: (i, 0))]
        out_shape = [jax.ShapeDtypeStruct((m, D_MODEL), F32), jax.ShapeDtypeStruct((m, GM_WIDTH), F32)]
        scratch = []
    else:
        bs = bs.T
        kernel = _mix_kernel
        out_specs = pl.BlockSpec((tm, D_MODEL), lambda i: (i, 0))
        out_shape = jax.ShapeDtypeStruct((m, D_MODEL), F32)
        scratch = [pltpu.VMEM((tm, GM_WIDTH), BF16), pltpu.VMEM((tm, D_MODEL), F32),
                   pltpu.VMEM((tm, D_MODEL), BF16)]
    return pl.pallas_call(
        kernel,
        grid=(m // tm,),
        in_specs=[
            pl.BlockSpec((tm, D_MODEL), lambda i: (i, 0)),
            zcol(COL_GU), zcol(COL_GV), zcol(COL_AR), zcol(COL_AG),
            pl.BlockSpec((tm, RET_V), lambda i: (i, 0)),
            full(ln_g), full(ln_b), full(ws), full(bs), full(wret), full(wgm), full(wo),
        ],
        out_specs=out_specs,
        out_shape=out_shape,
        scratch_shapes=scratch,
        compiler_params=_params("parallel"),
        name="mix_step" if single_position else "mix",
    )(x, z, z, z, z, og, ln_g, ln_b, ws, bs, wret, wgm, wo)


def _ffn_kernel(h_ref, g_ref, win_ref, wdown_ref, gf_ref, y_ref, *, final_norm):
    h = h_ref[...]
    hn = (_rms(h) * g_ref[...]).astype(BF16)
    f = _dot(hn, win_ref[...])
    f_gate = f[:, :D_FF]
    f_up = f[:, D_FF:]
    act = (f_gate * jax.nn.sigmoid(f_gate) * f_up).astype(BF16)
    out = h + _dot(act, wdown_ref[...])
    if final_norm:
        out = _rms(out) * gf_ref[...]
    y_ref[...] = out


def _ffn(h, g, win, wdown, g_final, *, tm, final_norm):
    m = h.shape[0]
    full = lambda a: pl.BlockSpec(a.shape, lambda i: (0,) * a.ndim)
    g = g.reshape(1, D_MODEL)
    g_final = g_final.reshape(1, D_MODEL)
    return pl.pallas_call(
        functools.partial(_ffn_kernel, final_norm=final_norm),
        grid=(m // tm,),
        in_specs=[pl.BlockSpec((tm, D_MODEL), lambda i: (i, 0)), full(g), full(win), full(wdown),
                  full(g_final)],
        out_specs=pl.BlockSpec((tm, D_MODEL), lambda i: (i, 0)),
        out_shape=jax.ShapeDtypeStruct((m, D_MODEL), F32),
        compiler_params=_params("parallel"),
        name="ffn",
    )(h, g, win, wdown, g_final)


def _rope_tables(pos):
    inv = ROPE_BASE ** (-jnp.arange(ROPE_HALF, dtype=F32) / ROPE_HALF)
    ang = pos[:, None] * inv[None, :]
    return jnp.cos(ang), jnp.sin(ang)


def kernel(x_prompt, x_sample, state_ret, norm_mix_g, w_in, w_ret_o, gm_ln_g, gm_ln_b, gm_ws, gm_bs,
           w_gm_o, w_o, norm_ffn_g, w_ffn_in, w_ffn_down, norm_final_g):
    batch, seq, _ = x_prompt.shape
    dec_batch, dec_seq, _ = x_sample.shape
    depth = w_in.shape[0]
    assert dec_seq == 1 and seq % PR_BLOCK == 0

    cos_p, sin_p = _rope_tables(jnp.arange(seq, dtype=F32))
    cos_s, sin_s = _rope_tables(PAST_LEN + jnp.arange(dec_seq, dtype=F32))

    hp = x_prompt.reshape(batch * seq, D_MODEL)
    hs = x_sample.reshape(dec_batch, D_MODEL)
    ret_p, ret_s, gmv_s = [], [], []
    for l in range(depth):
        last = l == depth - 1
        zs, win = _in_proj(hs, norm_mix_g[l], w_in[l], col0=0, ncols=D_IN, tm=dec_batch, tn=1024,
                           out_dtype=F32, emit_w=True)
        og, sp, ogs, ss = _proj_ret(hp, norm_mix_g[l], win, cos_p, sin_p, zs, cos_s, sin_s, state_ret[l],
                                    batch=batch, seq=seq)

        z, wret, wgm, wo, wfin, wfdown = _in_proj(
            hp, norm_mix_g[l], win, col0=RET_COLS, ncols=D_IN - RET_COLS, tm=2048, tn=1024, out_dtype=BF16,
            side_casts=(w_ret_o[l], w_gm_o[l], w_o[l], w_ffn_in[l], w_ffn_down[l]))
        h = _mix(hp, z, og, gm_ln_g[l], gm_ln_b[l], gm_ws[l], gm_bs[l], wret, wgm, wo,
                 z_col0=0, tm=512, single_position=False)
        hp = _ffn(h, norm_ffn_g[l], wfin, wfdown, norm_final_g, tm=512, final_norm=last)
        ret_p.append(sp)

        h, vs = _mix(hs, zs, ogs.reshape(dec_batch, RET_V), gm_ln_g[l], gm_ln_b[l], gm_ws[l], gm_bs[l],
                     wret, wgm, wo, z_col0=COL_GU, tm=dec_batch, single_position=True)
        hs = _ffn(h, norm_ffn_g[l], wfin, wfdown, norm_final_g, tm=dec_batch, final_norm=last)
        ret_s.append(ss)
        gmv_s.append(vs.reshape(dec_batch, dec_seq, GM_WIDTH))

    return (hp.reshape(batch, seq, D_MODEL), hs.reshape(dec_batch, dec_seq, D_MODEL),
            jnp.stack(ret_p), jnp.stack(ret_s), jnp.stack(gmv_s))
```

```python
import functools
import math

import numpy as np
import jax
import jax.numpy as jnp
from jax import lax
from jax.experimental import pallas as pl
from jax.experimental.pallas import tpu as pltpu

D_MODEL = 1024
PAST_LEN = 16384
RET_DK = 256
RET_HEADS = D_MODEL // 256
RET_DV = 2 * RET_DK
RET_QK = RET_HEADS * RET_DK
RET_V = RET_HEADS * RET_DV
RET_CHUNK = 128
ROPE_BASE = 10000.0
ROPE_HALF = RET_DK // 2
GM_GROUPS = 4
GM_WIDTH = D_MODEL
GM_CG = GM_WIDTH // GM_GROUPS
GM_CHUNK = 128
D_FF = ((8 * D_MODEL // 3 + 255) // 256) * 256
EPS = 1e-6
D_IN = 2 * RET_QK + 2 * RET_V + 2 * GM_WIDTH + 2 * D_MODEL

COL_Q = 0
COL_K = RET_QK
COL_V = 2 * RET_QK
COL_G = 2 * RET_QK + RET_V
COL_GU = 2 * RET_QK + 2 * RET_V
COL_GV = COL_GU + GM_WIDTH
COL_AR = COL_GV + GM_WIDTH
COL_AG = COL_AR + D_MODEL

LOG_GAMMA = tuple(float(np.log1p(-np.exp2(np.float32(-5.0 - h)))) for h in range(RET_HEADS))

VMEM_LIMIT_BYTES = 56 * 1024 * 1024
MXU_COLS = 256

F32 = jnp.float32
BF16 = jnp.bfloat16


def _dot(a, b):
    return jnp.dot(a, b, preferred_element_type=F32)


def _rms(x):
    return x * lax.rsqrt(jnp.mean(x * x, axis=-1, keepdims=True) + EPS)


def _params(*semantics):
    return pltpu.CompilerParams(dimension_semantics=semantics, vmem_limit_bytes=VMEM_LIMIT_BYTES)


SIDE_CAST_BLOCKS = 16


def _in_proj_kernel(*refs, n_side, emit_w):
    x_ref, g_ref, w_ref = refs[:3]
    side_in = refs[3:3 + n_side]
    z_ref = refs[3 + n_side]
    wb_ref = refs[4 + n_side] if emit_w else None
    side_out = refs[4 + n_side + emit_w:4 + 2 * n_side + emit_w]
    xn_ref = refs[-1]

    @pl.when(pl.program_id(1) == 0)
    def _():
        xn_ref[...] = (_rms(x_ref[...]) * g_ref[...]).astype(BF16)

    w = w_ref[...].astype(BF16)
    if emit_w:
        wb_ref[...] = w
    z_ref[...] = _dot(xn_ref[...], w).astype(z_ref.dtype)
    for src, dst in zip(side_in, side_out):
        dst[...] = src[...].astype(BF16)


def _in_proj(x, g, w, *, col0, ncols, tm, tn, out_dtype, emit_w=False, side_casts=()):
    m = x.shape[0]
    tile0 = col0 // tn
    gi, gj = m // tm, ncols // tn
    assert col0 % tn == 0 and ncols % tn == 0 and m % tm == 0
    assert not emit_w or gi == 1
    assert not side_casts or (gi * gj) % SIDE_CAST_BLOCKS == 0
    side_block = lambda i, j: ((i * gj + j) * SIDE_CAST_BLOCKS // (gi * gj), 0)
    side_specs = [pl.BlockSpec((a.shape[0] // SIDE_CAST_BLOCKS, a.shape[1]), side_block) for a in side_casts]
    out_specs = [pl.BlockSpec((tm, tn), lambda i, j: (i, j))]
    out_shape = [jax.ShapeDtypeStruct((m, ncols), out_dtype)]
    if emit_w:
        out_specs.append(pl.BlockSpec((D_MODEL, tn), lambda i, j: (0, j)))
        out_shape.append(jax.ShapeDtypeStruct((D_MODEL, ncols), BF16))
    return pl.pallas_call(
        functools.partial(_in_proj_kernel, n_side=len(side_casts), emit_w=emit_w),
        grid=(gi, gj),
        in_specs=[
            pl.BlockSpec((tm, D_MODEL), lambda i, j: (i, 0)),
            pl.BlockSpec((1, D_MODEL), lambda i, j: (0, 0)),
            pl.BlockSpec((D_MODEL, tn), lambda i, j: (0, tile0 + j)),
        ] + side_specs,
        out_specs=out_specs + side_specs,
        out_shape=out_shape + [jax.ShapeDtypeStruct(a.shape, BF16) for a in side_casts],
        scratch_shapes=[pltpu.VMEM((tm, D_MODEL), BF16)],
        compiler_params=_params("parallel", "arbitrary"),
        name="in_proj",
    )(x, g.reshape(1, D_MODEL), w, *side_casts)


def _rotate(x, cos, sin):
    x1 = x[:, :ROPE_HALF]
    x2 = x[:, ROPE_HALF:]
    return jnp.concatenate([x1 * cos - x2 * sin, x1 * sin + x2 * cos], axis=-1)


RET_COLS = COL_GU
PR_BLOCK = 512
PR_CHUNKS = PR_BLOCK // RET_CHUNK
PR_TILE = RET_COLS // PR_CHUNKS
assert PR_TILE % RET_DV == 0


def _decay_tables():
    L = RET_CHUNK
    lg = np.array(LOG_GAMMA, np.float32)
    n = np.arange(L, dtype=np.float32)
    diff = n[:, None] - n[None, :]
    intra = np.where(diff[None] >= 0, np.exp(np.maximum(diff, 0.0)[None] * lg[:, None, None]), 0.0)
    q_decay = np.exp((n + 1.0)[None, :, None] * lg[:, None, None])
    k_decay = np.exp((L - 1.0 - n)[None, :, None] * lg[:, None, None])
    return (jnp.asarray(intra, F32), jnp.asarray(np.broadcast_to(q_decay, (RET_HEADS, L, RET_DV)), F32),
            jnp.asarray(np.broadcast_to(k_decay, (RET_HEADS, L, RET_DK)), F32))


def _retention_chunk(z_ref, rows, cos, sin, intra_ref, qd_ref, kd_ref, st_ref, og_ref, before_head):
    def zcols(start, width):
        tile, off = divmod(start, PR_TILE)
        return z_ref[tile, rows, off:off + width]

    for h in range(RET_HEADS):
        before_head(h)
        q = _rotate(zcols(COL_Q + h * RET_DK, RET_DK).astype(F32), cos, sin)
        k = _rotate(zcols(COL_K + h * RET_DK, RET_DK).astype(F32), cos, sin) * (RET_DK ** -0.5)
        v = zcols(COL_V + h * RET_DV, RET_DV)
        qb = q.astype(BF16)
        scores = lax.dot_general(qb, k.astype(BF16), (((1,), (1,)), ((), ())),
                                 preferred_element_type=F32) * intra_ref[h]
        state = st_ref[0, h]
        o = _dot(scores.astype(BF16), v) + _dot(qb, state.astype(BF16)) * qd_ref[h]
        st_ref[0, h] = math.exp(RET_CHUNK * LOG_GAMMA[h]) * state + lax.dot_general(
            (k * kd_ref[h]).astype(BF16), v, (((0,), (0,)), ((), ())), preferred_element_type=F32)
        gate = zcols(COL_G + h * RET_DV, RET_DV).astype(F32)
        og_ref[:, h * RET_DV:(h + 1) * RET_DV] = (gate * jax.nn.sigmoid(gate) * _rms(o)).astype(og_ref.dtype)


def _row_to_col(row):
    n = row.shape[1]
    eye = lax.broadcasted_iota(jnp.int32, (n, n), 0) == lax.broadcasted_iota(jnp.int32, (n, n), 1)
    return jnp.sum(jnp.where(eye, row, 0.0), axis=1, keepdims=True)


def _retention_step_head(h, row, z_ref, cos, sin, st_ref, og_ref, new_st_ref):
    q = _rotate(z_ref[row, COL_Q + h * RET_DK:COL_Q + (h + 1) * RET_DK], cos, sin)
    k = _rotate(z_ref[row, COL_K + h * RET_DK:COL_K + (h + 1) * RET_DK], cos, sin) * (RET_DK ** -0.5)
    v = z_ref[row, COL_V + h * RET_DV:COL_V + (h + 1) * RET_DV]
    new_state = math.exp(LOG_GAMMA[h]) * st_ref[0, h] + _row_to_col(k) * v
    new_st_ref[0, h] = new_state
    o = jnp.sum(_row_to_col(q) * new_state, axis=0, keepdims=True)
    gate = z_ref[row, COL_G + h * RET_DV:COL_G + (h + 1) * RET_DV]
    og_ref[row, h * RET_DV:(h + 1) * RET_DV] = gate * jax.nn.sigmoid(gate) * _rms(o)


def _proj_ret_kernel(x_ref, g_ref, w_ref, cos_ref, sin_ref, intra_ref, qd_ref, kd_ref,
                     zs_ref, cos_s_ref, sin_s_ref, sst_ref,
                     og_ref, st_ref, ogs_ref, new_sst_ref,
                     xn_ref, za_ref, zb_ref, *, blocks_per_seq):
    t = pl.program_id(0)
    j = pl.program_id(1)
    ret_block = jnp.maximum(t - 1, 0)
    sample_row = pl.ds(jnp.minimum(t * PR_CHUNKS + j, zs_ref.shape[0] - 1), 1)
    pos_rows = pl.ds(pl.multiple_of(((ret_block % blocks_per_seq) * PR_CHUNKS + j) * RET_CHUNK, RET_CHUNK),
                     RET_CHUNK)

    @pl.when(j == 0)
    def _():
        xn_ref[...] = (_rms(x_ref[...]) * g_ref[...]).astype(BF16)

    @pl.when(jnp.logical_and(t == 0, j == 0))
    def _():
        zb_ref[...] = jnp.zeros_like(zb_ref)

    @pl.when(jnp.logical_and(ret_block % blocks_per_seq == 0, j == 0))
    def _():
        st_ref[...] = jnp.zeros_like(st_ref)

    rows = pl.ds(pl.multiple_of(j * RET_CHUNK, RET_CHUNK), RET_CHUNK)

    def step(z_write, z_read):
        n_pieces = PR_TILE // MXU_COLS

        def before_head(h):
            for p in range(h * n_pieces // RET_HEADS, (h + 1) * n_pieces // RET_HEADS):
                cols = slice(p * MXU_COLS, (p + 1) * MXU_COLS)
                w_cols = pl.ds(pl.multiple_of(j * PR_TILE + p * MXU_COLS, MXU_COLS), MXU_COLS)
                z_write[j, :, cols] = _dot(xn_ref[...], w_ref[:, w_cols]).astype(BF16)
            _retention_step_head(h, sample_row, zs_ref, cos_s_ref[...], sin_s_ref[...], sst_ref, ogs_ref,
                                 new_sst_ref)

        _retention_chunk(z_read, rows, cos_ref[pos_rows, :], sin_ref[pos_rows, :], intra_ref, qd_ref, kd_ref,
                         st_ref, og_ref, before_head)

    @pl.when(t % 2 == 0)
    def _():
        step(za_ref, zb_ref)

    @pl.when(t % 2 == 1)
    def _():
        step(zb_ref, za_ref)


def _proj_ret(x, g, w_bf16, cos, sin, z_sample, cos_s, sin_s, state_sample, *, batch, seq):
    n_blocks = batch * seq // PR_BLOCK
    blocks_per_seq = seq // PR_BLOCK
    dec_batch = z_sample.shape[0]
    assert dec_batch <= (n_blocks + 1) * PR_CHUNKS
    intra, q_decay, k_decay = _decay_tables()
    full = lambda a: pl.BlockSpec(a.shape, lambda t, j: (0,) * a.ndim, pipeline_mode=pl.Buffered(1))
    proj_block = lambda t: jnp.minimum(t, n_blocks - 1)
    ret_block = lambda t: jnp.maximum(t - 1, 0)
    og_block = lambda t: jnp.where(t == 0, n_blocks, t - 1)
    sample = lambda t, j: jnp.minimum(t * PR_CHUNKS + j, dec_batch - 1)
    state_spec = pl.BlockSpec((1, RET_HEADS, RET_DK, RET_DV), lambda t, j: (sample(t, j), 0, 0, 0))
    return pl.pallas_call(
        functools.partial(_proj_ret_kernel, blocks_per_seq=blocks_per_seq),
        grid=(n_blocks + 1, PR_CHUNKS),
        in_specs=[
            pl.BlockSpec((PR_BLOCK, D_MODEL), lambda t, j: (proj_block(t), 0)),
            pl.BlockSpec((1, D_MODEL), lambda t, j: (0, 0)),
            pl.BlockSpec((D_MODEL, RET_COLS), lambda t, j: (0, 0), pipeline_mode=pl.Buffered(1)),
            full(cos), full(sin), full(intra), full(q_decay), full(k_decay),
            pl.BlockSpec((dec_batch, RET_COLS), lambda t, j: (0, 0), pipeline_mode=pl.Buffered(1)),
            full(cos_s), full(sin_s),
            state_spec,
        ],
        out_specs=[
            pl.BlockSpec((RET_CHUNK, RET_V), lambda t, j: (og_block(t) * PR_CHUNKS + j, 0)),
            pl.BlockSpec((1, RET_HEADS, RET_DK, RET_DV), lambda t, j: (ret_block(t) // blocks_per_seq, 0, 0, 0)),
            pl.BlockSpec((dec_batch, RET_V), lambda t, j: (0, 0)),
            state_spec,
        ],
        out_shape=[
            jax.ShapeDtypeStruct(((n_blocks + 1) * PR_BLOCK, RET_V), BF16),
            jax.ShapeDtypeStruct((batch, RET_HEADS, RET_DK, RET_DV), F32),
            jax.ShapeDtypeStruct((dec_batch, RET_V), F32),
            jax.ShapeDtypeStruct(state_sample.shape, F32),
        ],
        scratch_shapes=[
            pltpu.VMEM((PR_BLOCK, D_MODEL), BF16),
            pltpu.VMEM((PR_CHUNKS, PR_BLOCK, PR_TILE), BF16),
            pltpu.VMEM((PR_CHUNKS, PR_BLOCK, PR_TILE), BF16),
        ],
        compiler_params=_params("arbitrary", "arbitrary"),
        name="proj_ret",
    )(x, g.reshape(1, D_MODEL), w_bf16, cos, sin, intra, q_decay, k_decay,
      z_sample, cos_s, sin_s, state_sample)


def _gelu_tanh(x):
    c = math.sqrt(2.0 / math.pi)
    return x * (0.5 + 0.5 * jnp.tanh(x * (c + (c * 0.044715) * (x * x))))


def _layer_norm(x, g, b):
    xc = x - jnp.mean(x, axis=-1, keepdims=True)
    return xc * lax.rsqrt(jnp.mean(xc * xc, axis=-1, keepdims=True) + EPS) * g + b


def _merge(x_ref, ar_ref, ag_ref, og_ref, gm, wret_ref, wgm_ref, wo_ref, h_ref):
    branch_ret = _dot(og_ref[...].astype(BF16), wret_ref[...])
    branch_gm = _dot(gm, wgm_ref[...])
    m = (jax.nn.sigmoid(ar_ref[...].astype(F32)) * branch_ret
         + jax.nn.sigmoid(ag_ref[...].astype(F32)) * branch_gm)
    h_ref[...] = x_ref[...] + _dot(m.astype(BF16), wo_ref[...])


def _mix_kernel(x_ref, gu_ref, gv_ref, ar_ref, ag_ref, og_ref, lng_ref, lnb_ref, ws_ref, bs_ref,
                wret_ref, wgm_ref, wo_ref, h_ref, gm_ref, br_ref, m_ref):
    L = GM_CHUNK
    n_chunks = x_ref.shape[0] // L
    n_pieces = D_MODEL // MXU_COLS
    piece = lambda p: slice(p * MXU_COLS, (p + 1) * MXU_COLS)
    causal = lax.broadcasted_iota(jnp.int32, (L, L), 0) >= lax.broadcasted_iota(jnp.int32, (L, L), 1)
    w_causal = [jnp.where(causal, ws_ref[g], 0.0).astype(BF16) for g in range(GM_GROUPS)]
    for c in range(n_chunks):
        for p in range(c * n_pieces // n_chunks, (c + 1) * n_pieces // n_chunks):
            br_ref[:, piece(p)] = _dot(og_ref[...], wret_ref[:, piece(p)])
        rows = slice(c * L, (c + 1) * L)
        u = _gelu_tanh(gu_ref[rows, :].astype(F32))
        v = _layer_norm(_gelu_tanh(gv_ref[rows, :].astype(F32)), lng_ref[...], lnb_ref[...]).astype(BF16)
        for g in range(GM_GROUPS):
            cols = slice(g * GM_CG, (g + 1) * GM_CG)
            mixed = _dot(w_causal[g], v[:, cols]) + bs_ref[:, g:g + 1]
            gm_ref[rows, cols] = (u[:, cols] * mixed).astype(BF16)
    for p in range(n_pieces):
        branch_gm = _dot(gm_ref[...], wgm_ref[:, piece(p)])
        m_ref[:, piece(p)] = (jax.nn.sigmoid(ar_ref[:, piece(p)].astype(F32)) * br_ref[:, piece(p)]
                              + jax.nn.sigmoid(ag_ref[:, piece(p)].astype(F32)) * branch_gm).astype(BF16)
    for p in range(n_pieces):
        h_ref[:, piece(p)] = x_ref[:, piece(p)] + _dot(m_ref[...], wo_ref[:, piece(p)])


def _mix_step_kernel(x_ref, gu_ref, gv_ref, ar_ref, ag_ref, og_ref, lng_ref, lnb_ref, ws_ref, bs_ref,
                     wret_ref, wgm_ref, wo_ref, h_ref, v_ref):
    u = _gelu_tanh(gu_ref[...])
    v = _layer_norm(_gelu_tanh(gv_ref[...]), lng_ref[...], lnb_ref[...])
    v_ref[...] = v
    gm = (u * (ws_ref[...] * v + bs_ref[...])).astype(BF16)
    _merge(x_ref, ar_ref, ag_ref, og_ref, gm, wret_ref, wgm_ref, wo_ref, h_ref)


def _mix(x, z, og, ln_g, ln_b, ws, bs, wret, wgm, wo, *, z_col0, tm, single_position):
    m = x.shape[0]
    zcol = lambda off: pl.BlockSpec((tm, D_MODEL), lambda i: (i, (z_col0 + off - COL_GU) // D_MODEL))
    full = lambda a: pl.BlockSpec(a.shape, lambda i: (0,) * a.ndim)
    ln_g = ln_g.reshape(1, GM_WIDTH)
    ln_b = ln_b.reshape(1, GM_WIDTH)
    if single_position:
        ws = jnp.repeat(ws[:, 0, 0], GM_CG).reshape(1, GM_WIDTH)
        bs = jnp.repeat(bs[:, 0], GM_CG).reshape(1, GM_WIDTH)
        kernel = _mix_step_kernel
        out_specs = [pl.BlockSpec((tm, D_MODEL), lambda i: (i, 0)),
                     pl.BlockSpec((tm, GM_WIDTH), lambda i: (i, 0))]
        out_shape = [jax.ShapeDtypeStruct((m, D_MODEL), F32), jax.ShapeDtypeStruct((m, GM_WIDTH), F32)]
        scratch = []
    else:
        bs = bs.T
        kernel = _mix_kernel
        out_specs = pl.BlockSpec((tm, D_MODEL), lambda i: (i, 0))
        out_shape = jax.ShapeDtypeStruct((m, D_MODEL), F32)
        scratch = [pltpu.VMEM((tm, GM_WIDTH), BF16), pltpu.VMEM((tm, D_MODEL), F32),
                   pltpu.VMEM((tm, D_MODEL), BF16)]
    return pl.pallas_call(
        kernel,
        grid=(m // tm,),
        in_specs=[
            pl.BlockSpec((tm, D_MODEL), lambda i: (i, 0)),
            zcol(COL_GU), zcol(COL_GV), zcol(COL_AR), zcol(COL_AG),
            pl.BlockSpec((tm, RET_V), lambda i: (i, 0)),
            full(ln_g), full(ln_b), full(ws), full(bs), full(wret), full(wgm), full(wo),
        ],
        out_specs=out_specs,
        out_shape=out_shape,
        scratch_shapes=scratch,
        compiler_params=_params("parallel"),
        name="mix_step" if single_position else "mix",
    )(x, z, z, z, z, og, ln_g, ln_b, ws, bs, wret, wgm, wo)


def _ffn_kernel(h_ref, g_ref, win_ref, wdown_ref, gf_ref, y_ref, *, final_norm):
    h = h_ref[...]
    hn = (_rms(h) * g_ref[...]).astype(BF16)
    f = _dot(hn, win_ref[...])
    f_gate = f[:, :D_FF]
    f_up = f[:, D_FF:]
    act = (f_gate * jax.nn.sigmoid(f_gate) * f_up).astype(BF16)
    out = h + _dot(act, wdown_ref[...])
    if final_norm:
        out = _rms(out) * gf_ref[...]
    y_ref[...] = out


def _ffn(h, g, win, wdown, g_final, *, tm, final_norm):
    m = h.shape[0]
    full = lambda a: pl.BlockSpec(a.shape, lambda i: (0,) * a.ndim)
    g = g.reshape(1, D_MODEL)
    g_final = g_final.reshape(1, D_MODEL)
    return pl.pallas_call(
        functools.partial(_ffn_kernel, final_norm=final_norm),
        grid=(m // tm,),
        in_specs=[pl.BlockSpec((tm, D_MODEL), lambda i: (i, 0)), full(g), full(win), full(wdown),
                  full(g_final)],
        out_specs=pl.BlockSpec((tm, D_MODEL), lambda i: (i, 0)),
        out_shape=jax.ShapeDtypeStruct((m, D_MODEL), F32),
        compiler_params=_params("parallel"),
        name="ffn",
    )(h, g, win, wdown, g_final)


def _rope_tables(pos):
    inv = ROPE_BASE ** (-jnp.arange(ROPE_HALF, dtype=F32) / ROPE_HALF)
    ang = pos[:, None] * inv[None, :]
    return jnp.cos(ang), jnp.sin(ang)


def kernel(x_prompt, x_sample, state_ret, norm_mix_g, w_in, w_ret_o, gm_ln_g, gm_ln_b, gm_ws, gm_bs,
           w_gm_o, w_o, norm_ffn_g, w_ffn_in, w_ffn_down, norm_final_g):
    batch, seq, _ = x_prompt.shape
    dec_batch, dec_seq, _ = x_sample.shape
    depth = w_in.shape[0]
    assert dec_seq == 1 and seq % PR_BLOCK == 0

    cos_p, sin_p = _rope_tables(jnp.arange(seq, dtype=F32))
    cos_s, sin_s = _rope_tables(PAST_LEN + jnp.arange(dec_seq, dtype=F32))

    hp = x_prompt.reshape(batch * seq, D_MODEL)
    hs = x_sample.reshape(dec_batch, D_MODEL)
    ret_p, ret_s, gmv_s = [], [], []
    for l in range(depth):
        last = l == depth - 1
        zs, win = _in_proj(hs, norm_mix_g[l], w_in[l], col0=0, ncols=D_IN, tm=dec_batch, tn=1024,
                           out_dtype=F32, emit_w=True)
        og, sp, ogs, ss = _proj_ret(hp, norm_mix_g[l], win, cos_p, sin_p, zs, cos_s, sin_s, state_ret[l],
                                    batch=batch, seq=seq)

        z, wret, wgm, wo, wfin, wfdown = _in_proj(
            hp, norm_mix_g[l], win, col0=RET_COLS, ncols=D_IN - RET_COLS, tm=2048, tn=1024, out_dtype=BF16,
            side_casts=(w_ret_o[l], w_gm_o[l], w_o[l], w_ffn_in[l], w_ffn_down[l]))
        h = _mix(hp, z, og, gm_ln_g[l], gm_ln_b[l], gm_ws[l], gm_bs[l], wret, wgm, wo,
                 z_col0=0, tm=512, single_position=False)
        hp = _ffn(h, norm_ffn_g[l], wfin, wfdown, norm_final_g, tm=512, final_norm=last)
        ret_p.append(sp)

        h, vs = _mix(hs, zs, ogs, gm_ln_g[l], gm_ln_b[l], gm_ws[l], gm_bs[l],
                     wret, wgm, wo, z_col0=COL_GU, tm=dec_batch, single_position=True)
        hs = _ffn(h, norm_ffn_g[l], wfin, wfdown, norm_final_g, tm=dec_batch, final_norm=last)
        ret_s.append(ss)
        gmv_s.append(vs.reshape(dec_batch, dec_seq, GM_WIDTH))

    return (hp.reshape(batch, seq, D_MODEL), hs.reshape(dec_batch, dec_seq, D_MODEL),
            jnp.stack(ret_p), jnp.stack(ret_s), jnp.stack(gmv_s))
```

```python
import functools
import math

import numpy as np
import jax
import jax.numpy as jnp
from jax import lax
from jax.experimental import pallas as pl
from jax.experimental.pallas import tpu as pltpu

D_MODEL = 1024
PAST_LEN = 16384
RET_DK = 256
RET_HEADS = D_MODEL // 256
RET_DV = 2 * RET_DK
RET_QK = RET_HEADS * RET_DK
RET_V = RET_HEADS * RET_DV
RET_CHUNK = 128
ROPE_BASE = 10000.0
ROPE_HALF = RET_DK // 2
GM_GROUPS = 4
GM_WIDTH = D_MODEL
GM_CG = GM_WIDTH // GM_GROUPS
GM_CHUNK = 128
D_FF = ((8 * D_MODEL // 3 + 255) // 256) * 256
EPS = 1e-6
D_IN = 2 * RET_QK + 2 * RET_V + 2 * GM_WIDTH + 2 * D_MODEL

COL_Q = 0
COL_K = RET_QK
COL_V = 2 * RET_QK
COL_G = 2 * RET_QK + RET_V
COL_GU = 2 * RET_QK + 2 * RET_V
COL_GV = COL_GU + GM_WIDTH
COL_AR = COL_GV + GM_WIDTH
COL_AG = COL_AR + D_MODEL

LOG_GAMMA = tuple(float(np.log1p(-np.exp2(np.float32(-5.0 - h)))) for h in range(RET_HEADS))

VMEM_LIMIT_BYTES = 56 * 1024 * 1024
MXU_COLS = 256

F32 = jnp.float32
BF16 = jnp.bfloat16


def _dot(a, b):
    return jnp.dot(a, b, preferred_element_type=F32)


def _rms(x):
    return x * lax.rsqrt(jnp.mean(x * x, axis=-1, keepdims=True) + EPS)


def _params(*semantics):
    return pltpu.CompilerParams(dimension_semantics=semantics, vmem_limit_bytes=VMEM_LIMIT_BYTES)


BF16_ROWS = 16


def _side_cast_specs(arrays, n_steps, step_of):
    specs = []
    for a in arrays:
        rows = a.shape[0]
        n_blocks = max(n for n in range(1, n_steps + 1) if rows % n == 0 and (rows // n) % BF16_ROWS == 0)
        specs.append(pl.BlockSpec((rows // n_blocks, a.shape[1]),
                                  lambda *ids, n_blocks=n_blocks: (step_of(*ids) * n_blocks // n_steps, 0)))
    return specs


def _side_cast(side_in, side_out):
    for src, dst in zip(side_in, side_out):
        dst[...] = src[...].astype(BF16)


def _in_proj_kernel(x_ref, g_ref, w_ref, z_ref, wb_ref, xn_ref):
    @pl.when(pl.program_id(0) == 0)
    def _():
        xn_ref[...] = (_rms(x_ref[...]) * g_ref[...]).astype(BF16)

    w = w_ref[...].astype(BF16)
    wb_ref[...] = w
    z_ref[...] = _dot(xn_ref[...], w)


def _in_proj(x, g, w, *, tn):
    m = x.shape[0]
    return pl.pallas_call(
        _in_proj_kernel,
        grid=(D_IN // tn,),
        in_specs=[
            pl.BlockSpec((m, D_MODEL), lambda j: (0, 0)),
            pl.BlockSpec((1, D_MODEL), lambda j: (0, 0)),
            pl.BlockSpec((D_MODEL, tn), lambda j: (0, j)),
        ],
        out_specs=[pl.BlockSpec((m, tn), lambda j: (0, j)), pl.BlockSpec((D_MODEL, tn), lambda j: (0, j))],
        out_shape=[jax.ShapeDtypeStruct((m, D_IN), F32), jax.ShapeDtypeStruct((D_MODEL, D_IN), BF16)],
        scratch_shapes=[pltpu.VMEM((m, D_MODEL), BF16)],
        compiler_params=_params("arbitrary"),
        name="in_proj",
    )(x, g.reshape(1, D_MODEL), w)


def _rotate(x, cos, sin):
    x1 = x[:, :ROPE_HALF]
    x2 = x[:, ROPE_HALF:]
    return jnp.concatenate([x1 * cos - x2 * sin, x1 * sin + x2 * cos], axis=-1)


RET_COLS = COL_GU
PR_BLOCK = 512
PR_CHUNKS = PR_BLOCK // RET_CHUNK
PR_TILE = RET_COLS // PR_CHUNKS
assert PR_TILE % RET_DV == 0


def _decay_tables():
    L = RET_CHUNK
    lg = np.array(LOG_GAMMA, np.float32)
    n = np.arange(L, dtype=np.float32)
    diff = n[:, None] - n[None, :]
    intra = np.where(diff[None] >= 0, np.exp(np.maximum(diff, 0.0)[None] * lg[:, None, None]), 0.0)
    q_decay = np.exp((n + 1.0)[None, :, None] * lg[:, None, None])
    k_decay = np.exp((L - 1.0 - n)[None, :, None] * lg[:, None, None])
    return (jnp.asarray(intra, F32), jnp.asarray(np.broadcast_to(q_decay, (RET_HEADS, L, RET_DV)), F32),
            jnp.asarray(np.broadcast_to(k_decay, (RET_HEADS, L, RET_DK)), F32))


def _retention_chunk(z_ref, rows, cos, sin, intra_ref, qd_ref, kd_ref, st_ref, og_ref, before_head):
    def zcols(start, width):
        tile, off = divmod(start, PR_TILE)
        return z_ref[tile, rows, off:off + width]

    for h in range(RET_HEADS):
        before_head(h)
        q = _rotate(zcols(COL_Q + h * RET_DK, RET_DK).astype(F32), cos, sin)
        k = _rotate(zcols(COL_K + h * RET_DK, RET_DK).astype(F32), cos, sin) * (RET_DK ** -0.5)
        v = zcols(COL_V + h * RET_DV, RET_DV)
        qb = q.astype(BF16)
        scores = lax.dot_general(qb, k.astype(BF16), (((1,), (1,)), ((), ())),
                                 preferred_element_type=F32) * intra_ref[h]
        state = st_ref[0, h]
        o = _dot(scores.astype(BF16), v) + _dot(qb, state.astype(BF16)) * qd_ref[h]
        st_ref[0, h] = math.exp(RET_CHUNK * LOG_GAMMA[h]) * state + lax.dot_general(
            (k * kd_ref[h]).astype(BF16), v, (((0,), (0,)), ((), ())), preferred_element_type=F32)
        gate = zcols(COL_G + h * RET_DV, RET_DV).astype(F32)
        og_ref[:, h * RET_DV:(h + 1) * RET_DV] = (gate * jax.nn.sigmoid(gate) * _rms(o)).astype(og_ref.dtype)


def _row_to_col(row):
    n = row.shape[1]
    eye = lax.broadcasted_iota(jnp.int32, (n, n), 0) == lax.broadcasted_iota(jnp.int32, (n, n), 1)
    return jnp.sum(jnp.where(eye, row, 0.0), axis=1, keepdims=True)


def _retention_step_head(h, row, z_ref, cos, sin, st_ref, og_ref, new_st_ref):
    q = _rotate(z_ref[row, COL_Q + h * RET_DK:COL_Q + (h + 1) * RET_DK], cos, sin)
    k = _rotate(z_ref[row, COL_K + h * RET_DK:COL_K + (h + 1) * RET_DK], cos, sin) * (RET_DK ** -0.5)
    v = z_ref[row, COL_V + h * RET_DV:COL_V + (h + 1) * RET_DV]
    new_state = math.exp(LOG_GAMMA[h]) * st_ref[0, h] + _row_to_col(k) * v
    new_st_ref[0, h] = new_state
    o = jnp.sum(_row_to_col(q) * new_state, axis=0, keepdims=True)
    gate = z_ref[row, COL_G + h * RET_DV:COL_G + (h + 1) * RET_DV]
    og_ref[row, h * RET_DV:(h + 1) * RET_DV] = gate * jax.nn.sigmoid(gate) * _rms(o)


def _proj_ret_kernel(*refs, blocks_per_seq, n_side):
    (x_ref, g_ref, w_ref, cos_ref, sin_ref, intra_ref, qd_ref, kd_ref,
     zs_ref, cos_s_ref, sin_s_ref, sst_ref) = refs[:12]
    side_in = refs[12:12 + n_side]
    og_ref, st_ref, ogs_ref, new_sst_ref, xn_ref = refs[12 + n_side:17 + n_side]
    side_out = refs[17 + n_side:17 + 2 * n_side]
    za_ref, zb_ref = refs[17 + 2 * n_side:]
    t = pl.program_id(0)
    j = pl.program_id(1)
    ret_block = jnp.maximum(t - 1, 0)
    sample_row = pl.ds(jnp.minimum(t * PR_CHUNKS + j, zs_ref.shape[0] - 1), 1)
    pos_rows = pl.ds(pl.multiple_of(((ret_block % blocks_per_seq) * PR_CHUNKS + j) * RET_CHUNK, RET_CHUNK),
                     RET_CHUNK)

    @pl.when(j == 0)
    def _():
        xn_ref[...] = (_rms(x_ref[...]) * g_ref[...]).astype(BF16)

    @pl.when(jnp.logical_and(t == 0, j == 0))
    def _():
        zb_ref[...] = jnp.zeros_like(zb_ref)

    @pl.when(jnp.logical_and(ret_block % blocks_per_seq == 0, j == 0))
    def _():
        st_ref[...] = jnp.zeros_like(st_ref)

    rows = pl.ds(pl.multiple_of(j * RET_CHUNK, RET_CHUNK), RET_CHUNK)

    def step(z_write, z_read):
        n_pieces = PR_TILE // MXU_COLS

        def before_head(h):
            for p in range(h * n_pieces // RET_HEADS, (h + 1) * n_pieces // RET_HEADS):
                cols = slice(p * MXU_COLS, (p + 1) * MXU_COLS)
                w_cols = pl.ds(pl.multiple_of(j * PR_TILE + p * MXU_COLS, MXU_COLS), MXU_COLS)
                z_write[j, :, cols] = _dot(xn_ref[...], w_ref[:, w_cols]).astype(BF16)
            _retention_step_head(h, sample_row, zs_ref, cos_s_ref[...], sin_s_ref[...], sst_ref, ogs_ref,
                                 new_sst_ref)

        _retention_chunk(z_read, rows, cos_ref[pos_rows, :], sin_ref[pos_rows, :], intra_ref, qd_ref, kd_ref,
                         st_ref, og_ref, before_head)

    @pl.when(t % 2 == 0)
    def _():
        step(za_ref, zb_ref)

    @pl.when(t % 2 == 1)
    def _():
        step(zb_ref, za_ref)

    _side_cast(side_in, side_out)


def _proj_ret(x, g, w_bf16, cos, sin, z_sample, cos_s, sin_s, state_sample, side_casts, *, batch, seq):
    n_blocks = batch * seq // PR_BLOCK
    blocks_per_seq = seq // PR_BLOCK
    dec_batch = z_sample.shape[0]
    assert dec_batch <= (n_blocks + 1) * PR_CHUNKS
    intra, q_decay, k_decay = _decay_tables()
    full = lambda a: pl.BlockSpec(a.shape, lambda t, j: (0,) * a.ndim, pipeline_mode=pl.Buffered(1))
    proj_block = lambda t: jnp.minimum(t, n_blocks - 1)
    ret_block = lambda t: jnp.maximum(t - 1, 0)
    og_block = lambda t: jnp.where(t == 0, n_blocks, t - 1)
    sample = lambda t, j: jnp.minimum(t * PR_CHUNKS + j, dec_batch - 1)
    state_spec = pl.BlockSpec((1, RET_HEADS, RET_DK, RET_DV), lambda t, j: (sample(t, j), 0, 0, 0))
    side_specs = _side_cast_specs(side_casts, (n_blocks + 1) * PR_CHUNKS, lambda t, j: t * PR_CHUNKS + j)
    return pl.pallas_call(
        functools.partial(_proj_ret_kernel, blocks_per_seq=blocks_per_seq, n_side=len(side_casts)),
        grid=(n_blocks + 1, PR_CHUNKS),
        in_specs=[
            pl.BlockSpec((PR_BLOCK, D_MODEL), lambda t, j: (proj_block(t), 0)),
            pl.BlockSpec((1, D_MODEL), lambda t, j: (0, 0)),
            pl.BlockSpec((D_MODEL, RET_COLS), lambda t, j: (0, 0), pipeline_mode=pl.Buffered(1)),
            full(cos), full(sin), full(intra), full(q_decay), full(k_decay),
            pl.BlockSpec((dec_batch, RET_COLS), lambda t, j: (0, 0), pipeline_mode=pl.Buffered(1)),
            full(cos_s), full(sin_s),
            state_spec,
        ] + side_specs,
        out_specs=[
            pl.BlockSpec((RET_CHUNK, RET_V), lambda t, j: (og_block(t) * PR_CHUNKS + j, 0)),
            pl.BlockSpec((1, RET_HEADS, RET_DK, RET_DV), lambda t, j: (ret_block(t) // blocks_per_seq, 0, 0, 0)),
            pl.BlockSpec((dec_batch, RET_V), lambda t, j: (0, 0)),
            state_spec,
            pl.BlockSpec((PR_BLOCK, D_MODEL), lambda t, j: (proj_block(t), 0)),
        ] + side_specs,
        out_shape=[
            jax.ShapeDtypeStruct(((n_blocks + 1) * PR_BLOCK, RET_V), BF16),
            jax.ShapeDtypeStruct((batch, RET_HEADS, RET_DK, RET_DV), F32),
            jax.ShapeDtypeStruct((dec_batch, RET_V), F32),
            jax.ShapeDtypeStruct(state_sample.shape, F32),
            jax.ShapeDtypeStruct((n_blocks * PR_BLOCK, D_MODEL), BF16),
        ] + [jax.ShapeDtypeStruct(a.shape, BF16) for a in side_casts],
        scratch_shapes=[
            pltpu.VMEM((PR_CHUNKS, PR_BLOCK, PR_TILE), BF16),
            pltpu.VMEM((PR_CHUNKS, PR_BLOCK, PR_TILE), BF16),
        ],
        compiler_params=_params("arbitrary", "arbitrary"),
        name="proj_ret",
    )(x, g.reshape(1, D_MODEL), w_bf16, cos, sin, intra, q_decay, k_decay,
      z_sample, cos_s, sin_s, state_sample, *side_casts)


def _gelu_tanh(x):
    c = math.sqrt(2.0 / math.pi)
    return x * (0.5 + 0.5 * jnp.tanh(x * (c + (c * 0.044715) * (x * x))))


def _layer_norm(x, g, b):
    xc = x - jnp.mean(x, axis=-1, keepdims=True)
    return xc * lax.rsqrt(jnp.mean(xc * xc, axis=-1, keepdims=True) + EPS) * g + b


def _merge(x_ref, ar_ref, ag_ref, og_ref, gm, wret_ref, wgm_ref, wo_ref, h_ref):
    branch_ret = _dot(og_ref[...].astype(BF16), wret_ref[...])
    branch_gm = _dot(gm, wgm_ref[...])
    m = (jax.nn.sigmoid(ar_ref[...].astype(F32)) * branch_ret
         + jax.nn.sigmoid(ag_ref[...].astype(F32)) * branch_gm)
    h_ref[...] = x_ref[...] + _dot(m.astype(BF16), wo_ref[...])


REST_COLS = D_IN - RET_COLS
REST_GU, REST_GV, REST_AR, REST_AG = 0, GM_WIDTH, 2 * GM_WIDTH, 2 * GM_WIDTH + D_MODEL


def _proj_mix_kernel(*refs, n_side):
    (x_ref, xn_ref, og_ref, wuv_ref, waa_ref, lng_ref, lnb_ref, ws_ref, bs_ref,
     wret_ref, wgm_ref, wo_ref) = refs[:12]
    side_in = refs[12:12 + n_side]
    h_ref = refs[12 + n_side]
    side_out = refs[13 + n_side:13 + 2 * n_side]
    zr_ref, gm_ref, br_ref, m_ref = refs[13 + 2 * n_side:]

    L = GM_CHUNK
    n_chunks = x_ref.shape[0] // L
    n_pieces = D_MODEL // MXU_COLS
    piece = lambda p, base=0: slice(base + p * MXU_COLS, base + (p + 1) * MXU_COLS)

    for p in range(n_pieces):
        zr_ref[:, piece(p, REST_GV)] = _dot(xn_ref[...], wuv_ref[:, piece(p, GM_WIDTH)])
    for p in range(n_pieces):
        zr_ref[:, piece(p, REST_GU)] = _dot(xn_ref[...], wuv_ref[:, piece(p)])

    causal = lax.broadcasted_iota(jnp.int32, (L, L), 0) >= lax.broadcasted_iota(jnp.int32, (L, L), 1)
    w_causal = [jnp.where(causal, ws_ref[g], 0.0).astype(BF16) for g in range(GM_GROUPS)]
    for c in range(n_chunks):
        for p in range(c * n_pieces // n_chunks, (c + 1) * n_pieces // n_chunks):
            zr_ref[:, piece(p, REST_AR)] = _dot(xn_ref[...], waa_ref[:, piece(p)])
            zr_ref[:, piece(p, REST_AG)] = _dot(xn_ref[...], waa_ref[:, piece(p, D_MODEL)])
            br_ref[:, piece(p)] = _dot(og_ref[...], wret_ref[:, piece(p)])
        rows = slice(c * L, (c + 1) * L)
        u = _gelu_tanh(zr_ref[rows, REST_GU:REST_GU + GM_WIDTH])
        v = _layer_norm(_gelu_tanh(zr_ref[rows, REST_GV:REST_GV + GM_WIDTH]), lng_ref[...],
                        lnb_ref[...]).astype(BF16)
        for g in range(GM_GROUPS):
            cols = slice(g * GM_CG, (g + 1) * GM_CG)
            mixed = _dot(w_causal[g], v[:, cols]) + bs_ref[:, g:g + 1]
            gm_ref[rows, cols] = (u[:, cols] * mixed).astype(BF16)
    for p in range(n_pieces):
        branch_gm = _dot(gm_ref[...], wgm_ref[:, piece(p)])
        m_ref[:, piece(p)] = (jax.nn.sigmoid(zr_ref[:, piece(p, REST_AR)]) * br_ref[:, piece(p)]
                              + jax.nn.sigmoid(zr_ref[:, piece(p, REST_AG)]) * branch_gm).astype(BF16)
    for p in range(n_pieces):
        h_ref[:, piece(p)] = x_ref[:, piece(p)] + _dot(m_ref[...], wo_ref[:, piece(p)])
    _side_cast(side_in, side_out)


def _proj_mix(x, xn, og, win, ln_g, ln_b, ws, bs, wret, wgm, wo, side_casts, *, tm):
    m = x.shape[0]
    half = REST_COLS // 2
    assert RET_COLS % half == 0
    full = lambda a: pl.BlockSpec(a.shape, lambda i: (0,) * a.ndim, pipeline_mode=pl.Buffered(1))
    wcols = lambda k: pl.BlockSpec((D_MODEL, half), lambda i: (0, RET_COLS // half + k),
                                   pipeline_mode=pl.Buffered(1))
    ln_g = ln_g.reshape(1, GM_WIDTH)
    ln_b = ln_b.reshape(1, GM_WIDTH)
    bs = bs.T
    side_specs = _side_cast_specs(side_casts, m // tm, lambda i: i)
    row_block = lambda width: pl.BlockSpec((tm, width), lambda i: (i, 0))
    return pl.pallas_call(
        functools.partial(_proj_mix_kernel, n_side=len(side_casts)),
        grid=(m // tm,),
        in_specs=[row_block(D_MODEL), row_block(D_MODEL), row_block(RET_V), wcols(0), wcols(1),
                  full(ln_g), full(ln_b), full(ws), full(bs), full(wret), full(wgm), full(wo)] + side_specs,
        out_specs=[row_block(D_MODEL)] + side_specs,
        out_shape=[jax.ShapeDtypeStruct((m, D_MODEL), F32)]
        + [jax.ShapeDtypeStruct(a.shape, BF16) for a in side_casts],
        scratch_shapes=[pltpu.VMEM((tm, REST_COLS), F32), pltpu.VMEM((tm, GM_WIDTH), BF16),
                        pltpu.VMEM((tm, D_MODEL), F32), pltpu.VMEM((tm, D_MODEL), BF16)],
        compiler_params=_params("parallel"),
        name="proj_mix",
    )(x, xn, og, win, win, ln_g, ln_b, ws, bs, wret, wgm, wo, *side_casts)


def _mix_step_kernel(x_ref, gu_ref, gv_ref, ar_ref, ag_ref, og_ref, lng_ref, lnb_ref, ws_ref, bs_ref,
                     wret_ref, wgm_ref, wo_ref, h_ref, v_ref):
    u = _gelu_tanh(gu_ref[...])
    v = _layer_norm(_gelu_tanh(gv_ref[...]), lng_ref[...], lnb_ref[...])
    v_ref[...] = v
    gm = (u * (ws_ref[...] * v + bs_ref[...])).astype(BF16)
    _merge(x_ref, ar_ref, ag_ref, og_ref, gm, wret_ref, wgm_ref, wo_ref, h_ref)


def _mix_step(x, z, og, ln_g, ln_b, ws, bs, wret, wgm, wo):
    m = x.shape[0]
    zcol = lambda off: pl.BlockSpec((m, D_MODEL), lambda i: (0, off // D_MODEL))
    full = lambda a: pl.BlockSpec(a.shape, lambda i: (0,) * a.ndim)
    ln_g = ln_g.reshape(1, GM_WIDTH)
    ln_b = ln_b.reshape(1, GM_WIDTH)
    ws = jnp.repeat(ws[:, 0, 0], GM_CG).reshape(1, GM_WIDTH)
    bs = jnp.repeat(bs[:, 0], GM_CG).reshape(1, GM_WIDTH)
    return pl.pallas_call(
        _mix_step_kernel,
        grid=(1,),
        in_specs=[full(x), zcol(COL_GU), zcol(COL_GV), zcol(COL_AR), zcol(COL_AG), full(og),
                  full(ln_g), full(ln_b), full(ws), full(bs), full(wret), full(wgm), full(wo)],
        out_specs=[pl.BlockSpec((m, D_MODEL), lambda i: (0, 0)), pl.BlockSpec((m, GM_WIDTH), lambda i: (0, 0))],
        out_shape=[jax.ShapeDtypeStruct((m, D_MODEL), F32), jax.ShapeDtypeStruct((m, GM_WIDTH), F32)],
        compiler_params=_params("arbitrary"),
        name="mix_step",
    )(x, z, z, z, z, og, ln_g, ln_b, ws, bs, wret, wgm, wo)


def _ffn_kernel(h_ref, g_ref, win_ref, wdown_ref, gf_ref, y_ref, *, final_norm):
    h = h_ref[...]
    hn = (_rms(h) * g_ref[...]).astype(BF16)
    f = _dot(hn, win_ref[...])
    f_gate = f[:, :D_FF]
    f_up = f[:, D_FF:]
    act = (f_gate * jax.nn.sigmoid(f_gate) * f_up).astype(BF16)
    out = h + _dot(act, wdown_ref[...])
    if final_norm:
        out = _rms(out) * gf_ref[...]
    y_ref[...] = out


def _ffn(h, g, win, wdown, g_final, *, tm, final_norm):
    m = h.shape[0]
    full = lambda a: pl.BlockSpec(a.shape, lambda i: (0,) * a.ndim)
    g = g.reshape(1, D_MODEL)
    g_final = g_final.reshape(1, D_MODEL)
    return pl.pallas_call(
        functools.partial(_ffn_kernel, final_norm=final_norm),
        grid=(m // tm,),
        in_specs=[pl.BlockSpec((tm, D_MODEL), lambda i: (i, 0)), full(g), full(win), full(wdown),
                  full(g_final)],
        out_specs=pl.BlockSpec((tm, D_MODEL), lambda i: (i, 0)),
        out_shape=jax.ShapeDtypeStruct((m, D_MODEL), F32),
        compiler_params=_params("parallel"),
        name="ffn",
    )(h, g, win, wdown, g_final)


def _rope_tables(pos):
    inv = ROPE_BASE ** (-jnp.arange(ROPE_HALF, dtype=F32) / ROPE_HALF)
    ang = pos[:, None] * inv[None, :]
    return jnp.cos(ang), jnp.sin(ang)


def kernel(x_prompt, x_sample, state_ret, norm_mix_g, w_in, w_ret_o, gm_ln_g, gm_ln_b, gm_ws, gm_bs,
           w_gm_o, w_o, norm_ffn_g, w_ffn_in, w_ffn_down, norm_final_g):
    batch, seq, _ = x_prompt.shape
    dec_batch, dec_seq, _ = x_sample.shape
    depth = w_in.shape[0]
    assert dec_seq == 1 and seq % PR_BLOCK == 0

    cos_p, sin_p = _rope_tables(jnp.arange(seq, dtype=F32))
    cos_s, sin_s = _rope_tables(PAST_LEN + jnp.arange(dec_seq, dtype=F32))

    hp = x_prompt.reshape(batch * seq, D_MODEL)
    hs = x_sample.reshape(dec_batch, D_MODEL)
    ret_p, ret_s, gmv_s = [], [], []
    for l in range(depth):
        last = l == depth - 1
        zs, win = _in_proj(hs, norm_mix_g[l], w_in[l], tn=1024)
        og, sp, ogs, ss, xn, wret, wgm, wo = _proj_ret(
            hp, norm_mix_g[l], win, cos_p, sin_p, zs, cos_s, sin_s, state_ret[l],
            (w_ret_o[l], w_gm_o[l], w_o[l]), batch=batch, seq=seq)

        h, wfin, wfdown = _proj_mix(hp, xn, og, win, gm_ln_g[l], gm_ln_b[l], gm_ws[l], gm_bs[l],
                                    wret, wgm, wo, (w_ffn_in[l], w_ffn_down[l]), tm=512)
        hp = _ffn(h, norm_ffn_g[l], wfin, wfdown, norm_final_g, tm=512, final_norm=last)
        ret_p.append(sp)

        h, vs = _mix_step(hs, zs, ogs, gm_ln_g[l], gm_ln_b[l], gm_ws[l], gm_bs[l], wret, wgm, wo)
        hs = _ffn(h, norm_ffn_g[l], wfin, wfdown, norm_final_g, tm=dec_batch, final_norm=last)
        ret_s.append(ss)
        gmv_s.append(vs.reshape(dec_batch, dec_seq, GM_WIDTH))

    return (hp.reshape(batch, seq, D_MODEL), hs.reshape(dec_batch, dec_seq, D_MODEL),
            jnp.stack(ret_p), jnp.stack(ret_s), jnp.stack(gmv_s))
```

```python
import functools
import math

import numpy as np
import jax
import jax.numpy as jnp
from jax import lax
from jax.experimental import pallas as pl
from jax.experimental.pallas import tpu as pltpu

D_MODEL = 1024
PAST_LEN = 16384
RET_DK = 256
RET_HEADS = D_MODEL // 256
RET_DV = 2 * RET_DK
RET_QK = RET_HEADS * RET_DK
RET_V = RET_HEADS * RET_DV
RET_CHUNK = 128
ROPE_BASE = 10000.0
ROPE_HALF = RET_DK // 2
GM_GROUPS = 4
GM_WIDTH = D_MODEL
GM_CG = GM_WIDTH // GM_GROUPS
GM_CHUNK = 128
D_FF = ((8 * D_MODEL // 3 + 255) // 256) * 256
EPS = 1e-6
D_IN = 2 * RET_QK + 2 * RET_V + 2 * GM_WIDTH + 2 * D_MODEL

COL_Q = 0
COL_K = RET_QK
COL_V = 2 * RET_QK
COL_G = 2 * RET_QK + RET_V
COL_GU = 2 * RET_QK + 2 * RET_V
COL_GV = COL_GU + GM_WIDTH
COL_AR = COL_GV + GM_WIDTH
COL_AG = COL_AR + D_MODEL

LOG_GAMMA = tuple(float(np.log1p(-np.exp2(np.float32(-5.0 - h)))) for h in range(RET_HEADS))

VMEM_LIMIT_BYTES = 56 * 1024 * 1024
MXU_COLS = 256

F32 = jnp.float32
BF16 = jnp.bfloat16


def _dot(a, b):
    return jnp.dot(a, b, preferred_element_type=F32)


def _rms(x):
    return x * lax.rsqrt(jnp.mean(x * x, axis=-1, keepdims=True) + EPS)


def _params(*semantics):
    return pltpu.CompilerParams(dimension_semantics=semantics, vmem_limit_bytes=VMEM_LIMIT_BYTES)


BF16_ROWS = 16


def _pieces_shape(rows, cols):
    return (cols // MXU_COLS, rows, MXU_COLS)


def _side_cast_specs(arrays, n_steps, step_of, *, pieces):
    in_specs, out_specs, out_shapes = [], [], []
    for a in arrays:
        rows, cols = a.shape
        n_blocks = max(n for n in range(1, n_steps + 1) if rows % n == 0 and (rows // n) % BF16_ROWS == 0)
        block = lambda *ids, n_blocks=n_blocks: step_of(*ids) * n_blocks // n_steps
        in_specs.append(pl.BlockSpec((rows // n_blocks, cols), lambda *ids, block=block: (block(*ids), 0)))
        if pieces:
            out_specs.append(pl.BlockSpec(_pieces_shape(rows // n_blocks, cols),
                                          lambda *ids, block=block: (0, block(*ids), 0)))
            out_shapes.append(jax.ShapeDtypeStruct(_pieces_shape(rows, cols), BF16))
        else:
            out_specs.append(in_specs[-1])
            out_shapes.append(jax.ShapeDtypeStruct(a.shape, BF16))
    return in_specs, out_specs, out_shapes


def _store_pieces(dst_ref, value):
    for p in range(dst_ref.shape[0]):
        dst_ref[p] = value[:, p * MXU_COLS:(p + 1) * MXU_COLS]


def _side_cast(side_in, side_out):
    for src, dst in zip(side_in, side_out):
        if len(dst.shape) == 3:
            _store_pieces(dst, src[...].astype(BF16))
        else:
            dst[...] = src[...].astype(BF16)


def _in_proj_kernel(x_ref, g_ref, w_ref, z_ref, wb_ref, xn_ref):
    @pl.when(pl.program_id(0) == 0)
    def _():
        xn_ref[...] = (_rms(x_ref[...]) * g_ref[...]).astype(BF16)

    w = w_ref[...].astype(BF16)
    _store_pieces(wb_ref, w)
    z_ref[...] = _dot(xn_ref[...], w)


def _in_proj(x, g, w, *, tn):
    m = x.shape[0]
    return pl.pallas_call(
        _in_proj_kernel,
        grid=(D_IN // tn,),
        in_specs=[
            pl.BlockSpec((m, D_MODEL), lambda j: (0, 0)),
            pl.BlockSpec((1, D_MODEL), lambda j: (0, 0)),
            pl.BlockSpec((D_MODEL, tn), lambda j: (0, j)),
        ],
        out_specs=[pl.BlockSpec((m, tn), lambda j: (0, j)),
                   pl.BlockSpec(_pieces_shape(D_MODEL, tn), lambda j: (j, 0, 0))],
        out_shape=[jax.ShapeDtypeStruct((m, D_IN), F32),
                   jax.ShapeDtypeStruct(_pieces_shape(D_MODEL, D_IN), BF16)],
        scratch_shapes=[pltpu.VMEM((m, D_MODEL), BF16)],
        compiler_params=_params("arbitrary"),
        name="in_proj",
    )(x, g.reshape(1, D_MODEL), w)


def _rotate(x, cos, sin):
    x1 = x[:, :ROPE_HALF]
    x2 = x[:, ROPE_HALF:]
    return jnp.concatenate([x1 * cos - x2 * sin, x1 * sin + x2 * cos], axis=-1)


RET_COLS = COL_GU
PR_BLOCK = 512
PR_CHUNKS = PR_BLOCK // RET_CHUNK
PR_TILE = RET_COLS // PR_CHUNKS
PR_TILE_PIECES = PR_TILE // MXU_COLS
RET_PIECES = RET_COLS // MXU_COLS


def _decay_tables():
    L = RET_CHUNK
    lg = np.array(LOG_GAMMA, np.float32)
    n = np.arange(L, dtype=np.float32)
    diff = n[:, None] - n[None, :]
    intra = np.where(diff[None] >= 0, np.exp(np.maximum(diff, 0.0)[None] * lg[:, None, None]), 0.0)
    q_decay = np.exp((n + 1.0)[None, :, None] * lg[:, None, None])
    k_decay = np.exp((L - 1.0 - n)[None, :, None] * lg[:, None, None])
    return (jnp.asarray(intra, F32), jnp.asarray(np.broadcast_to(q_decay, (RET_HEADS, L, RET_DV)), F32),
            jnp.asarray(np.broadcast_to(k_decay, (RET_HEADS, L, RET_DK)), F32))


def _retention_chunk(z_ref, rows, cos, sin, intra_ref, qd_ref, kd_ref, st_ref, og_ref, before_head):
    def zcols(start, width):
        first = start // MXU_COLS
        return jnp.concatenate([z_ref[first + i, rows, :] for i in range(width // MXU_COLS)], axis=-1)

    for h in range(RET_HEADS):
        before_head(h)
        q = _rotate(zcols(COL_Q + h * RET_DK, RET_DK).astype(F32), cos, sin)
        k = _rotate(zcols(COL_K + h * RET_DK, RET_DK).astype(F32), cos, sin) * (RET_DK ** -0.5)
        v = zcols(COL_V + h * RET_DV, RET_DV)
        qb = q.astype(BF16)
        scores = lax.dot_general(qb, k.astype(BF16), (((1,), (1,)), ((), ())),
                                 preferred_element_type=F32) * intra_ref[h]
        state = st_ref[0, h]
        o = _dot(scores.astype(BF16), v) + _dot(qb, state.astype(BF16)) * qd_ref[h]
        st_ref[0, h] = math.exp(RET_CHUNK * LOG_GAMMA[h]) * state + lax.dot_general(
            (k * kd_ref[h]).astype(BF16), v, (((0,), (0,)), ((), ())), preferred_element_type=F32)
        gate = zcols(COL_G + h * RET_DV, RET_DV).astype(F32)
        og_ref[:, h * RET_DV:(h + 1) * RET_DV] = (gate * jax.nn.sigmoid(gate) * _rms(o)).astype(og_ref.dtype)


def _row_to_col(row):
    n = row.shape[1]
    eye = lax.broadcasted_iota(jnp.int32, (n, n), 0) == lax.broadcasted_iota(jnp.int32, (n, n), 1)
    return jnp.sum(jnp.where(eye, row, 0.0), axis=1, keepdims=True)


def _retention_step_head(h, row, z_ref, cos, sin, st_ref, og_ref, new_st_ref):
    q = _rotate(z_ref[row, COL_Q + h * RET_DK:COL_Q + (h + 1) * RET_DK], cos, sin)
    k = _rotate(z_ref[row, COL_K + h * RET_DK:COL_K + (h + 1) * RET_DK], cos, sin) * (RET_DK ** -0.5)
    v = z_ref[row, COL_V + h * RET_DV:COL_V + (h + 1) * RET_DV]
    new_state = math.exp(LOG_GAMMA[h]) * st_ref[0, h] + _row_to_col(k) * v
    new_st_ref[0, h] = new_state
    o = jnp.sum(_row_to_col(q) * new_state, axis=0, keepdims=True)
    gate = z_ref[row, COL_G + h * RET_DV:COL_G + (h + 1) * RET_DV]
    og_ref[row, h * RET_DV:(h + 1) * RET_DV] = gate * jax.nn.sigmoid(gate) * _rms(o)


def _proj_ret_kernel(*refs, blocks_per_seq, n_side):
    (x_ref, g_ref, w_ref, cos_ref, sin_ref, intra_ref, qd_ref, kd_ref,
     zs_ref, cos_s_ref, sin_s_ref, sst_ref) = refs[:12]
    side_in = refs[12:12 + n_side]
    og_ref, st_ref, ogs_ref, new_sst_ref, xn_ref = refs[12 + n_side:17 + n_side]
    side_out = refs[17 + n_side:17 + 2 * n_side]
    za_ref, zb_ref = refs[17 + 2 * n_side:]
    t = pl.program_id(0)
    j = pl.program_id(1)
    ret_block = jnp.maximum(t - 1, 0)
    sample_row = pl.ds(jnp.minimum(t * PR_CHUNKS + j, zs_ref.shape[0] - 1), 1)
    pos_rows = pl.ds(pl.multiple_of(((ret_block % blocks_per_seq) * PR_CHUNKS + j) * RET_CHUNK, RET_CHUNK),
                     RET_CHUNK)

    @pl.when(j == 0)
    def _():
        xn_ref[...] = (_rms(x_ref[...]) * g_ref[...]).astype(BF16)

    @pl.when(jnp.logical_and(t == 0, j == 0))
    def _():
        zb_ref[...] = jnp.zeros_like(zb_ref)

    @pl.when(jnp.logical_and(ret_block % blocks_per_seq == 0, j == 0))
    def _():
        st_ref[...] = jnp.zeros_like(st_ref)

    rows = pl.ds(pl.multiple_of(j * RET_CHUNK, RET_CHUNK), RET_CHUNK)

    def step(z_write, z_read):
        def before_head(h):
            for p in range(h * PR_TILE_PIECES // RET_HEADS, (h + 1) * PR_TILE_PIECES // RET_HEADS):
                piece = j * PR_TILE_PIECES + p
                z_write[piece] = _dot(xn_ref[...], w_ref[piece]).astype(BF16)
            _retention_step_head(h, sample_row, zs_ref, cos_s_ref[...], sin_s_ref[...], sst_ref, ogs_ref,
                                 new_sst_ref)

        _retention_chunk(z_read, rows, cos_ref[pos_rows, :], sin_ref[pos_rows, :], intra_ref, qd_ref, kd_ref,
                         st_ref, og_ref, before_head)

    @pl.when(t % 2 == 0)
    def _():
        step(za_ref, zb_ref)

    @pl.when(t % 2 == 1)
    def _():
        step(zb_ref, za_ref)

    _side_cast(side_in, side_out)


def _proj_ret(x, g, w_pieces, cos, sin, z_sample, cos_s, sin_s, state_sample, side_casts, *, batch, seq):
    n_blocks = batch * seq // PR_BLOCK
    blocks_per_seq = seq // PR_BLOCK
    dec_batch = z_sample.shape[0]
    assert dec_batch <= (n_blocks + 1) * PR_CHUNKS
    intra, q_decay, k_decay = _decay_tables()
    full = lambda a: pl.BlockSpec(a.shape, lambda t, j: (0,) * a.ndim, pipeline_mode=pl.Buffered(1))
    proj_block = lambda t: jnp.minimum(t, n_blocks - 1)
    ret_block = lambda t: jnp.maximum(t - 1, 0)
    og_block = lambda t: jnp.where(t == 0, n_blocks, t - 1)
    sample = lambda t, j: jnp.minimum(t * PR_CHUNKS + j, dec_batch - 1)
    state_spec = pl.BlockSpec((1, RET_HEADS, RET_DK, RET_DV), lambda t, j: (sample(t, j), 0, 0, 0))
    side_in, side_out, side_shapes = _side_cast_specs(
        side_casts, (n_blocks + 1) * PR_CHUNKS, lambda t, j: t * PR_CHUNKS + j, pieces=True)
    return pl.pallas_call(
        functools.partial(_proj_ret_kernel, blocks_per_seq=blocks_per_seq, n_side=len(side_casts)),
        grid=(n_blocks + 1, PR_CHUNKS),
        in_specs=[
            pl.BlockSpec((PR_BLOCK, D_MODEL), lambda t, j: (proj_block(t), 0)),
            pl.BlockSpec((1, D_MODEL), lambda t, j: (0, 0)),
            pl.BlockSpec(_pieces_shape(D_MODEL, RET_COLS), lambda t, j: (0, 0, 0), pipeline_mode=pl.Buffered(1)),
            full(cos), full(sin), full(intra), full(q_decay), full(k_decay),
            pl.BlockSpec((dec_batch, RET_COLS), lambda t, j: (0, 0), pipeline_mode=pl.Buffered(1)),
            full(cos_s), full(sin_s),
            state_spec,
        ] + side_in,
        out_specs=[
            pl.BlockSpec((RET_CHUNK, RET_V), lambda t, j: (og_block(t) * PR_CHUNKS + j, 0)),
            pl.BlockSpec((1, RET_HEADS, RET_DK, RET_DV), lambda t, j: (ret_block(t) // blocks_per_seq, 0, 0, 0)),
            pl.BlockSpec((dec_batch, RET_V), lambda t, j: (0, 0)),
            state_spec,
            pl.BlockSpec((PR_BLOCK, D_MODEL), lambda t, j: (proj_block(t), 0)),
        ] + side_out,
        out_shape=[
            jax.ShapeDtypeStruct(((n_blocks + 1) * PR_BLOCK, RET_V), BF16),
            jax.ShapeDtypeStruct((batch, RET_HEADS, RET_DK, RET_DV), F32),
            jax.ShapeDtypeStruct((dec_batch, RET_V), F32),
            jax.ShapeDtypeStruct(state_sample.shape, F32),
            jax.ShapeDtypeStruct((n_blocks * PR_BLOCK, D_MODEL), BF16),
        ] + side_shapes,
        scratch_shapes=[
            pltpu.VMEM(_pieces_shape(PR_BLOCK, RET_COLS), BF16),
            pltpu.VMEM(_pieces_shape(PR_BLOCK, RET_COLS), BF16),
        ],
        compiler_params=_params("arbitrary", "arbitrary"),
        name="proj_ret",
    )(x, g.reshape(1, D_MODEL), w_pieces, cos, sin, intra, q_decay, k_decay,
      z_sample, cos_s, sin_s, state_sample, *side_casts)


def _gelu_tanh(x):
    c = math.sqrt(2.0 / math.pi)
    return x * (0.5 + 0.5 * jnp.tanh(x * (c + (c * 0.044715) * (x * x))))


def _layer_norm(x, g, b):
    xc = x - jnp.mean(x, axis=-1, keepdims=True)
    return xc * lax.rsqrt(jnp.mean(xc * xc, axis=-1, keepdims=True) + EPS) * g + b


def _dot_pieces(a, w_ref):
    return jnp.concatenate([_dot(a, w_ref[p]) for p in range(w_ref.shape[0])], axis=-1)


def _merge(x_ref, ar_ref, ag_ref, og_ref, gm, wret_ref, wgm_ref, wo_ref, h_ref):
    branch_ret = _dot_pieces(og_ref[...].astype(BF16), wret_ref)
    branch_gm = _dot_pieces(gm, wgm_ref)
    m = (jax.nn.sigmoid(ar_ref[...]) * branch_ret + jax.nn.sigmoid(ag_ref[...]) * branch_gm)
    h_ref[...] = x_ref[...] + _dot_pieces(m.astype(BF16), wo_ref)


REST_COLS = D_IN - RET_COLS
REST_GU, REST_GV, REST_AR, REST_AG = 0, GM_WIDTH, 2 * GM_WIDTH, 2 * GM_WIDTH + D_MODEL


def _proj_mix_kernel(*refs, n_side):
    (x_ref, xn_ref, og_ref, wuv_ref, waa_ref, lng_ref, lnb_ref, ws_ref, bs_ref,
     wret_ref, wgm_ref, wo_ref) = refs[:12]
    side_in = refs[12:12 + n_side]
    h_ref = refs[12 + n_side]
    side_out = refs[13 + n_side:13 + 2 * n_side]
    zr_ref, gm_ref, br_ref, m_ref = refs[13 + 2 * n_side:]

    L = GM_CHUNK
    n_chunks = x_ref.shape[0] // L
    n_pieces = D_MODEL // MXU_COLS
    piece = lambda p, base=0: slice(base + p * MXU_COLS, base + (p + 1) * MXU_COLS)

    for p in range(n_pieces):
        zr_ref[:, piece(p, REST_GV)] = _dot(xn_ref[...], wuv_ref[n_pieces + p])
    for p in range(n_pieces):
        zr_ref[:, piece(p, REST_GU)] = _dot(xn_ref[...], wuv_ref[p])

    causal = lax.broadcasted_iota(jnp.int32, (L, L), 0) >= lax.broadcasted_iota(jnp.int32, (L, L), 1)
    w_causal = [jnp.where(causal, ws_ref[g], 0.0).astype(BF16) for g in range(GM_GROUPS)]
    for c in range(n_chunks):
        for p in range(c * n_pieces // n_chunks, (c + 1) * n_pieces // n_chunks):
            zr_ref[:, piece(p, REST_AR)] = _dot(xn_ref[...], waa_ref[p])
            zr_ref[:, piece(p, REST_AG)] = _dot(xn_ref[...], waa_ref[n_pieces + p])
            br_ref[:, piece(p)] = _dot(og_ref[...], wret_ref[p])
        rows = slice(c * L, (c + 1) * L)
        u = _gelu_tanh(zr_ref[rows, REST_GU:REST_GU + GM_WIDTH])
        v = _layer_norm(_gelu_tanh(zr_ref[rows, REST_GV:REST_GV + GM_WIDTH]), lng_ref[...],
                        lnb_ref[...]).astype(BF16)
        for g in range(GM_GROUPS):
            cols = slice(g * GM_CG, (g + 1) * GM_CG)
            mixed = _dot(w_causal[g], v[:, cols]) + bs_ref[:, g:g + 1]
            gm_ref[rows, cols] = (u[:, cols] * mixed).astype(BF16)
    for p in range(n_pieces):
        branch_gm = _dot(gm_ref[...], wgm_ref[p])
        m_ref[:, piece(p)] = (jax.nn.sigmoid(zr_ref[:, piece(p, REST_AR)]) * br_ref[:, piece(p)]
                              + jax.nn.sigmoid(zr_ref[:, piece(p, REST_AG)]) * branch_gm).astype(BF16)
    for p in range(n_pieces):
        h_ref[:, piece(p)] = x_ref[:, piece(p)] + _dot(m_ref[...], wo_ref[p])
    _side_cast(side_in, side_out)


def _proj_mix(x, xn, og, w_pieces, ln_g, ln_b, ws, bs, wret, wgm, wo, side_casts, *, tm):
    m = x.shape[0]
    half = REST_COLS // 2
    assert RET_COLS % half == 0
    full = lambda a: pl.BlockSpec(a.shape, lambda i: (0,) * a.ndim, pipeline_mode=pl.Buffered(1))
    wcols = lambda k: pl.BlockSpec(_pieces_shape(D_MODEL, half), lambda i: (RET_COLS // half + k, 0, 0),
                                   pipeline_mode=pl.Buffered(1))
    ln_g = ln_g.reshape(1, GM_WIDTH)
    ln_b = ln_b.reshape(1, GM_WIDTH)
    bs = bs.T
    side_in, side_out, side_shapes = _side_cast_specs(side_casts, m // tm, lambda i: i, pieces=False)
    row_block = lambda width: pl.BlockSpec((tm, width), lambda i: (i, 0))
    return pl.pallas_call(
        functools.partial(_proj_mix_kernel, n_side=len(side_casts)),
        grid=(m // tm,),
        in_specs=[row_block(D_MODEL), row_block(D_MODEL), row_block(RET_V), wcols(0), wcols(1),
                  full(ln_g), full(ln_b), full(ws), full(bs), full(wret), full(wgm), full(wo)] + side_in,
        out_specs=[row_block(D_MODEL)] + side_out,
        out_shape=[jax.ShapeDtypeStruct((m, D_MODEL), F32)] + side_shapes,
        scratch_shapes=[pltpu.VMEM((tm, REST_COLS), F32), pltpu.VMEM((tm, GM_WIDTH), BF16),
                        pltpu.VMEM((tm, D_MODEL), F32), pltpu.VMEM((tm, D_MODEL), BF16)],
        compiler_params=_params("parallel"),
        name="proj_mix",
    )(x, xn, og, w_pieces, w_pieces, ln_g, ln_b, ws, bs, wret, wgm, wo, *side_casts)


def _mix_step_kernel(x_ref, gu_ref, gv_ref, ar_ref, ag_ref, og_ref, lng_ref, lnb_ref, ws_ref, bs_ref,
                     wret_ref, wgm_ref, wo_ref, h_ref, v_ref):
    u = _gelu_tanh(gu_ref[...])
    v = _layer_norm(_gelu_tanh(gv_ref[...]), lng_ref[...], lnb_ref[...])
    v_ref[...] = v
    gm = (u * (ws_ref[...] * v + bs_ref[...])).astype(BF16)
    _merge(x_ref, ar_ref, ag_ref, og_ref, gm, wret_ref, wgm_ref, wo_ref, h_ref)


def _mix_step(x, z, og, ln_g, ln_b, ws, bs, wret, wgm, wo):
    m = x.shape[0]
    zcol = lambda off: pl.BlockSpec((m, D_MODEL), lambda i: (0, off // D_MODEL))
    full = lambda a: pl.BlockSpec(a.shape, lambda i: (0,) * a.ndim)
    ln_g = ln_g.reshape(1, GM_WIDTH)
    ln_b = ln_b.reshape(1, GM_WIDTH)
    ws = jnp.repeat(ws[:, 0, 0], GM_CG).reshape(1, GM_WIDTH)
    bs = jnp.repeat(bs[:, 0], GM_CG).reshape(1, GM_WIDTH)
    return pl.pallas_call(
        _mix_step_kernel,
        grid=(1,),
        in_specs=[full(x), zcol(COL_GU), zcol(COL_GV), zcol(COL_AR), zcol(COL_AG), full(og),
                  full(ln_g), full(ln_b), full(ws), full(bs), full(wret), full(wgm), full(wo)],
        out_specs=[pl.BlockSpec((m, D_MODEL), lambda i: (0, 0)), pl.BlockSpec((m, GM_WIDTH), lambda i: (0, 0))],
        out_shape=[jax.ShapeDtypeStruct((m, D_MODEL), F32), jax.ShapeDtypeStruct((m, GM_WIDTH), F32)],
        compiler_params=_params("arbitrary"),
        name="mix_step",
    )(x, z, z, z, z, og, ln_g, ln_b, ws, bs, wret, wgm, wo)


def _ffn_kernel(h_ref, g_ref, win_ref, wdown_ref, gf_ref, y_ref, *, final_norm):
    h = h_ref[...]
    hn = (_rms(h) * g_ref[...]).astype(BF16)
    f = _dot(hn, win_ref[...])
    f_gate = f[:, :D_FF]
    f_up = f[:, D_FF:]
    act = (f_gate * jax.nn.sigmoid(f_gate) * f_up).astype(BF16)
    out = h + _dot(act, wdown_ref[...])
    if final_norm:
        out = _rms(out) * gf_ref[...]
    y_ref[...] = out


def _ffn(h, g, win, wdown, g_final, *, tm, final_norm):
    m = h.shape[0]
    full = lambda a: pl.BlockSpec(a.shape, lambda i: (0,) * a.ndim)
    g = g.reshape(1, D_MODEL)
    g_final = g_final.reshape(1, D_MODEL)
    return pl.pallas_call(
        functools.partial(_ffn_kernel, final_norm=final_norm),
        grid=(m // tm,),
        in_specs=[pl.BlockSpec((tm, D_MODEL), lambda i: (i, 0)), full(g), full(win), full(wdown),
                  full(g_final)],
        out_specs=pl.BlockSpec((tm, D_MODEL), lambda i: (i, 0)),
        out_shape=jax.ShapeDtypeStruct((m, D_MODEL), F32),
        compiler_params=_params("parallel"),
        name="ffn",
    )(h, g, win, wdown, g_final)


def _rope_tables(pos):
    inv = ROPE_BASE ** (-jnp.arange(ROPE_HALF, dtype=F32) / ROPE_HALF)
    ang = pos[:, None] * inv[None, :]
    return jnp.cos(ang), jnp.sin(ang)


def kernel(x_prompt, x_sample, state_ret, norm_mix_g, w_in, w_ret_o, gm_ln_g, gm_ln_b, gm_ws, gm_bs,
           w_gm_o, w_o, norm_ffn_g, w_ffn_in, w_ffn_down, norm_final_g):
    batch, seq, _ = x_prompt.shape
    dec_batch, dec_seq, _ = x_sample.shape
    depth = w_in.shape[0]
    assert dec_seq == 1 and seq % PR_BLOCK == 0

    cos_p, sin_p = _rope_tables(jnp.arange(seq, dtype=F32))
    cos_s, sin_s = _rope_tables(PAST_LEN + jnp.arange(dec_seq, dtype=F32))

    hp = x_prompt.reshape(batch * seq, D_MODEL)
    hs = x_sample.reshape(dec_batch, D_MODEL)
    ret_p, ret_s, gmv_s = [], [], []
    for l in range(depth):
        last = l == depth - 1
        zs, win = _in_proj(hs, norm_mix_g[l], w_in[l], tn=1024)
        og, sp, ogs, ss, xn, wret, wgm, wo = _proj_ret(
            hp, norm_mix_g[l], win, cos_p, sin_p, zs, cos_s, sin_s, state_ret[l],
            (w_ret_o[l], w_gm_o[l], w_o[l]), batch=batch, seq=seq)

        h, wfin, wfdown = _proj_mix(hp, xn, og, win, gm_ln_g[l], gm_ln_b[l], gm_ws[l], gm_bs[l],
                                    wret, wgm, wo, (w_ffn_in[l], w_ffn_down[l]), tm=512)
        hp = _ffn(h, norm_ffn_g[l], wfin, wfdown, norm_final_g, tm=512, final_norm=last)
        ret_p.append(sp)

        h, vs = _mix_step(hs, zs, ogs, gm_ln_g[l], gm_ln_b[l], gm_ws[l], gm_bs[l], wret, wgm, wo)
        hs = _ffn(h, norm_ffn_g[l], wfin, wfdown, norm_final_g, tm=dec_batch, final_norm=last)
        ret_s.append(ss)
        gmv_s.append(vs.reshape(dec_batch, dec_seq, GM_WIDTH))

    return (hp.reshape(batch, seq, D_MODEL), hs.reshape(dec_batch, dec_seq, D_MODEL),
            jnp.stack(ret_p), jnp.stack(ret_s), jnp.stack(gmv_s))
```

```python
import functools
import math

import numpy as np
import jax
import jax.numpy as jnp
from jax import lax
from jax.experimental import pallas as pl
from jax.experimental.pallas import tpu as pltpu

D_MODEL = 1024
PAST_LEN = 16384
RET_DK = 256
RET_HEADS = D_MODEL // 256
RET_DV = 2 * RET_DK
RET_QK = RET_HEADS * RET_DK
RET_V = RET_HEADS * RET_DV
RET_CHUNK = 128
ROPE_BASE = 10000.0
ROPE_HALF = RET_DK // 2
GM_GROUPS = 4
GM_WIDTH = D_MODEL
GM_CG = GM_WIDTH // GM_GROUPS
GM_CHUNK = 128
D_FF = ((8 * D_MODEL // 3 + 255) // 256) * 256
EPS = 1e-6
D_IN = 2 * RET_QK + 2 * RET_V + 2 * GM_WIDTH + 2 * D_MODEL

COL_Q = 0
COL_K = RET_QK
COL_V = 2 * RET_QK
COL_G = 2 * RET_QK + RET_V
COL_GU = 2 * RET_QK + 2 * RET_V
COL_GV = COL_GU + GM_WIDTH
COL_AR = COL_GV + GM_WIDTH
COL_AG = COL_AR + D_MODEL

LOG_GAMMA = tuple(float(np.log1p(-np.exp2(np.float32(-5.0 - h)))) for h in range(RET_HEADS))

VMEM_LIMIT_BYTES = 56 * 1024 * 1024
MXU_COLS = 256

F32 = jnp.float32
BF16 = jnp.bfloat16


def _dot(a, b):
    return jnp.dot(a, b, preferred_element_type=F32)


def _rms(x):
    return x * lax.rsqrt(jnp.mean(x * x, axis=-1, keepdims=True) + EPS)


def _params(*semantics):
    return pltpu.CompilerParams(dimension_semantics=semantics, vmem_limit_bytes=VMEM_LIMIT_BYTES)


BF16_ROWS = 16


def _pieces_shape(rows, cols):
    return (cols // MXU_COLS, rows, MXU_COLS)


def _side_cast_specs(arrays, n_steps, step_of, *, pieces):
    in_specs, out_specs, out_shapes = [], [], []
    for a in arrays:
        rows, cols = a.shape
        n_blocks = max(n for n in range(1, n_steps + 1) if rows % n == 0 and (rows // n) % BF16_ROWS == 0)
        block = lambda *ids, n_blocks=n_blocks: step_of(*ids) * n_blocks // n_steps
        in_specs.append(pl.BlockSpec((rows // n_blocks, cols), lambda *ids, block=block: (block(*ids), 0)))
        if pieces:
            out_specs.append(pl.BlockSpec(_pieces_shape(rows // n_blocks, cols),
                                          lambda *ids, block=block: (0, block(*ids), 0)))
            out_shapes.append(jax.ShapeDtypeStruct(_pieces_shape(rows, cols), BF16))
        else:
            out_specs.append(in_specs[-1])
            out_shapes.append(jax.ShapeDtypeStruct(a.shape, BF16))
    return in_specs, out_specs, out_shapes


def _store_pieces(dst_ref, value):
    for p in range(dst_ref.shape[0]):
        dst_ref[p] = value[:, p * MXU_COLS:(p + 1) * MXU_COLS]


def _side_cast(side_in, side_out):
    for src, dst in zip(side_in, side_out):
        if len(dst.shape) == 3:
            _store_pieces(dst, src[...].astype(BF16))
        else:
            dst[...] = src[...].astype(BF16)


def _in_proj_kernel(x_ref, g_ref, w_ref, z_ref, wb_ref, xn_ref):
    @pl.when(pl.program_id(0) == 0)
    def _():
        xn_ref[...] = (_rms(x_ref[...]) * g_ref[...]).astype(BF16)

    w = w_ref[...].astype(BF16)
    _store_pieces(wb_ref, w)
    z_ref[...] = _dot(xn_ref[...], w)


def _in_proj(x, g, w, *, tn):
    m = x.shape[0]
    return pl.pallas_call(
        _in_proj_kernel,
        grid=(D_IN // tn,),
        in_specs=[
            pl.BlockSpec((m, D_MODEL), lambda j: (0, 0)),
            pl.BlockSpec((1, D_MODEL), lambda j: (0, 0)),
            pl.BlockSpec((D_MODEL, tn), lambda j: (0, j)),
        ],
        out_specs=[pl.BlockSpec((m, tn), lambda j: (0, j)),
                   pl.BlockSpec(_pieces_shape(D_MODEL, tn), lambda j: (j, 0, 0))],
        out_shape=[jax.ShapeDtypeStruct((m, D_IN), F32),
                   jax.ShapeDtypeStruct(_pieces_shape(D_MODEL, D_IN), BF16)],
        scratch_shapes=[pltpu.VMEM((m, D_MODEL), BF16)],
        compiler_params=_params("arbitrary"),
        name="in_proj",
    )(x, g.reshape(1, D_MODEL), w)


def _rotate(x, cos, sin):
    x1 = x[:, :ROPE_HALF]
    x2 = x[:, ROPE_HALF:]
    return jnp.concatenate([x1 * cos - x2 * sin, x1 * sin + x2 * cos], axis=-1)


RET_COLS = COL_GU
PR_BLOCK = 512
PR_CHUNKS = PR_BLOCK // RET_CHUNK
PR_TILE = RET_COLS // PR_CHUNKS
PR_TILE_PIECES = PR_TILE // MXU_COLS
RET_PIECES = RET_COLS // MXU_COLS


def _decay_tables():
    L = RET_CHUNK
    lg = np.array(LOG_GAMMA, np.float32)
    n = np.arange(L, dtype=np.float32)
    diff = n[:, None] - n[None, :]
    intra = np.where(diff[None] >= 0, np.exp(np.maximum(diff, 0.0)[None] * lg[:, None, None]), 0.0)
    q_decay = np.exp((n + 1.0)[None, :, None] * lg[:, None, None])
    k_decay = np.exp((L - 1.0 - n)[None, :, None] * lg[:, None, None])
    return (jnp.asarray(intra, F32), jnp.asarray(np.broadcast_to(q_decay, (RET_HEADS, L, RET_DV)), F32),
            jnp.asarray(np.broadcast_to(k_decay, (RET_HEADS, L, RET_DK)), F32))


def _retention_chunk(z_ref, rows, cos, sin, intra_ref, qd_ref, kd_ref, st_ref, og_ref, before_head):
    def zcols(start, width):
        first = start // MXU_COLS
        return jnp.concatenate([z_ref[first + i, rows, :] for i in range(width // MXU_COLS)], axis=-1)

    qs, ks, vs, scores = [], [], [], []
    for h in range(RET_HEADS):
        before_head(h)
        q = _rotate(zcols(COL_Q + h * RET_DK, RET_DK).astype(F32), cos, sin)
        k = _rotate(zcols(COL_K + h * RET_DK, RET_DK).astype(F32), cos, sin) * (RET_DK ** -0.5)
        qs.append(q.astype(BF16))
        ks.append((k * kd_ref[h]).astype(BF16))
        vs.append(zcols(COL_V + h * RET_DV, RET_DV))
        scores.append((lax.dot_general(qs[h], k.astype(BF16), (((1,), (1,)), ((), ())),
                                       preferred_element_type=F32) * intra_ref[h]).astype(BF16))
    for h in range(RET_HEADS):
        state = st_ref[0, h]
        o = _dot(scores[h], vs[h]) + _dot(qs[h], state.astype(BF16)) * qd_ref[h]
        st_ref[0, h] = math.exp(RET_CHUNK * LOG_GAMMA[h]) * state + lax.dot_general(
            ks[h], vs[h], (((0,), (0,)), ((), ())), preferred_element_type=F32)
        gate = zcols(COL_G + h * RET_DV, RET_DV).astype(F32)
        og_ref[:, h * RET_DV:(h + 1) * RET_DV] = (gate * jax.nn.sigmoid(gate) * _rms(o)).astype(og_ref.dtype)


LANES = 128
SUBLANES = 8
STEP_ROWS = 32


def _row_to_col(row):
    n = row.shape[1]
    eye = lax.broadcasted_iota(jnp.int32, (LANES, LANES), 0) == lax.broadcasted_iota(jnp.int32, (LANES, LANES), 1)
    return jnp.concatenate(
        [jnp.sum(jnp.where(eye, row[:, b * LANES:(b + 1) * LANES], 0.0), axis=1, keepdims=True)
         for b in range(n // LANES)], axis=0)


def _retention_step_head(h, row, z_ref, cos, sin, st_ref, og_ref, new_st_ref):
    q = _rotate(z_ref[row, COL_Q + h * RET_DK:COL_Q + (h + 1) * RET_DK], cos, sin)
    k = _rotate(z_ref[row, COL_K + h * RET_DK:COL_K + (h + 1) * RET_DK], cos, sin) * (RET_DK ** -0.5)
    v = z_ref[row, COL_V + h * RET_DV:COL_V + (h + 1) * RET_DV]
    k_col = _row_to_col(k)
    q_col = _row_to_col(q)
    acc = None
    for r0 in range(0, RET_DK, STEP_ROWS):
        rows = slice(r0, r0 + STEP_ROWS)
        new_state = math.exp(LOG_GAMMA[h]) * st_ref[0, h, rows, :] + k_col[rows] * v
        new_st_ref[0, h, rows, :] = new_state
        part = (q_col[rows] * new_state).reshape(STEP_ROWS // SUBLANES, SUBLANES, RET_DV).sum(axis=0)
        acc = part if acc is None else acc + part
    o = jnp.sum(acc, axis=0, keepdims=True)
    gate = z_ref[row, COL_G + h * RET_DV:COL_G + (h + 1) * RET_DV]
    og_ref[row, h * RET_DV:(h + 1) * RET_DV] = gate * jax.nn.sigmoid(gate) * _rms(o)


def _proj_ret_kernel(*refs, blocks_per_seq, n_side):
    (x_ref, g_ref, w_ref, cos_ref, sin_ref, intra_ref, qd_ref, kd_ref,
     zs_ref, cos_s_ref, sin_s_ref, sst_ref) = refs[:12]
    side_in = refs[12:12 + n_side]
    og_ref, st_ref, ogs_ref, new_sst_ref, xn_ref = refs[12 + n_side:17 + n_side]
    side_out = refs[17 + n_side:17 + 2 * n_side]
    za_ref, zb_ref = refs[17 + 2 * n_side:]
    t = pl.program_id(0)
    j = pl.program_id(1)
    ret_block = jnp.maximum(t - 1, 0)
    sample_row = pl.ds(jnp.minimum(t * PR_CHUNKS + j, zs_ref.shape[0] - 1), 1)
    pos_rows = pl.ds(pl.multiple_of(((ret_block % blocks_per_seq) * PR_CHUNKS + j) * RET_CHUNK, RET_CHUNK),
                     RET_CHUNK)

    @pl.when(j == 0)
    def _():
        xn_ref[...] = (_rms(x_ref[...]) * g_ref[...]).astype(BF16)

    @pl.when(jnp.logical_and(t == 0, j == 0))
    def _():
        zb_ref[...] = jnp.zeros_like(zb_ref)

    @pl.when(jnp.logical_and(ret_block % blocks_per_seq == 0, j == 0))
    def _():
        st_ref[...] = jnp.zeros_like(st_ref)

    rows = pl.ds(pl.multiple_of(j * RET_CHUNK, RET_CHUNK), RET_CHUNK)

    def step(z_write, z_read):
        def before_head(h):
            for p in range(h * PR_TILE_PIECES // RET_HEADS, (h + 1) * PR_TILE_PIECES // RET_HEADS):
                piece = j * PR_TILE_PIECES + p
                z_write[piece] = _dot(xn_ref[...], w_ref[piece]).astype(BF16)
            _retention_step_head(h, sample_row, zs_ref, cos_s_ref[...], sin_s_ref[...], sst_ref, ogs_ref,
                                 new_sst_ref)

        _retention_chunk(z_read, rows, cos_ref[pos_rows, :], sin_ref[pos_rows, :], intra_ref, qd_ref, kd_ref,
                         st_ref, og_ref, before_head)

    @pl.when(t % 2 == 0)
    def _():
        step(za_ref, zb_ref)

    @pl.when(t % 2 == 1)
    def _():
        step(zb_ref, za_ref)

    _side_cast(side_in, side_out)


def _proj_ret(x, g, w_pieces, cos, sin, z_sample, cos_s, sin_s, state_sample, side_casts, *, batch, seq):
    n_blocks = batch * seq // PR_BLOCK
    blocks_per_seq = seq // PR_BLOCK
    dec_batch = z_sample.shape[0]
    assert dec_batch <= (n_blocks + 1) * PR_CHUNKS
    intra, q_decay, k_decay = _decay_tables()
    full = lambda a: pl.BlockSpec(a.shape, lambda t, j: (0,) * a.ndim, pipeline_mode=pl.Buffered(1))
    proj_block = lambda t: jnp.minimum(t, n_blocks - 1)
    ret_block = lambda t: jnp.maximum(t - 1, 0)
    og_block = lambda t: jnp.where(t == 0, n_blocks, t - 1)
    sample = lambda t, j: jnp.minimum(t * PR_CHUNKS + j, dec_batch - 1)
    state_spec = pl.BlockSpec((1, RET_HEADS, RET_DK, RET_DV), lambda t, j: (sample(t, j), 0, 0, 0))
    side_in, side_out, side_shapes = _side_cast_specs(
        side_casts, (n_blocks + 1) * PR_CHUNKS, lambda t, j: t * PR_CHUNKS + j, pieces=True)
    return pl.pallas_call(
        functools.partial(_proj_ret_kernel, blocks_per_seq=blocks_per_seq, n_side=len(side_casts)),
        grid=(n_blocks + 1, PR_CHUNKS),
        in_specs=[
            pl.BlockSpec((PR_BLOCK, D_MODEL), lambda t, j: (proj_block(t), 0)),
            pl.BlockSpec((1, D_MODEL), lambda t, j: (0, 0)),
            pl.BlockSpec(_pieces_shape(D_MODEL, RET_COLS), lambda t, j: (0, 0, 0), pipeline_mode=pl.Buffered(1)),
            full(cos), full(sin), full(intra), full(q_decay), full(k_decay),
            pl.BlockSpec((dec_batch, RET_COLS), lambda t, j: (0, 0), pipeline_mode=pl.Buffered(1)),
            full(cos_s), full(sin_s),
            state_spec,
        ] + side_in,
        out_specs=[
            pl.BlockSpec((RET_CHUNK, RET_V), lambda t, j: (og_block(t) * PR_CHUNKS + j, 0)),
            pl.BlockSpec((1, RET_HEADS, RET_DK, RET_DV), lambda t, j: (ret_block(t) // blocks_per_seq, 0, 0, 0)),
            pl.BlockSpec((dec_batch, RET_V), lambda t, j: (0, 0)),
            state_spec,
            pl.BlockSpec((PR_BLOCK, D_MODEL), lambda t, j: (proj_block(t), 0)),
        ] + side_out,
        out_shape=[
            jax.ShapeDtypeStruct(((n_blocks + 1) * PR_BLOCK, RET_V), BF16),
            jax.ShapeDtypeStruct((batch, RET_HEADS, RET_DK, RET_DV), F32),
            jax.ShapeDtypeStruct((dec_batch, RET_V), F32),
            jax.ShapeDtypeStruct(state_sample.shape, F32),
            jax.ShapeDtypeStruct((n_blocks * PR_BLOCK, D_MODEL), BF16),
        ] + side_shapes,
        scratch_shapes=[
            pltpu.VMEM(_pieces_shape(PR_BLOCK, RET_COLS), BF16),
            pltpu.VMEM(_pieces_shape(PR_BLOCK, RET_COLS), BF16),
        ],
        compiler_params=_params("arbitrary", "arbitrary"),
        name="proj_ret",
    )(x, g.reshape(1, D_MODEL), w_pieces, cos, sin, intra, q_decay, k_decay,
      z_sample, cos_s, sin_s, state_sample, *side_casts)


def _gelu_tanh(x):
    c = math.sqrt(2.0 / math.pi)
    return x * (0.5 + 0.5 * jnp.tanh(x * (c + (c * 0.044715) * (x * x))))


def _layer_norm(x, g, b):
    xc = x - jnp.mean(x, axis=-1, keepdims=True)
    return xc * lax.rsqrt(jnp.mean(xc * xc, axis=-1, keepdims=True) + EPS) * g + b


def _dot_pieces(a, w_ref):
    return jnp.concatenate([_dot(a, w_ref[p]) for p in range(w_ref.shape[0])], axis=-1)


def _merge(x_ref, ar_ref, ag_ref, og_ref, gm, wret_ref, wgm_ref, wo_ref, h_ref):
    branch_ret = _dot_pieces(og_ref[...].astype(BF16), wret_ref)
    branch_gm = _dot_pieces(gm, wgm_ref)
    m = (jax.nn.sigmoid(ar_ref[...]) * branch_ret + jax.nn.sigmoid(ag_ref[...]) * branch_gm)
    h_ref[...] = x_ref[...] + _dot_pieces(m.astype(BF16), wo_ref)


REST_COLS = D_IN - RET_COLS
REST_GU, REST_GV, REST_AR, REST_AG = 0, GM_WIDTH, 2 * GM_WIDTH, 2 * GM_WIDTH + D_MODEL


def _proj_mix_kernel(*refs, n_side):
    (x_ref, xn_ref, og_ref, wuv_ref, waa_ref, lng_ref, lnb_ref, ws_ref, bs_ref,
     wret_ref, wgm_ref, wo_ref) = refs[:12]
    side_in = refs[12:12 + n_side]
    h_ref = refs[12 + n_side]
    side_out = refs[13 + n_side:13 + 2 * n_side]
    zr_ref, gm_ref, br_ref, m_ref = refs[13 + 2 * n_side:]

    L = GM_CHUNK
    n_chunks = x_ref.shape[0] // L
    n_pieces = D_MODEL // MXU_COLS
    piece = lambda p, base=0: slice(base + p * MXU_COLS, base + (p + 1) * MXU_COLS)

    for p in range(n_pieces):
        zr_ref[:, piece(p, REST_GV)] = _dot(xn_ref[...], wuv_ref[n_pieces + p])
    for p in range(n_pieces):
        zr_ref[:, piece(p, REST_GU)] = _dot(xn_ref[...], wuv_ref[p])

    causal = lax.broadcasted_iota(jnp.int32, (L, L), 0) >= lax.broadcasted_iota(jnp.int32, (L, L), 1)
    w_causal = [jnp.where(causal, ws_ref[g], 0.0).astype(BF16) for g in range(GM_GROUPS)]
    for c in range(n_chunks):
        for p in range(c * n_pieces // n_chunks, (c + 1) * n_pieces // n_chunks):
            zr_ref[:, piece(p, REST_AR)] = _dot(xn_ref[...], waa_ref[p])
            zr_ref[:, piece(p, REST_AG)] = _dot(xn_ref[...], waa_ref[n_pieces + p])
            br_ref[:, piece(p)] = _dot(og_ref[...], wret_ref[p])
        rows = slice(c * L, (c + 1) * L)
        u = _gelu_tanh(zr_ref[rows, REST_GU:REST_GU + GM_WIDTH])
        v = _layer_norm(_gelu_tanh(zr_ref[rows, REST_GV:REST_GV + GM_WIDTH]), lng_ref[...],
                        lnb_ref[...]).astype(BF16)
        for g in range(GM_GROUPS):
            cols = slice(g * GM_CG, (g + 1) * GM_CG)
            mixed = _dot(w_causal[g], v[:, cols]) + bs_ref[:, g:g + 1]
            gm_ref[rows, cols] = (u[:, cols] * mixed).astype(BF16)
    for p in range(n_pieces):
        branch_gm = _dot(gm_ref[...], wgm_ref[p])
        m_ref[:, piece(p)] = (jax.nn.sigmoid(zr_ref[:, piece(p, REST_AR)]) * br_ref[:, piece(p)]
                              + jax.nn.sigmoid(zr_ref[:, piece(p, REST_AG)]) * branch_gm).astype(BF16)
    for p in range(n_pieces):
        h_ref[:, piece(p)] = x_ref[:, piece(p)] + _dot(m_ref[...], wo_ref[p])
    _side_cast(side_in, side_out)


def _proj_mix(x, xn, og, w_pieces, ln_g, ln_b, ws, bs, wret, wgm, wo, side_casts, *, tm):
    m = x.shape[0]
    half = REST_COLS // 2
    assert RET_COLS % half == 0
    full = lambda a: pl.BlockSpec(a.shape, lambda i: (0,) * a.ndim, pipeline_mode=pl.Buffered(1))
    wcols = lambda k: pl.BlockSpec(_pieces_shape(D_MODEL, half), lambda i: (RET_COLS // half + k, 0, 0),
                                   pipeline_mode=pl.Buffered(1))
    ln_g = ln_g.reshape(1, GM_WIDTH)
    ln_b = ln_b.reshape(1, GM_WIDTH)
    bs = bs.T
    side_in, side_out, side_shapes = _side_cast_specs(side_casts, m // tm, lambda i: i, pieces=False)
    row_block = lambda width: pl.BlockSpec((tm, width), lambda i: (i, 0))
    return pl.pallas_call(
        functools.partial(_proj_mix_kernel, n_side=len(side_casts)),
        grid=(m // tm,),
        in_specs=[row_block(D_MODEL), row_block(D_MODEL), row_block(RET_V), wcols(0), wcols(1),
                  full(ln_g), full(ln_b), full(ws), full(bs), full(wret), full(wgm), full(wo)] + side_in,
        out_specs=[row_block(D_MODEL)] + side_out,
        out_shape=[jax.ShapeDtypeStruct((m, D_MODEL), F32)] + side_shapes,
        scratch_shapes=[pltpu.VMEM((tm, REST_COLS), F32), pltpu.VMEM((tm, GM_WIDTH), BF16),
                        pltpu.VMEM((tm, D_MODEL), F32), pltpu.VMEM((tm, D_MODEL), BF16)],
        compiler_params=_params("parallel"),
        name="proj_mix",
    )(x, xn, og, w_pieces, w_pieces, ln_g, ln_b, ws, bs, wret, wgm, wo, *side_casts)


def _mix_step_kernel(x_ref, gu_ref, gv_ref, ar_ref, ag_ref, og_ref, lng_ref, lnb_ref, ws_ref, bs_ref,
                     wret_ref, wgm_ref, wo_ref, h_ref, v_ref):
    u = _gelu_tanh(gu_ref[...])
    v = _layer_norm(_gelu_tanh(gv_ref[...]), lng_ref[...], lnb_ref[...])
    v_ref[...] = v
    gm = (u * (ws_ref[...] * v + bs_ref[...])).astype(BF16)
    _merge(x_ref, ar_ref, ag_ref, og_ref, gm, wret_ref, wgm_ref, wo_ref, h_ref)


def _mix_step(x, z, og, ln_g, ln_b, ws, bs, wret, wgm, wo):
    m = x.shape[0]
    zcol = lambda off: pl.BlockSpec((m, D_MODEL), lambda i: (0, off // D_MODEL))
    full = lambda a: pl.BlockSpec(a.shape, lambda i: (0,) * a.ndim)
    ln_g = ln_g.reshape(1, GM_WIDTH)
    ln_b = ln_b.reshape(1, GM_WIDTH)
    ws = jnp.repeat(ws[:, 0, 0], GM_CG).reshape(1, GM_WIDTH)
    bs = jnp.repeat(bs[:, 0], GM_CG).reshape(1, GM_WIDTH)
    return pl.pallas_call(
        _mix_step_kernel,
        grid=(1,),
        in_specs=[full(x), zcol(COL_GU), zcol(COL_GV), zcol(COL_AR), zcol(COL_AG), full(og),
                  full(ln_g), full(ln_b), full(ws), full(bs), full(wret), full(wgm), full(wo)],
        out_specs=[pl.BlockSpec((m, D_MODEL), lambda i: (0, 0)), pl.BlockSpec((m, GM_WIDTH), lambda i: (0, 0))],
        out_shape=[jax.ShapeDtypeStruct((m, D_MODEL), F32), jax.ShapeDtypeStruct((m, GM_WIDTH), F32)],
        compiler_params=_params("arbitrary"),
        name="mix_step",
    )(x, z, z, z, z, og, ln_g, ln_b, ws, bs, wret, wgm, wo)


def _ffn_kernel(h_ref, g_ref, win_ref, wdown_ref, gf_ref, y_ref, *, final_norm):
    h = h_ref[...]
    hn = (_rms(h) * g_ref[...]).astype(BF16)
    f = _dot(hn, win_ref[...])
    f_gate = f[:, :D_FF]
    f_up = f[:, D_FF:]
    act = (f_gate * jax.nn.sigmoid(f_gate) * f_up).astype(BF16)
    out = h + _dot(act, wdown_ref[...])
    if final_norm:
        out = _rms(out) * gf_ref[...]
    y_ref[...] = out


def _ffn(h, g, win, wdown, g_final, *, tm, final_norm):
    m = h.shape[0]
    full = lambda a: pl.BlockSpec(a.shape, lambda i: (0,) * a.ndim)
    g = g.reshape(1, D_MODEL)
    g_final = g_final.reshape(1, D_MODEL)
    return pl.pallas_call(
        functools.partial(_ffn_kernel, final_norm=final_norm),
        grid=(m // tm,),
        in_specs=[pl.BlockSpec((tm, D_MODEL), lambda i: (i, 0)), full(g), full(win), full(wdown),
                  full(g_final)],
        out_specs=pl.BlockSpec((tm, D_MODEL), lambda i: (i, 0)),
        out_shape=jax.ShapeDtypeStruct((m, D_MODEL), F32),
        compiler_params=_params("parallel"),
        name="ffn",
    )(h, g, win, wdown, g_final)


def _rope_tables(pos):
    inv = ROPE_BASE ** (-jnp.arange(ROPE_HALF, dtype=F32) / ROPE_HALF)
    ang = pos[:, None] * inv[None, :]
    return jnp.cos(ang), jnp.sin(ang)


def kernel(x_prompt, x_sample, state_ret, norm_mix_g, w_in, w_ret_o, gm_ln_g, gm_ln_b, gm_ws, gm_bs,
           w_gm_o, w_o, norm_ffn_g, w_ffn_in, w_ffn_down, norm_final_g):
    batch, seq, _ = x_prompt.shape
    dec_batch, dec_seq, _ = x_sample.shape
    depth = w_in.shape[0]
    assert dec_seq == 1 and seq % PR_BLOCK == 0

    cos_p, sin_p = _rope_tables(jnp.arange(seq, dtype=F32))
    cos_s, sin_s = _rope_tables(PAST_LEN + jnp.arange(dec_seq, dtype=F32))

    hp = x_prompt.reshape(batch * seq, D_MODEL)
    hs = x_sample.reshape(dec_batch, D_MODEL)
    ret_p, ret_s, gmv_s = [], [], []
    for l in range(depth):
        last = l == depth - 1
        zs, win = _in_proj(hs, norm_mix_g[l], w_in[l], tn=1024)
        og, sp, ogs, ss, xn, wret, wgm, wo = _proj_ret(
            hp, norm_mix_g[l], win, cos_p, sin_p, zs, cos_s, sin_s, state_ret[l],
            (w_ret_o[l], w_gm_o[l], w_o[l]), batch=batch, seq=seq)

        h, wfin, wfdown = _proj_mix(hp, xn, og, win, gm_ln_g[l], gm_ln_b[l], gm_ws[l], gm_bs[l],
                                    wret, wgm, wo, (w_ffn_in[l], w_ffn_down[l]), tm=512)
        hp = _ffn(h, norm_ffn_g[l], wfin, wfdown, norm_final_g, tm=512, final_norm=last)
        ret_p.append(sp)

        h, vs = _mix_step(hs, zs, ogs, gm_ln_g[l], gm_ln_b[l], gm_ws[l], gm_bs[l], wret, wgm, wo)
        hs = _ffn(h, norm_ffn_g[l], wfin, wfdown, norm_final_g, tm=dec_batch, final_norm=last)
        ret_s.append(ss)
        gmv_s.append(vs.reshape(dec_batch, dec_seq, GM_WIDTH))

    return (hp.reshape(batch, seq, D_MODEL), hs.reshape(dec_batch, dec_seq, D_MODEL),
            jnp.stack(ret_p), jnp.stack(ret_s), jnp.stack(gmv_s))
```

```python
import functools
import math

import numpy as np
import jax
import jax.numpy as jnp
from jax import lax
from jax.experimental import pallas as pl
from jax.experimental.pallas import tpu as pltpu

D_MODEL = 1024
PAST_LEN = 16384
RET_DK = 256
RET_HEADS = D_MODEL // 256
RET_DV = 2 * RET_DK
RET_QK = RET_HEADS * RET_DK
RET_V = RET_HEADS * RET_DV
RET_CHUNK = 128
ROPE_BASE = 10000.0
ROPE_HALF = RET_DK // 2
GM_GROUPS = 4
GM_WIDTH = D_MODEL
GM_CG = GM_WIDTH // GM_GROUPS
GM_CHUNK = 128
D_FF = ((8 * D_MODEL // 3 + 255) // 256) * 256
EPS = 1e-6
D_IN = 2 * RET_QK + 2 * RET_V + 2 * GM_WIDTH + 2 * D_MODEL

COL_Q = 0
COL_K = RET_QK
COL_V = 2 * RET_QK
COL_G = 2 * RET_QK + RET_V
COL_GU = 2 * RET_QK + 2 * RET_V
COL_GV = COL_GU + GM_WIDTH
COL_AR = COL_GV + GM_WIDTH
COL_AG = COL_AR + D_MODEL

LOG_GAMMA = tuple(float(np.log1p(-np.exp2(np.float32(-5.0 - h)))) for h in range(RET_HEADS))

VMEM_LIMIT_BYTES = 56 * 1024 * 1024
MXU_COLS = 256

F32 = jnp.float32
BF16 = jnp.bfloat16


def _dot(a, b):
    return jnp.dot(a, b, preferred_element_type=F32)


def _rms(x):
    return x * lax.rsqrt(jnp.mean(x * x, axis=-1, keepdims=True) + EPS)


def _params(*semantics):
    return pltpu.CompilerParams(dimension_semantics=semantics, vmem_limit_bytes=VMEM_LIMIT_BYTES)


BF16_ROWS = 16


def _pieces_shape(rows, cols):
    return (cols // MXU_COLS, rows, MXU_COLS)


def _side_cast_specs(arrays, n_steps, step_of, *, pieces):
    in_specs, out_specs, out_shapes = [], [], []
    for a in arrays:
        rows, cols = a.shape
        n_blocks = max(n for n in range(1, n_steps + 1) if rows % n == 0 and (rows // n) % BF16_ROWS == 0)
        block = lambda *ids, n_blocks=n_blocks: step_of(*ids) * n_blocks // n_steps
        in_specs.append(pl.BlockSpec((rows // n_blocks, cols), lambda *ids, block=block: (block(*ids), 0)))
        if pieces:
            out_specs.append(pl.BlockSpec(_pieces_shape(rows // n_blocks, cols),
                                          lambda *ids, block=block: (0, block(*ids), 0)))
            out_shapes.append(jax.ShapeDtypeStruct(_pieces_shape(rows, cols), BF16))
        else:
            out_specs.append(in_specs[-1])
            out_shapes.append(jax.ShapeDtypeStruct(a.shape, BF16))
    return in_specs, out_specs, out_shapes


def _store_pieces(dst_ref, value):
    for p in range(dst_ref.shape[0]):
        dst_ref[p] = value[:, p * MXU_COLS:(p + 1) * MXU_COLS]


def _side_cast(side_in, side_out):
    for src, dst in zip(side_in, side_out):
        if len(dst.shape) == 3:
            _store_pieces(dst, src[...].astype(BF16))
        else:
            dst[...] = src[...].astype(BF16)


def _in_proj_kernel(x_ref, g_ref, w_ref, z_ref, wb_ref, xn_ref):
    @pl.when(pl.program_id(0) == 0)
    def _():
        xn_ref[...] = (_rms(x_ref[...]) * g_ref[...]).astype(BF16)

    w = w_ref[...].astype(BF16)
    _store_pieces(wb_ref, w)
    z_ref[...] = _dot(xn_ref[...], w)


def _in_proj(x, g, w, *, tn):
    m = x.shape[0]
    return pl.pallas_call(
        _in_proj_kernel,
        grid=(D_IN // tn,),
        in_specs=[
            pl.BlockSpec((m, D_MODEL), lambda j: (0, 0)),
            pl.BlockSpec((1, D_MODEL), lambda j: (0, 0)),
            pl.BlockSpec((D_MODEL, tn), lambda j: (0, j)),
        ],
        out_specs=[pl.BlockSpec((m, tn), lambda j: (0, j)),
                   pl.BlockSpec(_pieces_shape(D_MODEL, tn), lambda j: (j, 0, 0))],
        out_shape=[jax.ShapeDtypeStruct((m, D_IN), F32),
                   jax.ShapeDtypeStruct(_pieces_shape(D_MODEL, D_IN), BF16)],
        scratch_shapes=[pltpu.VMEM((m, D_MODEL), BF16)],
        compiler_params=_params("arbitrary"),
        name="in_proj",
    )(x, g.reshape(1, D_MODEL), w)


def _rotate(x, cos, sin):
    x1 = x[:, :ROPE_HALF]
    x2 = x[:, ROPE_HALF:]
    return jnp.concatenate([x1 * cos - x2 * sin, x1 * sin + x2 * cos], axis=-1)


RET_COLS = COL_GU
PR_BLOCK = 512
PR_CHUNKS = PR_BLOCK // RET_CHUNK
PR_TILE = RET_COLS // PR_CHUNKS
PR_TILE_PIECES = PR_TILE // MXU_COLS
RET_PIECES = RET_COLS // MXU_COLS


def _decay_tables():
    L = RET_CHUNK
    lg = np.array(LOG_GAMMA, np.float32)
    n = np.arange(L, dtype=np.float32)
    diff = n[:, None] - n[None, :]
    intra = np.where(diff[None] >= 0, np.exp(np.maximum(diff, 0.0)[None] * lg[:, None, None]), 0.0)
    q_decay = np.exp((n + 1.0)[None, :, None] * lg[:, None, None])
    k_decay = np.exp((L - 1.0 - n)[None, :, None] * lg[:, None, None])
    return (jnp.asarray(intra, F32), jnp.asarray(np.broadcast_to(q_decay, (RET_HEADS, L, RET_DV)), F32),
            jnp.asarray(np.broadcast_to(k_decay, (RET_HEADS, L, RET_DK)), F32))


def _retention_chunk(z_ref, rows, cos, sin, intra_ref, qd_ref, kd_ref, st_ref, og_ref, interleave):
    def zcols(start, width):
        first = start // MXU_COLS
        return jnp.concatenate([z_ref[first + i, rows, :] for i in range(width // MXU_COLS)], axis=-1)

    qs, ks, vs, scores = [], [], [], []
    for h in range(RET_HEADS):
        if h % 2 == 0:
            interleave(h // 2)
        q = _rotate(zcols(COL_Q + h * RET_DK, RET_DK).astype(F32), cos, sin)
        k = _rotate(zcols(COL_K + h * RET_DK, RET_DK).astype(F32), cos, sin) * (RET_DK ** -0.5)
        qs.append(q.astype(BF16))
        ks.append((k * kd_ref[h]).astype(BF16))
        vs.append(zcols(COL_V + h * RET_DV, RET_DV))
        scores.append((lax.dot_general(qs[h], k.astype(BF16), (((1,), (1,)), ((), ())),
                                       preferred_element_type=F32) * intra_ref[h]).astype(BF16))
    for h in range(RET_HEADS):
        if h % 2 == 0:
            interleave(RET_HEADS // 2 + h // 2)
        state = st_ref[0, h]
        o = _dot(scores[h], vs[h]) + _dot(qs[h], state.astype(BF16)) * qd_ref[h]
        st_ref[0, h] = math.exp(RET_CHUNK * LOG_GAMMA[h]) * state + lax.dot_general(
            ks[h], vs[h], (((0,), (0,)), ((), ())), preferred_element_type=F32)
        gate = zcols(COL_G + h * RET_DV, RET_DV).astype(F32)
        og_ref[:, h * RET_DV:(h + 1) * RET_DV] = (gate * jax.nn.sigmoid(gate) * _rms(o)).astype(og_ref.dtype)


LANES = 128
SUBLANES = 8
STEP_ROWS = 32


def _row_to_col(row):
    n = row.shape[1]
    eye = lax.broadcasted_iota(jnp.int32, (LANES, LANES), 0) == lax.broadcasted_iota(jnp.int32, (LANES, LANES), 1)
    return jnp.concatenate(
        [jnp.sum(jnp.where(eye, row[:, b * LANES:(b + 1) * LANES], 0.0), axis=1, keepdims=True)
         for b in range(n // LANES)], axis=0)


def _retention_step_head(h, row, z_ref, cos, sin, st_ref, og_ref, new_st_ref):
    q = _rotate(z_ref[row, COL_Q + h * RET_DK:COL_Q + (h + 1) * RET_DK], cos, sin)
    k = _rotate(z_ref[row, COL_K + h * RET_DK:COL_K + (h + 1) * RET_DK], cos, sin) * (RET_DK ** -0.5)
    v = z_ref[row, COL_V + h * RET_DV:COL_V + (h + 1) * RET_DV]
    k_col = _row_to_col(k)
    q_col = _row_to_col(q)
    acc = None
    for r0 in range(0, RET_DK, STEP_ROWS):
        rows = slice(r0, r0 + STEP_ROWS)
        new_state = math.exp(LOG_GAMMA[h]) * st_ref[0, h, rows, :] + k_col[rows] * v
        new_st_ref[0, h, rows, :] = new_state
        part = (q_col[rows] * new_state).reshape(STEP_ROWS // SUBLANES, SUBLANES, RET_DV).sum(axis=0)
        acc = part if acc is None else acc + part
    o = jnp.sum(acc, axis=0, keepdims=True)
    gate = z_ref[row, COL_G + h * RET_DV:COL_G + (h + 1) * RET_DV]
    og_ref[row, h * RET_DV:(h + 1) * RET_DV] = gate * jax.nn.sigmoid(gate) * _rms(o)


def _proj_ret_kernel(*refs, blocks_per_seq, n_side):
    (x_ref, g_ref, w_ref, cos_ref, sin_ref, intra_ref, qd_ref, kd_ref,
     zs_ref, cos_s_ref, sin_s_ref, sst_ref) = refs[:12]
    side_in = refs[12:12 + n_side]
    og_ref, st_ref, ogs_ref, new_sst_ref, xn_ref = refs[12 + n_side:17 + n_side]
    side_out = refs[17 + n_side:17 + 2 * n_side]
    za_ref, zb_ref = refs[17 + 2 * n_side:]
    t = pl.program_id(0)
    j = pl.program_id(1)
    ret_block = jnp.maximum(t - 1, 0)
    sample_row = pl.ds(jnp.minimum(t * PR_CHUNKS + j, zs_ref.shape[0] - 1), 1)
    pos_rows = pl.ds(pl.multiple_of(((ret_block % blocks_per_seq) * PR_CHUNKS + j) * RET_CHUNK, RET_CHUNK),
                     RET_CHUNK)

    @pl.when(j == 0)
    def _():
        xn_ref[...] = (_rms(x_ref[...]) * g_ref[...]).astype(BF16)

    @pl.when(jnp.logical_and(t == 0, j == 0))
    def _():
        zb_ref[...] = jnp.zeros_like(zb_ref)

    @pl.when(jnp.logical_and(ret_block % blocks_per_seq == 0, j == 0))
    def _():
        st_ref[...] = jnp.zeros_like(st_ref)

    rows = pl.ds(pl.multiple_of(j * RET_CHUNK, RET_CHUNK), RET_CHUNK)

    def step(z_write, z_read):
        def interleave(slot):
            for p in range(slot * PR_TILE_PIECES // RET_HEADS, (slot + 1) * PR_TILE_PIECES // RET_HEADS):
                piece = j * PR_TILE_PIECES + p
                z_write[piece] = _dot(xn_ref[...], w_ref[piece]).astype(BF16)
            _retention_step_head(slot, sample_row, zs_ref, cos_s_ref[...], sin_s_ref[...], sst_ref, ogs_ref,
                                 new_sst_ref)

        _retention_chunk(z_read, rows, cos_ref[pos_rows, :], sin_ref[pos_rows, :], intra_ref, qd_ref, kd_ref,
                         st_ref, og_ref, interleave)

    @pl.when(t % 2 == 0)
    def _():
        step(za_ref, zb_ref)

    @pl.when(t % 2 == 1)
    def _():
        step(zb_ref, za_ref)

    _side_cast(side_in, side_out)


def _proj_ret(x, g, w_pieces, cos, sin, z_sample, cos_s, sin_s, state_sample, side_casts, *, batch, seq):
    n_blocks = batch * seq // PR_BLOCK
    blocks_per_seq = seq // PR_BLOCK
    dec_batch = z_sample.shape[0]
    assert dec_batch <= (n_blocks + 1) * PR_CHUNKS
    intra, q_decay, k_decay = _decay_tables()
    full = lambda a: pl.BlockSpec(a.shape, lambda t, j: (0,) * a.ndim, pipeline_mode=pl.Buffered(1))
    proj_block = lambda t: jnp.minimum(t, n_blocks - 1)
    ret_block = lambda t: jnp.maximum(t - 1, 0)
    og_block = lambda t: jnp.where(t == 0, n_blocks, t - 1)
    sample = lambda t, j: jnp.minimum(t * PR_CHUNKS + j, dec_batch - 1)
    state_spec = pl.BlockSpec((1, RET_HEADS, RET_DK, RET_DV), lambda t, j: (sample(t, j), 0, 0, 0))
    side_in, side_out, side_shapes = _side_cast_specs(
        side_casts, (n_blocks + 1) * PR_CHUNKS, lambda t, j: t * PR_CHUNKS + j, pieces=True)
    return pl.pallas_call(
        functools.partial(_proj_ret_kernel, blocks_per_seq=blocks_per_seq, n_side=len(side_casts)),
        grid=(n_blocks + 1, PR_CHUNKS),
        in_specs=[
            pl.BlockSpec((PR_BLOCK, D_MODEL), lambda t, j: (proj_block(t), 0)),
            pl.BlockSpec((1, D_MODEL), lambda t, j: (0, 0)),
            pl.BlockSpec(_pieces_shape(D_MODEL, RET_COLS), lambda t, j: (0, 0, 0), pipeline_mode=pl.Buffered(1)),
            full(cos), full(sin), full(intra), full(q_decay), full(k_decay),
            pl.BlockSpec((dec_batch, RET_COLS), lambda t, j: (0, 0), pipeline_mode=pl.Buffered(1)),
            full(cos_s), full(sin_s),
            state_spec,
        ] + side_in,
        out_specs=[
            pl.BlockSpec((RET_CHUNK, RET_V), lambda t, j: (og_block(t) * PR_CHUNKS + j, 0)),
            pl.BlockSpec((1, RET_HEADS, RET_DK, RET_DV), lambda t, j: (ret_block(t) // blocks_per_seq, 0, 0, 0)),
            pl.BlockSpec((dec_batch, RET_V), lambda t, j: (0, 0)),
            state_spec,
            pl.BlockSpec((PR_BLOCK, D_MODEL), lambda t, j: (proj_block(t), 0)),
        ] + side_out,
        out_shape=[
            jax.ShapeDtypeStruct(((n_blocks + 1) * PR_BLOCK, RET_V), BF16),
            jax.ShapeDtypeStruct((batch, RET_HEADS, RET_DK, RET_DV), F32),
            jax.ShapeDtypeStruct((dec_batch, RET_V), F32),
            jax.ShapeDtypeStruct(state_sample.shape, F32),
            jax.ShapeDtypeStruct((n_blocks * PR_BLOCK, D_MODEL), BF16),
        ] + side_shapes,
        scratch_shapes=[
            pltpu.VMEM(_pieces_shape(PR_BLOCK, RET_COLS), BF16),
            pltpu.VMEM(_pieces_shape(PR_BLOCK, RET_COLS), BF16),
        ],
        compiler_params=_params("arbitrary", "arbitrary"),
        name="proj_ret",
    )(x, g.reshape(1, D_MODEL), w_pieces, cos, sin, intra, q_decay, k_decay,
      z_sample, cos_s, sin_s, state_sample, *side_casts)


def _gelu_tanh(x):
    c = math.sqrt(2.0 / math.pi)
    return x * (0.5 + 0.5 * jnp.tanh(x * (c + (c * 0.044715) * (x * x))))


def _layer_norm(x, g, b):
    xc = x - jnp.mean(x, axis=-1, keepdims=True)
    return xc * lax.rsqrt(jnp.mean(xc * xc, axis=-1, keepdims=True) + EPS) * g + b


def _dot_pieces(a, w_ref):
    return jnp.concatenate([_dot(a, w_ref[p]) for p in range(w_ref.shape[0])], axis=-1)


def _merge(x_ref, ar_ref, ag_ref, og_ref, gm, wret_ref, wgm_ref, wo_ref, h_ref):
    branch_ret = _dot_pieces(og_ref[...].astype(BF16), wret_ref)
    branch_gm = _dot_pieces(gm, wgm_ref)
    m = (jax.nn.sigmoid(ar_ref[...]) * branch_ret + jax.nn.sigmoid(ag_ref[...]) * branch_gm)
    h_ref[...] = x_ref[...] + _dot_pieces(m.astype(BF16), wo_ref)


REST_COLS = D_IN - RET_COLS
REST_GU, REST_GV, REST_AR, REST_AG = 0, GM_WIDTH, 2 * GM_WIDTH, 2 * GM_WIDTH + D_MODEL


def _proj_mix_kernel(*refs, n_side):
    (x_ref, xn_ref, og_ref, wuv_ref, waa_ref, lng_ref, lnb_ref, ws_ref, bs_ref,
     wret_ref, wgm_ref, wo_ref) = refs[:12]
    side_in = refs[12:12 + n_side]
    h_ref = refs[12 + n_side]
    side_out = refs[13 + n_side:13 + 2 * n_side]
    zr_ref, gm_ref, br_ref, m_ref = refs[13 + 2 * n_side:]

    L = GM_CHUNK
    n_chunks = x_ref.shape[0] // L
    n_pieces = D_MODEL // MXU_COLS
    piece = lambda p, base=0: slice(base + p * MXU_COLS, base + (p + 1) * MXU_COLS)

    for p in range(n_pieces):
        zr_ref[:, piece(p, REST_GV)] = _dot(xn_ref[...], wuv_ref[n_pieces + p])
    for p in range(n_pieces):
        zr_ref[:, piece(p, REST_GU)] = _dot(xn_ref[...], wuv_ref[p])

    causal = lax.broadcasted_iota(jnp.int32, (L, L), 0) >= lax.broadcasted_iota(jnp.int32, (L, L), 1)
    w_causal = [jnp.where(causal, ws_ref[g], 0.0).astype(BF16) for g in range(GM_GROUPS)]
    for c in range(n_chunks):
        for p in range(c * n_pieces // n_chunks, (c + 1) * n_pieces // n_chunks):
            zr_ref[:, piece(p, REST_AR)] = _dot(xn_ref[...], waa_ref[p])
            zr_ref[:, piece(p, REST_AG)] = _dot(xn_ref[...], waa_ref[n_pieces + p])
            br_ref[:, piece(p)] = _dot(og_ref[...], wret_ref[p])
        rows = slice(c * L, (c + 1) * L)
        u = _gelu_tanh(zr_ref[rows, REST_GU:REST_GU + GM_WIDTH])
        v = _layer_norm(_gelu_tanh(zr_ref[rows, REST_GV:REST_GV + GM_WIDTH]), lng_ref[...],
                        lnb_ref[...]).astype(BF16)
        for g in range(GM_GROUPS):
            cols = slice(g * GM_CG, (g + 1) * GM_CG)
            mixed = _dot(w_causal[g], v[:, cols]) + bs_ref[:, g:g + 1]
            gm_ref[rows, cols] = (u[:, cols] * mixed).astype(BF16)
    for p in range(n_pieces):
        branch_gm = _dot(gm_ref[...], wgm_ref[p])
        m_ref[:, piece(p)] = (jax.nn.sigmoid(zr_ref[:, piece(p, REST_AR)]) * br_ref[:, piece(p)]
                              + jax.nn.sigmoid(zr_ref[:, piece(p, REST_AG)]) * branch_gm).astype(BF16)
    for p in range(n_pieces):
        h_ref[:, piece(p)] = x_ref[:, piece(p)] + _dot(m_ref[...], wo_ref[p])
    _side_cast(side_in, side_out)


def _proj_mix(x, xn, og, w_pieces, ln_g, ln_b, ws, bs, wret, wgm, wo, side_casts, *, tm):
    m = x.shape[0]
    half = REST_COLS // 2
    assert RET_COLS % half == 0
    full = lambda a: pl.BlockSpec(a.shape, lambda i: (0,) * a.ndim, pipeline_mode=pl.Buffered(1))
    wcols = lambda k: pl.BlockSpec(_pieces_shape(D_MODEL, half), lambda i: (RET_COLS // half + k, 0, 0),
                                   pipeline_mode=pl.Buffered(1))
    ln_g = ln_g.reshape(1, GM_WIDTH)
    ln_b = ln_b.reshape(1, GM_WIDTH)
    bs = bs.T
    side_in, side_out, side_shapes = _side_cast_specs(side_casts, m // tm, lambda i: i, pieces=False)
    row_block = lambda width: pl.BlockSpec((tm, width), lambda i: (i, 0))
    return pl.pallas_call(
        functools.partial(_proj_mix_kernel, n_side=len(side_casts)),
        grid=(m // tm,),
        in_specs=[row_block(D_MODEL), row_block(D_MODEL), row_block(RET_V), wcols(0), wcols(1),
                  full(ln_g), full(ln_b), full(ws), full(bs), full(wret), full(wgm), full(wo)] + side_in,
        out_specs=[row_block(D_MODEL)] + side_out,
        out_shape=[jax.ShapeDtypeStruct((m, D_MODEL), F32)] + side_shapes,
        scratch_shapes=[pltpu.VMEM((tm, REST_COLS), F32), pltpu.VMEM((tm, GM_WIDTH), BF16),
                        pltpu.VMEM((tm, D_MODEL), F32), pltpu.VMEM((tm, D_MODEL), BF16)],
        compiler_params=_params("parallel"),
        name="proj_mix",
    )(x, xn, og, w_pieces, w_pieces, ln_g, ln_b, ws, bs, wret, wgm, wo, *side_casts)


def _mix_step_kernel(x_ref, gu_ref, gv_ref, ar_ref, ag_ref, og_ref, lng_ref, lnb_ref, ws_ref, bs_ref,
                     wret_ref, wgm_ref, wo_ref, h_ref, v_ref):
    u = _gelu_tanh(gu_ref[...])
    v = _layer_norm(_gelu_tanh(gv_ref[...]), lng_ref[...], lnb_ref[...])
    v_ref[...] = v
    gm = (u * (ws_ref[...] * v + bs_ref[...])).astype(BF16)
    _merge(x_ref, ar_ref, ag_ref, og_ref, gm, wret_ref, wgm_ref, wo_ref, h_ref)


def _mix_step(x, z, og, ln_g, ln_b, ws, bs, wret, wgm, wo):
    m = x.shape[0]
    zcol = lambda off: pl.BlockSpec((m, D_MODEL), lambda i: (0, off // D_MODEL))
    full = lambda a: pl.BlockSpec(a.shape, lambda i: (0,) * a.ndim)
    ln_g = ln_g.reshape(1, GM_WIDTH)
    ln_b = ln_b.reshape(1, GM_WIDTH)
    ws = jnp.repeat(ws[:, 0, 0], GM_CG).reshape(1, GM_WIDTH)
    bs = jnp.repeat(bs[:, 0], GM_CG).reshape(1, GM_WIDTH)
    return pl.pallas_call(
        _mix_step_kernel,
        grid=(1,),
        in_specs=[full(x), zcol(COL_GU), zcol(COL_GV), zcol(COL_AR), zcol(COL_AG), full(og),
                  full(ln_g), full(ln_b), full(ws), full(bs), full(wret), full(wgm), full(wo)],
        out_specs=[pl.BlockSpec((m, D_MODEL), lambda i: (0, 0)), pl.BlockSpec((m, GM_WIDTH), lambda i: (0, 0))],
        out_shape=[jax.ShapeDtypeStruct((m, D_MODEL), F32), jax.ShapeDtypeStruct((m, GM_WIDTH), F32)],
        compiler_params=_params("arbitrary"),
        name="mix_step",
    )(x, z, z, z, z, og, ln_g, ln_b, ws, bs, wret, wgm, wo)


FFN_PART_ROWS = 256


def _ffn_kernel(h_ref, g_ref, win_ref, wdown_ref, gf_ref, y_ref, *, final_norm):
    tm = h_ref.shape[0]
    n_parts = max(1, tm // FFN_PART_ROWS)
    for r in range(n_parts):
        rows = slice(r * tm // n_parts, (r + 1) * tm // n_parts)
        h = h_ref[rows, :]
        hn = (_rms(h) * g_ref[...]).astype(BF16)
        f = _dot(hn, win_ref[...])
        f_gate = f[:, :D_FF]
        f_up = f[:, D_FF:]
        act = (f_gate * jax.nn.sigmoid(f_gate) * f_up).astype(BF16)
        out = h + _dot(act, wdown_ref[...])
        if final_norm:
            out = _rms(out) * gf_ref[...]
        y_ref[rows, :] = out


def _ffn(h, g, win, wdown, g_final, *, tm, final_norm):
    m = h.shape[0]
    full = lambda a: pl.BlockSpec(a.shape, lambda i: (0,) * a.ndim, pipeline_mode=pl.Buffered(1))
    g = g.reshape(1, D_MODEL)
    g_final = g_final.reshape(1, D_MODEL)
    return pl.pallas_call(
        functools.partial(_ffn_kernel, final_norm=final_norm),
        grid=(m // tm,),
        in_specs=[pl.BlockSpec((tm, D_MODEL), lambda i: (i, 0)), full(g), full(win), full(wdown),
                  full(g_final)],
        out_specs=pl.BlockSpec((tm, D_MODEL), lambda i: (i, 0)),
        out_shape=jax.ShapeDtypeStruct((m, D_MODEL), F32),
        compiler_params=_params("parallel"),
        name="ffn",
    )(h, g, win, wdown, g_final)


def _rope_tables(pos):
    inv = ROPE_BASE ** (-jnp.arange(ROPE_HALF, dtype=F32) / ROPE_HALF)
    ang = pos[:, None] * inv[None, :]
    return jnp.cos(ang), jnp.sin(ang)


def kernel(x_prompt, x_sample, state_ret, norm_mix_g, w_in, w_ret_o, gm_ln_g, gm_ln_b, gm_ws, gm_bs,
           w_gm_o, w_o, norm_ffn_g, w_ffn_in, w_ffn_down, norm_final_g):
    batch, seq, _ = x_prompt.shape
    dec_batch, dec_seq, _ = x_sample.shape
    depth = w_in.shape[0]
    assert dec_seq == 1 and seq % PR_BLOCK == 0

    cos_p, sin_p = _rope_tables(jnp.arange(seq, dtype=F32))
    cos_s, sin_s = _rope_tables(PAST_LEN + jnp.arange(dec_seq, dtype=F32))

    hp = x_prompt.reshape(batch * seq, D_MODEL)
    hs = x_sample.reshape(dec_batch, D_MODEL)
    ret_p, ret_s, gmv_s = [], [], []
    for l in range(depth):
        last = l == depth - 1
        zs, win = _in_proj(hs, norm_mix_g[l], w_in[l], tn=1024)
        og, sp, ogs, ss, xn, wret, wgm, wo = _proj_ret(
            hp, norm_mix_g[l], win, cos_p, sin_p, zs, cos_s, sin_s, state_ret[l],
            (w_ret_o[l], w_gm_o[l], w_o[l]), batch=batch, seq=seq)

        h, wfin, wfdown = _proj_mix(hp, xn, og, win, gm_ln_g[l], gm_ln_b[l], gm_ws[l], gm_bs[l],
                                    wret, wgm, wo, (w_ffn_in[l], w_ffn_down[l]), tm=512)
        hp = _ffn(h, norm_ffn_g[l], wfin, wfdown, norm_final_g, tm=1024, final_norm=last)
        ret_p.append(sp)

        h, vs = _mix_step(hs, zs, ogs, gm_ln_g[l], gm_ln_b[l], gm_ws[l], gm_bs[l], wret, wgm, wo)
        hs = _ffn(h, norm_ffn_g[l], wfin, wfdown, norm_final_g, tm=dec_batch, final_norm=last)
        ret_s.append(ss)
        gmv_s.append(vs.reshape(dec_batch, dec_seq, GM_WIDTH))

    return (hp.reshape(batch, seq, D_MODEL), hs.reshape(dec_batch, dec_seq, D_MODEL),
            jnp.stack(ret_p), jnp.stack(ret_s), jnp.stack(gmv_s))
```

```python
import functools
import math

import numpy as np
import jax
import jax.numpy as jnp
from jax import lax
from jax.experimental import pallas as pl
from jax.experimental.pallas import tpu as pltpu
from jax.experimental.pallas import tpu_sc as plsc

D_MODEL = 1024
PAST_LEN = 16384
RET_DK = 256
RET_HEADS = D_MODEL // 256
RET_DV = 2 * RET_DK
RET_QK = RET_HEADS * RET_DK
RET_V = RET_HEADS * RET_DV
RET_CHUNK = 128
ROPE_BASE = 10000.0
ROPE_HALF = RET_DK // 2
GM_GROUPS = 4
GM_WIDTH = D_MODEL
GM_CG = GM_WIDTH // GM_GROUPS
GM_CHUNK = 128
D_FF = ((8 * D_MODEL // 3 + 255) // 256) * 256
EPS = 1e-6
D_IN = 2 * RET_QK + 2 * RET_V + 2 * GM_WIDTH + 2 * D_MODEL

COL_Q = 0
COL_K = RET_QK
COL_V = 2 * RET_QK
COL_G = 2 * RET_QK + RET_V
COL_GU = 2 * RET_QK + 2 * RET_V
COL_GV = COL_GU + GM_WIDTH
COL_AR = COL_GV + GM_WIDTH
COL_AG = COL_AR + D_MODEL

LOG_GAMMA = tuple(float(np.log1p(-np.exp2(np.float32(-5.0 - h)))) for h in range(RET_HEADS))

VMEM_LIMIT_BYTES = 56 * 1024 * 1024
MXU_COLS = 256

F32 = jnp.float32
BF16 = jnp.bfloat16


def _dot(a, b):
    return jnp.dot(a, b, preferred_element_type=F32)


def _rms(x):
    return x * lax.rsqrt(jnp.mean(x * x, axis=-1, keepdims=True) + EPS)


def _params(*semantics):
    return pltpu.CompilerParams(dimension_semantics=semantics, vmem_limit_bytes=VMEM_LIMIT_BYTES)


BF16_ROWS = 16


def _pieces_shape(rows, cols):
    return (cols // MXU_COLS, rows, MXU_COLS)


def _side_cast_specs(arrays, n_steps, step_of, *, pieces):
    in_specs, out_specs, out_shapes = [], [], []
    for a in arrays:
        rows, cols = a.shape
        n_blocks = max(n for n in range(1, n_steps + 1) if rows % n == 0 and (rows // n) % BF16_ROWS == 0)
        block = lambda *ids, n_blocks=n_blocks: step_of(*ids) * n_blocks // n_steps
        in_specs.append(pl.BlockSpec((rows // n_blocks, cols), lambda *ids, block=block: (block(*ids), 0)))
        if pieces:
            out_specs.append(pl.BlockSpec(_pieces_shape(rows // n_blocks, cols),
                                          lambda *ids, block=block: (0, block(*ids), 0)))
            out_shapes.append(jax.ShapeDtypeStruct(_pieces_shape(rows, cols), BF16))
        else:
            out_specs.append(in_specs[-1])
            out_shapes.append(jax.ShapeDtypeStruct(a.shape, BF16))
    return in_specs, out_specs, out_shapes


def _store_pieces(dst_ref, value):
    for p in range(dst_ref.shape[0]):
        dst_ref[p] = value[:, p * MXU_COLS:(p + 1) * MXU_COLS]


def _side_cast(side_in, side_out):
    for src, dst in zip(side_in, side_out):
        if len(dst.shape) == 3:
            _store_pieces(dst, src[...].astype(BF16))
        else:
            dst[...] = src[...].astype(BF16)


def _in_proj_kernel(x_ref, g_ref, w_ref, z_ref, wb_ref, xn_ref):
    @pl.when(pl.program_id(0) == 0)
    def _():
        xn_ref[...] = (_rms(x_ref[...]) * g_ref[...]).astype(BF16)

    w = w_ref[...].astype(BF16)
    _store_pieces(wb_ref, w)
    z_ref[...] = _dot(xn_ref[...], w)


def _in_proj(x, g, w, *, tn):
    m = x.shape[0]
    return pl.pallas_call(
        _in_proj_kernel,
        grid=(D_IN // tn,),
        in_specs=[
            pl.BlockSpec((m, D_MODEL), lambda j: (0, 0)),
            pl.BlockSpec((1, D_MODEL), lambda j: (0, 0)),
            pl.BlockSpec((D_MODEL, tn), lambda j: (0, j)),
        ],
        out_specs=[pl.BlockSpec((m, tn), lambda j: (0, j)),
                   pl.BlockSpec(_pieces_shape(D_MODEL, tn), lambda j: (j, 0, 0))],
        out_shape=[jax.ShapeDtypeStruct((m, D_IN), F32),
                   jax.ShapeDtypeStruct(_pieces_shape(D_MODEL, D_IN), BF16)],
        scratch_shapes=[pltpu.VMEM((m, D_MODEL), BF16)],
        compiler_params=_params("arbitrary"),
        name="in_proj",
    )(x, g.reshape(1, D_MODEL), w)


def _rotate(x, cos, sin):
    x1 = x[:, :ROPE_HALF]
    x2 = x[:, ROPE_HALF:]
    return jnp.concatenate([x1 * cos - x2 * sin, x1 * sin + x2 * cos], axis=-1)


RET_COLS = COL_GU
PR_BLOCK = 512
PR_CHUNKS = PR_BLOCK // RET_CHUNK
PR_TILE = RET_COLS // PR_CHUNKS
PR_TILE_PIECES = PR_TILE // MXU_COLS
RET_PIECES = RET_COLS // MXU_COLS


def _decay_tables():
    L = RET_CHUNK
    lg = np.array(LOG_GAMMA, np.float32)
    n = np.arange(L, dtype=np.float32)
    diff = n[:, None] - n[None, :]
    intra = np.where(diff[None] >= 0, np.exp(np.maximum(diff, 0.0)[None] * lg[:, None, None]), 0.0)
    q_decay = np.exp((n + 1.0)[None, :, None] * lg[:, None, None])
    k_decay = np.exp((L - 1.0 - n)[None, :, None] * lg[:, None, None])
    return (jnp.asarray(intra, F32), jnp.asarray(np.broadcast_to(q_decay, (RET_HEADS, L, RET_DV)), F32),
            jnp.asarray(np.broadcast_to(k_decay, (RET_HEADS, L, RET_DK)), F32))


def _retention_chunk(z_ref, rows, cos, sin, intra_ref, qd_ref, kd_ref, st_ref, og_ref, interleave):
    def zcols(start, width):
        first = start // MXU_COLS
        return jnp.concatenate([z_ref[first + i, rows, :] for i in range(width // MXU_COLS)], axis=-1)

    qs, ks, vs, scores = [], [], [], []
    for h in range(RET_HEADS):
        if h % 2 == 0:
            interleave(h // 2)
        q = _rotate(zcols(COL_Q + h * RET_DK, RET_DK).astype(F32), cos, sin)
        k = _rotate(zcols(COL_K + h * RET_DK, RET_DK).astype(F32), cos, sin) * (RET_DK ** -0.5)
        qs.append(q.astype(BF16))
        ks.append((k * kd_ref[h]).astype(BF16))
        vs.append(zcols(COL_V + h * RET_DV, RET_DV))
        scores.append((lax.dot_general(qs[h], k.astype(BF16), (((1,), (1,)), ((), ())),
                                       preferred_element_type=F32) * intra_ref[h]).astype(BF16))
    for h in range(RET_HEADS):
        if h % 2 == 0:
            interleave(RET_HEADS // 2 + h // 2)
        state = st_ref[0, h]
        o = _dot(scores[h], vs[h]) + _dot(qs[h], state.astype(BF16)) * qd_ref[h]
        st_ref[0, h] = math.exp(RET_CHUNK * LOG_GAMMA[h]) * state + lax.dot_general(
            ks[h], vs[h], (((0,), (0,)), ((), ())), preferred_element_type=F32)
        gate = zcols(COL_G + h * RET_DV, RET_DV).astype(F32)
        og_ref[:, h * RET_DV:(h + 1) * RET_DV] = (gate * jax.nn.sigmoid(gate) * _rms(o)).astype(og_ref.dtype)


SC_LANES = 16
SC_ROWS = 32


def _sample_qk_kernel(z_ref, cos_ref, sin_ref, q_ref, k_ref):
    for h in range(RET_HEADS):
        cols = slice(h * RET_DK, (h + 1) * RET_DK)
        q_ref[:, cols] = _rotate(z_ref[:, COL_Q + h * RET_DK:COL_Q + (h + 1) * RET_DK], cos_ref[...], sin_ref[...])
        k_ref[:, cols] = _rotate(z_ref[:, COL_K + h * RET_DK:COL_K + (h + 1) * RET_DK], cos_ref[...],
                                 sin_ref[...]) * (RET_DK ** -0.5)


def _sample_qk(z, cos, sin):
    m = z.shape[0]
    full = lambda a: pl.BlockSpec(a.shape, lambda i: (0,) * a.ndim)
    out = pl.BlockSpec((m, RET_QK), lambda i: (0, 0))
    return pl.pallas_call(
        _sample_qk_kernel,
        grid=(1,),
        in_specs=[pl.BlockSpec((m, 2 * RET_QK), lambda i: (0, 0)), full(cos), full(sin)],
        out_specs=[out, out],
        out_shape=[jax.ShapeDtypeStruct((m, RET_QK), F32)] * 2,
        compiler_params=_params("arbitrary"),
        name="sample_qk",
    )(z, cos, sin)


def _state_update(state, q, k, v):
    n_seq = state.shape[0]
    n_pairs = n_seq * RET_HEADS
    blocks = RET_DK // SC_ROWS
    state2d = state.reshape(n_pairs * RET_DK, RET_DV)
    q2d = q.reshape(n_pairs * blocks, SC_ROWS)
    k2d = k.reshape(n_pairs * blocks, SC_ROWS)
    v2d = v.reshape(n_pairs, RET_DV)
    gamma = jnp.asarray(np.tile(np.exp(np.array(LOG_GAMMA, np.float32))[None, :, None],
                                (n_seq, 1, SC_LANES)).reshape(n_pairs, SC_LANES))
    mesh = plsc.VectorSubcoreMesh(core_axis_name="core", subcore_axis_name="subcore")

    @pl.kernel(out_type=[jax.ShapeDtypeStruct(state2d.shape, F32),
                         jax.ShapeDtypeStruct((n_pairs * blocks, RET_DV), F32)],
               mesh=mesh, scratch_types=[])
    def state_update(st_hbm, q_hbm, k_hbm, v_hbm, g_hbm, new_hbm, o_hbm):
        def body(st, q, k, v, g, new, o):
            lane0 = pl.ds(0, SC_LANES)
            row0 = pl.ds(0, 1)
            gam = g.at[row0, lane0][...]

            @pl.loop(0, RET_DV, step=SC_LANES)
            def _(c):
                cols = pl.ds(c, SC_LANES)
                vv = v.at[row0, cols][...]
                acc = jnp.zeros((1, SC_LANES), F32)
                for r0 in range(0, SC_ROWS, SC_LANES):
                    kv = k.at[row0, pl.ds(r0, SC_LANES)][...]
                    qv = q.at[row0, pl.ds(r0, SC_LANES)][...]
                    for i in range(SC_LANES):
                        row = pl.ds(r0 + i, 1)
                        n = gam * st.at[row, cols][...] + kv[0, i] * vv
                        new.at[row, cols][...] = n
                        acc = acc + qv[0, i] * n
                o.at[row0, cols][...] = acc

        row_block = lambda p, b: (p * blocks + b, 0)
        pair = lambda p, b: (p, 0)
        pltpu.emit_pipeline(
            body,
            grid=(n_pairs, blocks),
            in_specs=[pl.BlockSpec((SC_ROWS, RET_DV), row_block), pl.BlockSpec((1, SC_ROWS), row_block),
                      pl.BlockSpec((1, SC_ROWS), row_block), pl.BlockSpec((1, RET_DV), pair),
                      pl.BlockSpec((1, SC_LANES), pair)],
            out_specs=[pl.BlockSpec((SC_ROWS, RET_DV), row_block), pl.BlockSpec((1, RET_DV), row_block)],
            core_axis_name=("core", "subcore"),
            dimension_semantics=(pltpu.PARALLEL, pltpu.PARALLEL),
        )(st_hbm, q_hbm, k_hbm, v_hbm, g_hbm, new_hbm, o_hbm)

    new_state, o_parts = state_update(state2d, q2d, k2d, v2d, gamma)
    return new_state.reshape(state.shape), o_parts.reshape(n_seq, RET_HEADS, blocks, RET_DV)


def _proj_ret_kernel(*refs, blocks_per_seq, n_side):
    x_ref, g_ref, w_ref, cos_ref, sin_ref, intra_ref, qd_ref, kd_ref = refs[:8]
    side_in = refs[8:8 + n_side]
    og_ref, st_ref, xn_ref = refs[8 + n_side:11 + n_side]
    side_out = refs[11 + n_side:11 + 2 * n_side]
    za_ref, zb_ref = refs[11 + 2 * n_side:]
    t = pl.program_id(0)
    j = pl.program_id(1)
    ret_block = jnp.maximum(t - 1, 0)
    pos_rows = pl.ds(pl.multiple_of(((ret_block % blocks_per_seq) * PR_CHUNKS + j) * RET_CHUNK, RET_CHUNK),
                     RET_CHUNK)

    @pl.when(j == 0)
    def _():
        xn_ref[...] = (_rms(x_ref[...]) * g_ref[...]).astype(BF16)

    @pl.when(jnp.logical_and(t == 0, j == 0))
    def _():
        zb_ref[...] = jnp.zeros_like(zb_ref)

    @pl.when(jnp.logical_and(ret_block % blocks_per_seq == 0, j == 0))
    def _():
        st_ref[...] = jnp.zeros_like(st_ref)

    rows = pl.ds(pl.multiple_of(j * RET_CHUNK, RET_CHUNK), RET_CHUNK)

    def step(z_write, z_read):
        def interleave(slot):
            for p in range(slot * PR_TILE_PIECES // RET_HEADS, (slot + 1) * PR_TILE_PIECES // RET_HEADS):
                piece = j * PR_TILE_PIECES + p
                z_write[piece] = _dot(xn_ref[...], w_ref[piece]).astype(BF16)

        _retention_chunk(z_read, rows, cos_ref[pos_rows, :], sin_ref[pos_rows, :], intra_ref, qd_ref, kd_ref,
                         st_ref, og_ref, interleave)

    @pl.when(t % 2 == 0)
    def _():
        step(za_ref, zb_ref)

    @pl.when(t % 2 == 1)
    def _():
        step(zb_ref, za_ref)

    _side_cast(side_in, side_out)


def _proj_ret(x, g, w_pieces, cos, sin, side_casts, *, batch, seq):
    n_blocks = batch * seq // PR_BLOCK
    blocks_per_seq = seq // PR_BLOCK
    intra, q_decay, k_decay = _decay_tables()
    full = lambda a: pl.BlockSpec(a.shape, lambda t, j: (0,) * a.ndim, pipeline_mode=pl.Buffered(1))
    proj_block = lambda t: jnp.minimum(t, n_blocks - 1)
    ret_block = lambda t: jnp.maximum(t - 1, 0)
    og_block = lambda t: jnp.where(t == 0, n_blocks, t - 1)
    side_in, side_out, side_shapes = _side_cast_specs(
        side_casts, (n_blocks + 1) * PR_CHUNKS, lambda t, j: t * PR_CHUNKS + j, pieces=True)
    return pl.pallas_call(
        functools.partial(_proj_ret_kernel, blocks_per_seq=blocks_per_seq, n_side=len(side_casts)),
        grid=(n_blocks + 1, PR_CHUNKS),
        in_specs=[
            pl.BlockSpec((PR_BLOCK, D_MODEL), lambda t, j: (proj_block(t), 0)),
            pl.BlockSpec((1, D_MODEL), lambda t, j: (0, 0)),
            pl.BlockSpec(_pieces_shape(D_MODEL, RET_COLS), lambda t, j: (0, 0, 0), pipeline_mode=pl.Buffered(1)),
            full(cos), full(sin), full(intra), full(q_decay), full(k_decay),
        ] + side_in,
        out_specs=[
            pl.BlockSpec((RET_CHUNK, RET_V), lambda t, j: (og_block(t) * PR_CHUNKS + j, 0)),
            pl.BlockSpec((1, RET_HEADS, RET_DK, RET_DV), lambda t, j: (ret_block(t) // blocks_per_seq, 0, 0, 0)),
            pl.BlockSpec((PR_BLOCK, D_MODEL), lambda t, j: (proj_block(t), 0)),
        ] + side_out,
        out_shape=[
            jax.ShapeDtypeStruct(((n_blocks + 1) * PR_BLOCK, RET_V), BF16),
            jax.ShapeDtypeStruct((batch, RET_HEADS, RET_DK, RET_DV), F32),
            jax.ShapeDtypeStruct((n_blocks * PR_BLOCK, D_MODEL), BF16),
        ] + side_shapes,
        scratch_shapes=[
            pltpu.VMEM(_pieces_shape(PR_BLOCK, RET_COLS), BF16),
            pltpu.VMEM(_pieces_shape(PR_BLOCK, RET_COLS), BF16),
        ],
        compiler_params=_params("arbitrary", "arbitrary"),
        name="proj_ret",
    )(x, g.reshape(1, D_MODEL), w_pieces, cos, sin, intra, q_decay, k_decay, *side_casts)


def _gelu_tanh(x):
    c = math.sqrt(2.0 / math.pi)
    return x * (0.5 + 0.5 * jnp.tanh(x * (c + (c * 0.044715) * (x * x))))


def _layer_norm(x, g, b):
    xc = x - jnp.mean(x, axis=-1, keepdims=True)
    return xc * lax.rsqrt(jnp.mean(xc * xc, axis=-1, keepdims=True) + EPS) * g + b


def _dot_pieces(a, w_ref):
    return jnp.concatenate([_dot(a, w_ref[p]) for p in range(w_ref.shape[0])], axis=-1)


def _merge(x_ref, ar_ref, ag_ref, og_ref, gm, wret_ref, wgm_ref, wo_ref, h_ref):
    branch_ret = _dot_pieces(og_ref[...].astype(BF16), wret_ref)
    branch_gm = _dot_pieces(gm, wgm_ref)
    m = (jax.nn.sigmoid(ar_ref[...]) * branch_ret + jax.nn.sigmoid(ag_ref[...]) * branch_gm)
    h_ref[...] = x_ref[...] + _dot_pieces(m.astype(BF16), wo_ref)


REST_COLS = D_IN - RET_COLS
REST_GU, REST_GV, REST_AR, REST_AG = 0, GM_WIDTH, 2 * GM_WIDTH, 2 * GM_WIDTH + D_MODEL


def _proj_mix_kernel(*refs, n_side):
    (x_ref, xn_ref, og_ref, wuv_ref, waa_ref, lng_ref, lnb_ref, ws_ref, bs_ref,
     wret_ref, wgm_ref, wo_ref) = refs[:12]
    side_in = refs[12:12 + n_side]
    h_ref = refs[12 + n_side]
    side_out = refs[13 + n_side:13 + 2 * n_side]
    zr_ref, gm_ref, br_ref, m_ref = refs[13 + 2 * n_side:]

    L = GM_CHUNK
    n_chunks = x_ref.shape[0] // L
    n_pieces = D_MODEL // MXU_COLS
    piece = lambda p, base=0: slice(base + p * MXU_COLS, base + (p + 1) * MXU_COLS)

    for p in range(n_pieces):
        zr_ref[:, piece(p, REST_GV)] = _dot(xn_ref[...], wuv_ref[n_pieces + p])
    for p in range(n_pieces):
        zr_ref[:, piece(p, REST_GU)] = _dot(xn_ref[...], wuv_ref[p])

    causal = lax.broadcasted_iota(jnp.int32, (L, L), 0) >= lax.broadcasted_iota(jnp.int32, (L, L), 1)
    w_causal = [jnp.where(causal, ws_ref[g], 0.0).astype(BF16) for g in range(GM_GROUPS)]
    for c in range(n_chunks):
        for p in range(c * n_pieces // n_chunks, (c + 1) * n_pieces // n_chunks):
            zr_ref[:, piece(p, REST_AR)] = _dot(xn_ref[...], waa_ref[p])
            zr_ref[:, piece(p, REST_AG)] = _dot(xn_ref[...], waa_ref[n_pieces + p])
            br_ref[:, piece(p)] = _dot(og_ref[...], wret_ref[p])
        rows = slice(c * L, (c + 1) * L)
        u = _gelu_tanh(zr_ref[rows, REST_GU:REST_GU + GM_WIDTH])
        v = _layer_norm(_gelu_tanh(zr_ref[rows, REST_GV:REST_GV + GM_WIDTH]), lng_ref[...],
                        lnb_ref[...]).astype(BF16)
        for g in range(GM_GROUPS):
            cols = slice(g * GM_CG, (g + 1) * GM_CG)
            mixed = _dot(w_causal[g], v[:, cols]) + bs_ref[:, g:g + 1]
            gm_ref[rows, cols] = (u[:, cols] * mixed).astype(BF16)
    for p in range(n_pieces):
        branch_gm = _dot(gm_ref[...], wgm_ref[p])
        m_ref[:, piece(p)] = (jax.nn.sigmoid(zr_ref[:, piece(p, REST_AR)]) * br_ref[:, piece(p)]
                              + jax.nn.sigmoid(zr_ref[:, piece(p, REST_AG)]) * branch_gm).astype(BF16)
    for p in range(n_pieces):
        h_ref[:, piece(p)] = x_ref[:, piece(p)] + _dot(m_ref[...], wo_ref[p])
    _side_cast(side_in, side_out)


def _proj_mix(x, xn, og, w_pieces, ln_g, ln_b, ws, bs, wret, wgm, wo, side_casts, *, tm):
    m = x.shape[0]
    half = REST_COLS // 2
    assert RET_COLS % half == 0
    full = lambda a: pl.BlockSpec(a.shape, lambda i: (0,) * a.ndim, pipeline_mode=pl.Buffered(1))
    wcols = lambda k: pl.BlockSpec(_pieces_shape(D_MODEL, half), lambda i: (RET_COLS // half + k, 0, 0),
                                   pipeline_mode=pl.Buffered(1))
    ln_g = ln_g.reshape(1, GM_WIDTH)
    ln_b = ln_b.reshape(1, GM_WIDTH)
    bs = bs.T
    side_in, side_out, side_shapes = _side_cast_specs(side_casts, m // tm, lambda i: i, pieces=False)
    row_block = lambda width: pl.BlockSpec((tm, width), lambda i: (i, 0))
    return pl.pallas_call(
        functools.partial(_proj_mix_kernel, n_side=len(side_casts)),
        grid=(m // tm,),
        in_specs=[row_block(D_MODEL), row_block(D_MODEL), row_block(RET_V), wcols(0), wcols(1),
                  full(ln_g), full(ln_b), full(ws), full(bs), full(wret), full(wgm), full(wo)] + side_in,
        out_specs=[row_block(D_MODEL)] + side_out,
        out_shape=[jax.ShapeDtypeStruct((m, D_MODEL), F32)] + side_shapes,
        scratch_shapes=[pltpu.VMEM((tm, REST_COLS), F32), pltpu.VMEM((tm, GM_WIDTH), BF16),
                        pltpu.VMEM((tm, D_MODEL), F32), pltpu.VMEM((tm, D_MODEL), BF16)],
        compiler_params=_params("parallel"),
        name="proj_mix",
    )(x, xn, og, w_pieces, w_pieces, ln_g, ln_b, ws, bs, wret, wgm, wo, *side_casts)


def _mix_step_kernel(x_ref, gu_ref, gv_ref, ar_ref, ag_ref, gate_ref, op_ref, lng_ref, lnb_ref, ws_ref, bs_ref,
                     wret_ref, wgm_ref, wo_ref, h_ref, v_ref, og_ref):
    for hd in range(RET_HEADS):
        cols = slice(hd * RET_DV, (hd + 1) * RET_DV)
        o = jnp.sum(op_ref[:, hd], axis=1)
        gate = gate_ref[:, cols]
        og_ref[:, cols] = gate * jax.nn.sigmoid(gate) * _rms(o)
    u = _gelu_tanh(gu_ref[...])
    v = _layer_norm(_gelu_tanh(gv_ref[...]), lng_ref[...], lnb_ref[...])
    v_ref[...] = v
    gm = (u * (ws_ref[...] * v + bs_ref[...])).astype(BF16)
    _merge(x_ref, ar_ref, ag_ref, og_ref, gm, wret_ref, wgm_ref, wo_ref, h_ref)


def _mix_step(x, z, o_parts, ln_g, ln_b, ws, bs, wret, wgm, wo):
    m = x.shape[0]
    zcol = lambda off: pl.BlockSpec((m, D_MODEL), lambda i: (0, off // D_MODEL))
    full = lambda a: pl.BlockSpec(a.shape, lambda i: (0,) * a.ndim)
    ln_g = ln_g.reshape(1, GM_WIDTH)
    ln_b = ln_b.reshape(1, GM_WIDTH)
    ws = jnp.repeat(ws[:, 0, 0], GM_CG).reshape(1, GM_WIDTH)
    bs = jnp.repeat(bs[:, 0], GM_CG).reshape(1, GM_WIDTH)
    return pl.pallas_call(
        _mix_step_kernel,
        grid=(1,),
        in_specs=[full(x), zcol(COL_GU), zcol(COL_GV), zcol(COL_AR), zcol(COL_AG),
                  pl.BlockSpec((m, RET_V), lambda i: (0, COL_G // RET_V)), full(o_parts),
                  full(ln_g), full(ln_b), full(ws), full(bs), full(wret), full(wgm), full(wo)],
        out_specs=[pl.BlockSpec((m, D_MODEL), lambda i: (0, 0)), pl.BlockSpec((m, GM_WIDTH), lambda i: (0, 0))],
        out_shape=[jax.ShapeDtypeStruct((m, D_MODEL), F32), jax.ShapeDtypeStruct((m, GM_WIDTH), F32)],
        scratch_shapes=[pltpu.VMEM((m, RET_V), F32)],
        compiler_params=_params("arbitrary"),
        name="mix_step",
    )(x, z, z, z, z, z, o_parts, ln_g, ln_b, ws, bs, wret, wgm, wo)


FFN_PART_ROWS = 256


def _ffn_kernel(h_ref, g_ref, win_ref, wdown_ref, gf_ref, y_ref, *, final_norm):
    tm = h_ref.shape[0]
    n_parts = max(1, tm // FFN_PART_ROWS)
    for r in range(n_parts):
        rows = slice(r * tm // n_parts, (r + 1) * tm // n_parts)
        h = h_ref[rows, :]
        hn = (_rms(h) * g_ref[...]).astype(BF16)
        f = _dot(hn, win_ref[...])
        f_gate = f[:, :D_FF]
        f_up = f[:, D_FF:]
        act = (f_gate * jax.nn.sigmoid(f_gate) * f_up).astype(BF16)
        out = h + _dot(act, wdown_ref[...])
        if final_norm:
            out = _rms(out) * gf_ref[...]
        y_ref[rows, :] = out


def _ffn(h, g, win, wdown, g_final, *, tm, final_norm):
    m = h.shape[0]
    full = lambda a: pl.BlockSpec(a.shape, lambda i: (0,) * a.ndim, pipeline_mode=pl.Buffered(1))
    g = g.reshape(1, D_MODEL)
    g_final = g_final.reshape(1, D_MODEL)
    return pl.pallas_call(
        functools.partial(_ffn_kernel, final_norm=final_norm),
        grid=(m // tm,),
        in_specs=[pl.BlockSpec((tm, D_MODEL), lambda i: (i, 0)), full(g), full(win), full(wdown),
                  full(g_final)],
        out_specs=pl.BlockSpec((tm, D_MODEL), lambda i: (i, 0)),
        out_shape=jax.ShapeDtypeStruct((m, D_MODEL), F32),
        compiler_params=_params("parallel"),
        name="ffn",
    )(h, g, win, wdown, g_final)


def _rope_tables(pos):
    inv = ROPE_BASE ** (-jnp.arange(ROPE_HALF, dtype=F32) / ROPE_HALF)
    ang = pos[:, None] * inv[None, :]
    return jnp.cos(ang), jnp.sin(ang)


def kernel(x_prompt, x_sample, state_ret, norm_mix_g, w_in, w_ret_o, gm_ln_g, gm_ln_b, gm_ws, gm_bs,
           w_gm_o, w_o, norm_ffn_g, w_ffn_in, w_ffn_down, norm_final_g):
    batch, seq, _ = x_prompt.shape
    dec_batch, dec_seq, _ = x_sample.shape
    depth = w_in.shape[0]
    assert dec_seq == 1 and seq % PR_BLOCK == 0

    cos_p, sin_p = _rope_tables(jnp.arange(seq, dtype=F32))
    cos_s, sin_s = _rope_tables(PAST_LEN + jnp.arange(dec_seq, dtype=F32))

    hp = x_prompt.reshape(batch * seq, D_MODEL)
    hs = x_sample.reshape(dec_batch, D_MODEL)
    ret_p, ret_s, gmv_s = [], [], []
    for l in range(depth):
        last = l == depth - 1
        zs, win = _in_proj(hs, norm_mix_g[l], w_in[l], tn=1024)
        qs, ks = _sample_qk(zs, cos_s, sin_s)
        ss, o_parts = _state_update(state_ret[l], qs, ks, zs[:, COL_V:COL_V + RET_V])
        og, sp, xn, wret, wgm, wo = _proj_ret(
            hp, norm_mix_g[l], win, cos_p, sin_p, (w_ret_o[l], w_gm_o[l], w_o[l]), batch=batch, seq=seq)

        h, wfin, wfdown = _proj_mix(hp, xn, og, win, gm_ln_g[l], gm_ln_b[l], gm_ws[l], gm_bs[l],
                                    wret, wgm, wo, (w_ffn_in[l], w_ffn_down[l]), tm=512)
        hp = _ffn(h, norm_ffn_g[l], wfin, wfdown, norm_final_g, tm=1024, final_norm=last)
        ret_p.append(sp)

        h, vs = _mix_step(hs, zs, o_parts, gm_ln_g[l], gm_ln_b[l], gm_ws[l], gm_bs[l], wret, wgm, wo)
        hs = _ffn(h, norm_ffn_g[l], wfin, wfdown, norm_final_g, tm=dec_batch, final_norm=last)
        ret_s.append(ss)
        gmv_s.append(vs.reshape(dec_batch, dec_seq, GM_WIDTH))

    return (hp.reshape(batch, seq, D_MODEL), hs.reshape(dec_batch, dec_seq, D_MODEL),
            jnp.stack(ret_p), jnp.stack(ret_s), jnp.stack(gmv_s))
```

```python
import functools
import math

import numpy as np
import jax
import jax.numpy as jnp
from jax import lax
from jax.experimental import pallas as pl
from jax.experimental.pallas import tpu as pltpu
from jax.experimental.pallas import tpu_sc as plsc

D_MODEL = 1024
PAST_LEN = 16384
RET_DK = 256
RET_HEADS = D_MODEL // 256
RET_DV = 2 * RET_DK
RET_QK = RET_HEADS * RET_DK
RET_V = RET_HEADS * RET_DV
RET_CHUNK = 128
ROPE_BASE = 10000.0
ROPE_HALF = RET_DK // 2
GM_GROUPS = 4
GM_WIDTH = D_MODEL
GM_CG = GM_WIDTH // GM_GROUPS
GM_CHUNK = 128
D_FF = ((8 * D_MODEL // 3 + 255) // 256) * 256
EPS = 1e-6
D_IN = 2 * RET_QK + 2 * RET_V + 2 * GM_WIDTH + 2 * D_MODEL

COL_Q = 0
COL_K = RET_QK
COL_V = 2 * RET_QK
COL_G = 2 * RET_QK + RET_V
COL_GU = 2 * RET_QK + 2 * RET_V
COL_GV = COL_GU + GM_WIDTH
COL_AR = COL_GV + GM_WIDTH
COL_AG = COL_AR + D_MODEL

LOG_GAMMA = tuple(float(np.log1p(-np.exp2(np.float32(-5.0 - h)))) for h in range(RET_HEADS))

VMEM_LIMIT_BYTES = 56 * 1024 * 1024
MXU_COLS = 256

F32 = jnp.float32
BF16 = jnp.bfloat16


def _dot(a, b):
    return jnp.dot(a, b, preferred_element_type=F32)


def _rms(x):
    return x * lax.rsqrt(jnp.mean(x * x, axis=-1, keepdims=True) + EPS)


def _params(*semantics):
    return pltpu.CompilerParams(dimension_semantics=semantics, vmem_limit_bytes=VMEM_LIMIT_BYTES)


BF16_ROWS = 16


def _pieces_shape(rows, cols):
    return (cols // MXU_COLS, rows, MXU_COLS)


def _side_cast_specs(arrays, n_steps, step_of, *, pieces):
    in_specs, out_specs, out_shapes = [], [], []
    for a in arrays:
        rows, cols = a.shape
        n_blocks = max(n for n in range(1, n_steps + 1) if rows % n == 0 and (rows // n) % BF16_ROWS == 0)
        block = lambda *ids, n_blocks=n_blocks: step_of(*ids) * n_blocks // n_steps
        in_specs.append(pl.BlockSpec((rows // n_blocks, cols), lambda *ids, block=block: (block(*ids), 0)))
        if pieces:
            out_specs.append(pl.BlockSpec(_pieces_shape(rows // n_blocks, cols),
                                          lambda *ids, block=block: (0, block(*ids), 0)))
            out_shapes.append(jax.ShapeDtypeStruct(_pieces_shape(rows, cols), BF16))
        else:
            out_specs.append(in_specs[-1])
            out_shapes.append(jax.ShapeDtypeStruct(a.shape, BF16))
    return in_specs, out_specs, out_shapes


def _store_pieces(dst_ref, value):
    for p in range(dst_ref.shape[0]):
        dst_ref[p] = value[:, p * MXU_COLS:(p + 1) * MXU_COLS]


def _side_cast(side_in, side_out):
    for src, dst in zip(side_in, side_out):
        if len(dst.shape) == 3:
            _store_pieces(dst, src[...].astype(BF16))
        else:
            dst[...] = src[...].astype(BF16)


def _in_proj_kernel(x_ref, g_ref, w_ref, z_ref, wb_ref, xn_ref):
    @pl.when(pl.program_id(0) == 0)
    def _():
        xn_ref[...] = (_rms(x_ref[...]) * g_ref[...]).astype(BF16)

    w = w_ref[...].astype(BF16)
    _store_pieces(wb_ref, w)
    z_ref[...] = _dot(xn_ref[...], w)


def _in_proj(x, g, w, *, tn):
    m = x.shape[0]
    return pl.pallas_call(
        _in_proj_kernel,
        grid=(D_IN // tn,),
        in_specs=[
            pl.BlockSpec((m, D_MODEL), lambda j: (0, 0)),
            pl.BlockSpec((1, D_MODEL), lambda j: (0, 0)),
            pl.BlockSpec((D_MODEL, tn), lambda j: (0, j)),
        ],
        out_specs=[pl.BlockSpec((m, tn), lambda j: (0, j)),
                   pl.BlockSpec(_pieces_shape(D_MODEL, tn), lambda j: (j, 0, 0))],
        out_shape=[jax.ShapeDtypeStruct((m, D_IN), F32),
                   jax.ShapeDtypeStruct(_pieces_shape(D_MODEL, D_IN), BF16)],
        scratch_shapes=[pltpu.VMEM((m, D_MODEL), BF16)],
        compiler_params=_params("arbitrary"),
        name="in_proj",
    )(x, g.reshape(1, D_MODEL), w)


def _rotate(x, cos, sin):
    x1 = x[:, :ROPE_HALF]
    x2 = x[:, ROPE_HALF:]
    return jnp.concatenate([x1 * cos - x2 * sin, x1 * sin + x2 * cos], axis=-1)


RET_COLS = COL_GU
PR_BLOCK = 512
PR_CHUNKS = PR_BLOCK // RET_CHUNK
PR_TILE = RET_COLS // PR_CHUNKS
PR_TILE_PIECES = PR_TILE // MXU_COLS
RET_PIECES = RET_COLS // MXU_COLS


def _decay_tables():
    L = RET_CHUNK
    lg = np.array(LOG_GAMMA, np.float32)
    n = np.arange(L, dtype=np.float32)
    diff = n[:, None] - n[None, :]
    intra = np.where(diff[None] >= 0, np.exp(np.maximum(diff, 0.0)[None] * lg[:, None, None]), 0.0)
    q_decay = np.exp((n + 1.0)[None, :, None] * lg[:, None, None])
    k_decay = np.exp((L - 1.0 - n)[None, :, None] * lg[:, None, None])
    return (jnp.asarray(intra, F32), jnp.asarray(np.broadcast_to(q_decay, (RET_HEADS, L, RET_DV)), F32),
            jnp.asarray(np.broadcast_to(k_decay, (RET_HEADS, L, RET_DK)), F32))


def _retention_chunk(z_ref, rows, cos, sin, intra_ref, qd_ref, kd_ref, st_ref, og_ref, interleave):
    def zcols(start, width):
        first = start // MXU_COLS
        return jnp.concatenate([z_ref[first + i, rows, :] for i in range(width // MXU_COLS)], axis=-1)

    qs, ks, vs, scores = [], [], [], []
    for h in range(RET_HEADS):
        if h % 2 == 0:
            interleave(h // 2)
        q = _rotate(zcols(COL_Q + h * RET_DK, RET_DK).astype(F32), cos, sin)
        k = _rotate(zcols(COL_K + h * RET_DK, RET_DK).astype(F32), cos, sin) * (RET_DK ** -0.5)
        qs.append(q.astype(BF16))
        ks.append((k * kd_ref[h]).astype(BF16))
        vs.append(zcols(COL_V + h * RET_DV, RET_DV))
        scores.append((lax.dot_general(qs[h], k.astype(BF16), (((1,), (1,)), ((), ())),
                                       preferred_element_type=F32) * intra_ref[h]).astype(BF16))
    for h in range(RET_HEADS):
        if h % 2 == 0:
            interleave(RET_HEADS // 2 + h // 2)
        state = st_ref[0, h]
        o = _dot(scores[h], vs[h]) + _dot(qs[h], state.astype(BF16)) * qd_ref[h]
        st_ref[0, h] = math.exp(RET_CHUNK * LOG_GAMMA[h]) * state + lax.dot_general(
            ks[h], vs[h], (((0,), (0,)), ((), ())), preferred_element_type=F32)
        gate = zcols(COL_G + h * RET_DV, RET_DV).astype(F32)
        og_ref[:, h * RET_DV:(h + 1) * RET_DV] = (gate * jax.nn.sigmoid(gate) * _rms(o)).astype(og_ref.dtype)


SC_LANES = 16
SC_ROWS = 32


def _sample_qk_kernel(z_ref, cos_ref, sin_ref, q_ref, k_ref):
    for h in range(RET_HEADS):
        cols = slice(h * RET_DK, (h + 1) * RET_DK)
        q_ref[:, cols] = _rotate(z_ref[:, COL_Q + h * RET_DK:COL_Q + (h + 1) * RET_DK], cos_ref[...], sin_ref[...])
        k_ref[:, cols] = _rotate(z_ref[:, COL_K + h * RET_DK:COL_K + (h + 1) * RET_DK], cos_ref[...],
                                 sin_ref[...]) * (RET_DK ** -0.5)


def _sample_qk(z, cos, sin):
    m = z.shape[0]
    full = lambda a: pl.BlockSpec(a.shape, lambda i: (0,) * a.ndim)
    out = pl.BlockSpec((m, RET_QK), lambda i: (0, 0))
    return pl.pallas_call(
        _sample_qk_kernel,
        grid=(1,),
        in_specs=[pl.BlockSpec((m, 2 * RET_QK), lambda i: (0, 0)), full(cos), full(sin)],
        out_specs=[out, out],
        out_shape=[jax.ShapeDtypeStruct((m, RET_QK), F32)] * 2,
        compiler_params=_params("arbitrary"),
        name="sample_qk",
    )(z, cos, sin)


def _state_update(state, q, k, z):
    n_seq = state.shape[0]
    n_pairs = n_seq * RET_HEADS
    blocks = RET_DK // SC_ROWS
    state2d = state.reshape(n_pairs * RET_DK, RET_DV)
    gamma = jnp.asarray(np.tile(np.exp(np.array(LOG_GAMMA, np.float32))[None, :, None],
                                (n_seq, 1, SC_LANES)).reshape(n_pairs, SC_LANES))
    mesh = plsc.VectorSubcoreMesh(core_axis_name="core", subcore_axis_name="subcore")

    @pl.kernel(out_type=[jax.ShapeDtypeStruct(state2d.shape, F32),
                         jax.ShapeDtypeStruct((n_pairs * blocks, RET_DV), F32)],
               mesh=mesh, scratch_types=[])
    def state_update(st_hbm, q_hbm, k_hbm, v_hbm, g_hbm, new_hbm, o_hbm):
        def body(st, q, k, v, g, new, o):
            lane0 = pl.ds(0, SC_LANES)
            row0 = pl.ds(0, 1)
            gam = g.at[row0, lane0][...]

            @pl.loop(0, RET_DV, step=SC_LANES)
            def _(c):
                cols = pl.ds(c, SC_LANES)
                vv = v.at[row0, cols][...]
                acc = jnp.zeros((1, SC_LANES), F32)
                for r0 in range(0, SC_ROWS, SC_LANES):
                    kv = k.at[row0, pl.ds(r0, SC_LANES)][...]
                    qv = q.at[row0, pl.ds(r0, SC_LANES)][...]
                    for i in range(SC_LANES):
                        row = pl.ds(r0 + i, 1)
                        n = gam * st.at[row, cols][...] + kv[0, i] * vv
                        new.at[row, cols][...] = n
                        acc = acc + qv[0, i] * n
                o.at[row0, cols][...] = acc

        row_block = lambda p, b: (p * blocks + b, 0)
        pair = lambda p, b: (p, 0)
        v_cols = lambda p, b: (p // RET_HEADS, COL_V // RET_DV + p % RET_HEADS)
        pltpu.emit_pipeline(
            body,
            grid=(n_pairs, blocks),
            in_specs=[pl.BlockSpec((SC_ROWS, RET_DV), row_block), pl.BlockSpec((1, SC_ROWS), row_block),
                      pl.BlockSpec((1, SC_ROWS), row_block), pl.BlockSpec((1, RET_DV), v_cols),
                      pl.BlockSpec((1, SC_LANES), pair)],
            out_specs=[pl.BlockSpec((SC_ROWS, RET_DV), row_block), pl.BlockSpec((1, RET_DV), row_block)],
            core_axis_name=("core", "subcore"),
            dimension_semantics=(pltpu.PARALLEL, pltpu.PARALLEL),
        )(st_hbm, q_hbm, k_hbm, v_hbm, g_hbm, new_hbm, o_hbm)

    q2d = q.reshape(n_pairs * blocks, SC_ROWS)
    k2d = k.reshape(n_pairs * blocks, SC_ROWS)
    new_state, o_parts = state_update(state2d, q2d, k2d, z, gamma)
    return new_state.reshape(state.shape), o_parts.reshape(n_seq, RET_HEADS, blocks, RET_DV)


def _proj_ret_kernel(*refs, blocks_per_seq, n_side):
    x_ref, g_ref, w_ref, cos_ref, sin_ref, intra_ref, qd_ref, kd_ref = refs[:8]
    side_in = refs[8:8 + n_side]
    og_ref, st_ref, xn_ref = refs[8 + n_side:11 + n_side]
    side_out = refs[11 + n_side:11 + 2 * n_side]
    za_ref, zb_ref = refs[11 + 2 * n_side:]
    t = pl.program_id(0)
    j = pl.program_id(1)
    ret_block = jnp.maximum(t - 1, 0)
    pos_rows = pl.ds(pl.multiple_of(((ret_block % blocks_per_seq) * PR_CHUNKS + j) * RET_CHUNK, RET_CHUNK),
                     RET_CHUNK)

    @pl.when(j == 0)
    def _():
        xn_ref[...] = (_rms(x_ref[...]) * g_ref[...]).astype(BF16)

    @pl.when(jnp.logical_and(t == 0, j == 0))
    def _():
        zb_ref[...] = jnp.zeros_like(zb_ref)

    @pl.when(jnp.logical_and(ret_block % blocks_per_seq == 0, j == 0))
    def _():
        st_ref[...] = jnp.zeros_like(st_ref)

    rows = pl.ds(pl.multiple_of(j * RET_CHUNK, RET_CHUNK), RET_CHUNK)

    def step(z_write, z_read):
        def interleave(slot):
            for p in range(slot * PR_TILE_PIECES // RET_HEADS, (slot + 1) * PR_TILE_PIECES // RET_HEADS):
                piece = j * PR_TILE_PIECES + p
                z_write[piece] = _dot(xn_ref[...], w_ref[piece]).astype(BF16)

        _retention_chunk(z_read, rows, cos_ref[pos_rows, :], sin_ref[pos_rows, :], intra_ref, qd_ref, kd_ref,
                         st_ref, og_ref, interleave)

    @pl.when(t % 2 == 0)
    def _():
        step(za_ref, zb_ref)

    @pl.when(t % 2 == 1)
    def _():
        step(zb_ref, za_ref)

    _side_cast(side_in, side_out)


def _proj_ret(x, g, w_pieces, cos, sin, side_casts, *, batch, seq):
    n_blocks = batch * seq // PR_BLOCK
    blocks_per_seq = seq // PR_BLOCK
    intra, q_decay, k_decay = _decay_tables()
    full = lambda a: pl.BlockSpec(a.shape, lambda t, j: (0,) * a.ndim, pipeline_mode=pl.Buffered(1))
    proj_block = lambda t: jnp.minimum(t, n_blocks - 1)
    ret_block = lambda t: jnp.maximum(t - 1, 0)
    og_block = lambda t: jnp.where(t == 0, n_blocks, t - 1)
    side_in, side_out, side_shapes = _side_cast_specs(
        side_casts, (n_blocks + 1) * PR_CHUNKS, lambda t, j: t * PR_CHUNKS + j, pieces=True)
    return pl.pallas_call(
        functools.partial(_proj_ret_kernel, blocks_per_seq=blocks_per_seq, n_side=len(side_casts)),
        grid=(n_blocks + 1, PR_CHUNKS),
        in_specs=[
            pl.BlockSpec((PR_BLOCK, D_MODEL), lambda t, j: (proj_block(t), 0)),
            pl.BlockSpec((1, D_MODEL), lambda t, j: (0, 0)),
            pl.BlockSpec(_pieces_shape(D_MODEL, RET_COLS), lambda t, j: (0, 0, 0), pipeline_mode=pl.Buffered(1)),
            full(cos), full(sin), full(intra), full(q_decay), full(k_decay),
        ] + side_in,
        out_specs=[
            pl.BlockSpec((RET_CHUNK, RET_V), lambda t, j: (og_block(t) * PR_CHUNKS + j, 0)),
            pl.BlockSpec((1, RET_HEADS, RET_DK, RET_DV), lambda t, j: (ret_block(t) // blocks_per_seq, 0, 0, 0)),
            pl.BlockSpec((PR_BLOCK, D_MODEL), lambda t, j: (proj_block(t), 0)),
        ] + side_out,
        out_shape=[
            jax.ShapeDtypeStruct(((n_blocks + 1) * PR_BLOCK, RET_V), BF16),
            jax.ShapeDtypeStruct((batch, RET_HEADS, RET_DK, RET_DV), F32),
            jax.ShapeDtypeStruct((n_blocks * PR_BLOCK, D_MODEL), BF16),
        ] + side_shapes,
        scratch_shapes=[
            pltpu.VMEM(_pieces_shape(PR_BLOCK, RET_COLS), BF16),
            pltpu.VMEM(_pieces_shape(PR_BLOCK, RET_COLS), BF16),
        ],
        compiler_params=_params("arbitrary", "arbitrary"),
        name="proj_ret",
    )(x, g.reshape(1, D_MODEL), w_pieces, cos, sin, intra, q_decay, k_decay, *side_casts)


def _gelu_tanh(x):
    c = math.sqrt(2.0 / math.pi)
    return x * (0.5 + 0.5 * jnp.tanh(x * (c + (c * 0.044715) * (x * x))))


def _layer_norm(x, g, b):
    xc = x - jnp.mean(x, axis=-1, keepdims=True)
    return xc * lax.rsqrt(jnp.mean(xc * xc, axis=-1, keepdims=True) + EPS) * g + b


def _dot_pieces(a, w_ref):
    return jnp.concatenate([_dot(a, w_ref[p]) for p in range(w_ref.shape[0])], axis=-1)


def _merge(x_ref, ar_ref, ag_ref, og_ref, gm, wret_ref, wgm_ref, wo_ref, h_ref):
    branch_ret = _dot_pieces(og_ref[...].astype(BF16), wret_ref)
    branch_gm = _dot_pieces(gm, wgm_ref)
    m = (jax.nn.sigmoid(ar_ref[...]) * branch_ret + jax.nn.sigmoid(ag_ref[...]) * branch_gm)
    h_ref[...] = x_ref[...] + _dot_pieces(m.astype(BF16), wo_ref)


REST_COLS = D_IN - RET_COLS
REST_GU, REST_GV, REST_AR, REST_AG = 0, GM_WIDTH, 2 * GM_WIDTH, 2 * GM_WIDTH + D_MODEL


def _proj_mix_kernel(*refs, n_side):
    (x_ref, xn_ref, og_ref, wuv_ref, waa_ref, lng_ref, lnb_ref, ws_ref, bs_ref,
     wret_ref, wgm_ref, wo_ref) = refs[:12]
    side_in = refs[12:12 + n_side]
    h_ref = refs[12 + n_side]
    side_out = refs[13 + n_side:13 + 2 * n_side]
    zr_ref, gm_ref, br_ref, m_ref = refs[13 + 2 * n_side:]

    L = GM_CHUNK
    n_chunks = x_ref.shape[0] // L
    n_pieces = D_MODEL // MXU_COLS
    piece = lambda p, base=0: slice(base + p * MXU_COLS, base + (p + 1) * MXU_COLS)

    for p in range(n_pieces):
        zr_ref[:, piece(p, REST_GV)] = _dot(xn_ref[...], wuv_ref[n_pieces + p])
    for p in range(n_pieces):
        zr_ref[:, piece(p, REST_GU)] = _dot(xn_ref[...], wuv_ref[p])

    causal = lax.broadcasted_iota(jnp.int32, (L, L), 0) >= lax.broadcasted_iota(jnp.int32, (L, L), 1)
    w_causal = [jnp.where(causal, ws_ref[g], 0.0).astype(BF16) for g in range(GM_GROUPS)]
    for c in range(n_chunks):
        for p in range(c * n_pieces // n_chunks, (c + 1) * n_pieces // n_chunks):
            zr_ref[:, piece(p, REST_AR)] = _dot(xn_ref[...], waa_ref[p])
            zr_ref[:, piece(p, REST_AG)] = _dot(xn_ref[...], waa_ref[n_pieces + p])
            br_ref[:, piece(p)] = _dot(og_ref[...], wret_ref[p])
        rows = slice(c * L, (c + 1) * L)
        u = _gelu_tanh(zr_ref[rows, REST_GU:REST_GU + GM_WIDTH])
        v = _layer_norm(_gelu_tanh(zr_ref[rows, REST_GV:REST_GV + GM_WIDTH]), lng_ref[...],
                        lnb_ref[...]).astype(BF16)
        for g in range(GM_GROUPS):
            cols = slice(g * GM_CG, (g + 1) * GM_CG)
            mixed = _dot(w_causal[g], v[:, cols]) + bs_ref[:, g:g + 1]
            gm_ref[rows, cols] = (u[:, cols] * mixed).astype(BF16)
    for p in range(n_pieces):
        branch_gm = _dot(gm_ref[...], wgm_ref[p])
        m_ref[:, piece(p)] = (jax.nn.sigmoid(zr_ref[:, piece(p, REST_AR)]) * br_ref[:, piece(p)]
                              + jax.nn.sigmoid(zr_ref[:, piece(p, REST_AG)]) * branch_gm).astype(BF16)
    for p in range(n_pieces):
        h_ref[:, piece(p)] = x_ref[:, piece(p)] + _dot(m_ref[...], wo_ref[p])
    _side_cast(side_in, side_out)


def _proj_mix(x, xn, og, w_pieces, ln_g, ln_b, ws, bs, wret, wgm, wo, side_casts, *, tm):
    m = x.shape[0]
    half = REST_COLS // 2
    assert RET_COLS % half == 0
    full = lambda a: pl.BlockSpec(a.shape, lambda i: (0,) * a.ndim, pipeline_mode=pl.Buffered(1))
    wcols = lambda k: pl.BlockSpec(_pieces_shape(D_MODEL, half), lambda i: (RET_COLS // half + k, 0, 0),
                                   pipeline_mode=pl.Buffered(1))
    ln_g = ln_g.reshape(1, GM_WIDTH)
    ln_b = ln_b.reshape(1, GM_WIDTH)
    bs = bs.T
    side_in, side_out, side_shapes = _side_cast_specs(side_casts, m // tm, lambda i: i, pieces=False)
    row_block = lambda width: pl.BlockSpec((tm, width), lambda i: (i, 0))
    return pl.pallas_call(
        functools.partial(_proj_mix_kernel, n_side=len(side_casts)),
        grid=(m // tm,),
        in_specs=[row_block(D_MODEL), row_block(D_MODEL), row_block(RET_V), wcols(0), wcols(1),
                  full(ln_g), full(ln_b), full(ws), full(bs), full(wret), full(wgm), full(wo)] + side_in,
        out_specs=[row_block(D_MODEL)] + side_out,
        out_shape=[jax.ShapeDtypeStruct((m, D_MODEL), F32)] + side_shapes,
        scratch_shapes=[pltpu.VMEM((tm, REST_COLS), F32), pltpu.VMEM((tm, GM_WIDTH), BF16),
                        pltpu.VMEM((tm, D_MODEL), F32), pltpu.VMEM((tm, D_MODEL), BF16)],
        compiler_params=_params("parallel"),
        name="proj_mix",
    )(x, xn, og, w_pieces, w_pieces, ln_g, ln_b, ws, bs, wret, wgm, wo, *side_casts)


def _mix_step_kernel(x_ref, gu_ref, gv_ref, ar_ref, ag_ref, gate_ref, op_ref, lng_ref, lnb_ref, ws_ref, bs_ref,
                     wret_ref, wgm_ref, wo_ref, h_ref, v_ref, og_ref):
    for hd in range(RET_HEADS):
        cols = slice(hd * RET_DV, (hd + 1) * RET_DV)
        o = jnp.sum(op_ref[:, hd], axis=1)
        gate = gate_ref[:, cols]
        og_ref[:, cols] = gate * jax.nn.sigmoid(gate) * _rms(o)
    u = _gelu_tanh(gu_ref[...])
    v = _layer_norm(_gelu_tanh(gv_ref[...]), lng_ref[...], lnb_ref[...])
    v_ref[...] = v
    gm = (u * (ws_ref[...] * v + bs_ref[...])).astype(BF16)
    _merge(x_ref, ar_ref, ag_ref, og_ref, gm, wret_ref, wgm_ref, wo_ref, h_ref)


def _mix_step(x, z, o_parts, ln_g, ln_b, ws, bs, wret, wgm, wo):
    m = x.shape[0]
    zcol = lambda off: pl.BlockSpec((m, D_MODEL), lambda i: (0, off // D_MODEL))
    full = lambda a: pl.BlockSpec(a.shape, lambda i: (0,) * a.ndim)
    ln_g = ln_g.reshape(1, GM_WIDTH)
    ln_b = ln_b.reshape(1, GM_WIDTH)
    ws = jnp.repeat(ws[:, 0, 0], GM_CG).reshape(1, GM_WIDTH)
    bs = jnp.repeat(bs[:, 0], GM_CG).reshape(1, GM_WIDTH)
    return pl.pallas_call(
        _mix_step_kernel,
        grid=(1,),
        in_specs=[full(x), zcol(COL_GU), zcol(COL_GV), zcol(COL_AR), zcol(COL_AG),
                  pl.BlockSpec((m, RET_V), lambda i: (0, COL_G // RET_V)), full(o_parts),
                  full(ln_g), full(ln_b), full(ws), full(bs), full(wret), full(wgm), full(wo)],
        out_specs=[pl.BlockSpec((m, D_MODEL), lambda i: (0, 0)), pl.BlockSpec((m, GM_WIDTH), lambda i: (0, 0))],
        out_shape=[jax.ShapeDtypeStruct((m, D_MODEL), F32), jax.ShapeDtypeStruct((m, GM_WIDTH), F32)],
        scratch_shapes=[pltpu.VMEM((m, RET_V), F32)],
        compiler_params=_params("arbitrary"),
        name="mix_step",
    )(x, z, z, z, z, z, o_parts, ln_g, ln_b, ws, bs, wret, wgm, wo)


FFN_PART_ROWS = 256


def _ffn_kernel(h_ref, g_ref, win_ref, wdown_ref, gf_ref, y_ref, *, final_norm):
    tm = h_ref.shape[0]
    n_parts = max(1, tm // FFN_PART_ROWS)
    for r in range(n_parts):
        rows = slice(r * tm // n_parts, (r + 1) * tm // n_parts)
        h = h_ref[rows, :]
        hn = (_rms(h) * g_ref[...]).astype(BF16)
        f = _dot(hn, win_ref[...])
        f_gate = f[:, :D_FF]
        f_up = f[:, D_FF:]
        act = (f_gate * jax.nn.sigmoid(f_gate) * f_up).astype(BF16)
        out = h + _dot(act, wdown_ref[...])
        if final_norm:
            out = _rms(out) * gf_ref[...]
        y_ref[rows, :] = out


def _ffn(h, g, win, wdown, g_final, *, tm, final_norm):
    m = h.shape[0]
    full = lambda a: pl.BlockSpec(a.shape, lambda i: (0,) * a.ndim, pipeline_mode=pl.Buffered(1))
    g = g.reshape(1, D_MODEL)
    g_final = g_final.reshape(1, D_MODEL)
    return pl.pallas_call(
        functools.partial(_ffn_kernel, final_norm=final_norm),
        grid=(m // tm,),
        in_specs=[pl.BlockSpec((tm, D_MODEL), lambda i: (i, 0)), full(g), full(win), full(wdown),
                  full(g_final)],
        out_specs=pl.BlockSpec((tm, D_MODEL), lambda i: (i, 0)),
        out_shape=jax.ShapeDtypeStruct((m, D_MODEL), F32),
        compiler_params=_params("parallel"),
        name="ffn",
    )(h, g, win, wdown, g_final)


def _rope_tables(pos):
    inv = ROPE_BASE ** (-jnp.arange(ROPE_HALF, dtype=F32) / ROPE_HALF)
    ang = pos[:, None] * inv[None, :]
    return jnp.cos(ang), jnp.sin(ang)


def kernel(x_prompt, x_sample, state_ret, norm_mix_g, w_in, w_ret_o, gm_ln_g, gm_ln_b, gm_ws, gm_bs,
           w_gm_o, w_o, norm_ffn_g, w_ffn_in, w_ffn_down, norm_final_g):
    batch, seq, _ = x_prompt.shape
    dec_batch, dec_seq, _ = x_sample.shape
    depth = w_in.shape[0]
    assert dec_seq == 1 and seq % PR_BLOCK == 0

    cos_p, sin_p = _rope_tables(jnp.arange(seq, dtype=F32))
    cos_s, sin_s = _rope_tables(PAST_LEN + jnp.arange(dec_seq, dtype=F32))

    hp = x_prompt.reshape(batch * seq, D_MODEL)
    hs = x_sample.reshape(dec_batch, D_MODEL)
    ret_p, ret_s, gmv_s = [], [], []
    for l in range(depth):
        last = l == depth - 1
        zs, win = _in_proj(hs, norm_mix_g[l], w_in[l], tn=1024)
        qs, ks = _sample_qk(zs, cos_s, sin_s)
        ss, o_parts = _state_update(state_ret[l], qs, ks, zs)
        og, sp, xn, wret, wgm, wo = _proj_ret(
            hp, norm_mix_g[l], win, cos_p, sin_p, (w_ret_o[l], w_gm_o[l], w_o[l]), batch=batch, seq=seq)

        h, wfin, wfdown = _proj_mix(hp, xn, og, win, gm_ln_g[l], gm_ln_b[l], gm_ws[l], gm_bs[l],
                                    wret, wgm, wo, (w_ffn_in[l], w_ffn_down[l]), tm=512)
        hp = _ffn(h, norm_ffn_g[l], wfin, wfdown, norm_final_g, tm=1024, final_norm=last)
        ret_p.append(sp)

        h, vs = _mix_step(hs, zs, o_parts, gm_ln_g[l], gm_ln_b[l], gm_ws[l], gm_bs[l], wret, wgm, wo)
        hs = _ffn(h, norm_ffn_g[l], wfin, wfdown, norm_final_g, tm=dec_batch, final_norm=last)
        ret_s.append(ss)
        gmv_s.append(vs.reshape(dec_batch, dec_seq, GM_WIDTH))

    return (hp.reshape(batch, seq, D_MODEL), hs.reshape(dec_batch, dec_seq, D_MODEL),
            jnp.stack(ret_p), jnp.stack(ret_s), jnp.stack(gmv_s))
```

```python
import functools
import math

import numpy as np
import jax
import jax.numpy as jnp
from jax import lax
from jax.experimental import pallas as pl
from jax.experimental.pallas import tpu as pltpu
from jax.experimental.pallas import tpu_sc as plsc

D_MODEL = 1024
PAST_LEN = 16384
RET_DK = 256
RET_HEADS = D_MODEL // 256
RET_DV = 2 * RET_DK
RET_QK = RET_HEADS * RET_DK
RET_V = RET_HEADS * RET_DV
RET_CHUNK = 128
ROPE_BASE = 10000.0
ROPE_HALF = RET_DK // 2
GM_GROUPS = 4
GM_WIDTH = D_MODEL
GM_CG = GM_WIDTH // GM_GROUPS
GM_CHUNK = 128
D_FF = ((8 * D_MODEL // 3 + 255) // 256) * 256
EPS = 1e-6
D_IN = 2 * RET_QK + 2 * RET_V + 2 * GM_WIDTH + 2 * D_MODEL

COL_Q = 0
COL_K = RET_QK
COL_V = 2 * RET_QK
COL_G = 2 * RET_QK + RET_V
COL_GU = 2 * RET_QK + 2 * RET_V
COL_GV = COL_GU + GM_WIDTH
COL_AR = COL_GV + GM_WIDTH
COL_AG = COL_AR + D_MODEL

LOG_GAMMA = tuple(float(np.log1p(-np.exp2(np.float32(-5.0 - h)))) for h in range(RET_HEADS))

VMEM_LIMIT_BYTES = 56 * 1024 * 1024
MXU_COLS = 256

MIX_BLOCK = 512
FFN_BLOCK = 1024
IN_PROJ_TILE = 2048

F32 = jnp.float32
BF16 = jnp.bfloat16


def _dot(a, b):
    return jnp.dot(a, b, preferred_element_type=F32)


def _rms(x):
    return x * lax.rsqrt(jnp.mean(x * x, axis=-1, keepdims=True) + EPS)


def _params(*semantics):
    return pltpu.CompilerParams(dimension_semantics=semantics, vmem_limit_bytes=VMEM_LIMIT_BYTES)


BF16_ROWS = 16


def _pieces_shape(rows, cols):
    return (cols // MXU_COLS, rows, MXU_COLS)


def _side_cast_specs(arrays, n_steps, step_of, *, pieces):
    in_specs, out_specs, out_shapes = [], [], []
    for a in arrays:
        rows, cols = a.shape
        n_blocks = max(n for n in range(1, n_steps + 1) if rows % n == 0 and (rows // n) % BF16_ROWS == 0)
        block = lambda *ids, n_blocks=n_blocks: step_of(*ids) * n_blocks // n_steps
        in_specs.append(pl.BlockSpec((rows // n_blocks, cols), lambda *ids, block=block: (block(*ids), 0)))
        if pieces:
            out_specs.append(pl.BlockSpec(_pieces_shape(rows // n_blocks, cols),
                                          lambda *ids, block=block: (0, block(*ids), 0)))
            out_shapes.append(jax.ShapeDtypeStruct(_pieces_shape(rows, cols), BF16))
        else:
            out_specs.append(in_specs[-1])
            out_shapes.append(jax.ShapeDtypeStruct(a.shape, BF16))
    return in_specs, out_specs, out_shapes


def _store_pieces(dst_ref, value):
    for p in range(dst_ref.shape[0]):
        dst_ref[p] = value[:, p * MXU_COLS:(p + 1) * MXU_COLS]


def _side_cast(side_in, side_out):
    for src, dst in zip(side_in, side_out):
        if len(dst.shape) == 3:
            _store_pieces(dst, src[...].astype(BF16))
        else:
            dst[...] = src[...].astype(BF16)


def _in_proj_kernel(x_ref, g_ref, w_ref, z_ref, wb_ref, xn_ref):
    @pl.when(pl.program_id(0) == 0)
    def _():
        xn_ref[...] = (_rms(x_ref[...]) * g_ref[...]).astype(BF16)

    w = w_ref[...].astype(BF16)
    _store_pieces(wb_ref, w)
    z_ref[...] = _dot(xn_ref[...], w)


def _in_proj(x, g, w, *, tn):
    m = x.shape[0]
    return pl.pallas_call(
        _in_proj_kernel,
        grid=(D_IN // tn,),
        in_specs=[
            pl.BlockSpec((m, D_MODEL), lambda j: (0, 0)),
            pl.BlockSpec((1, D_MODEL), lambda j: (0, 0)),
            pl.BlockSpec((D_MODEL, tn), lambda j: (0, j)),
        ],
        out_specs=[pl.BlockSpec((m, tn), lambda j: (0, j)),
                   pl.BlockSpec(_pieces_shape(D_MODEL, tn), lambda j: (j, 0, 0))],
        out_shape=[jax.ShapeDtypeStruct((m, D_IN), F32),
                   jax.ShapeDtypeStruct(_pieces_shape(D_MODEL, D_IN), BF16)],
        scratch_shapes=[pltpu.VMEM((m, D_MODEL), BF16)],
        compiler_params=_params("arbitrary"),
        name="in_proj",
    )(x, g.reshape(1, D_MODEL), w)


def _rotate(x, cos, sin):
    x1 = x[:, :ROPE_HALF]
    x2 = x[:, ROPE_HALF:]
    return jnp.concatenate([x1 * cos - x2 * sin, x1 * sin + x2 * cos], axis=-1)


RET_COLS = COL_GU
PR_BLOCK = 512
PR_CHUNKS = PR_BLOCK // RET_CHUNK
PR_TILE = RET_COLS // PR_CHUNKS
PR_TILE_PIECES = PR_TILE // MXU_COLS
RET_PIECES = RET_COLS // MXU_COLS


def _decay_tables():
    L = RET_CHUNK
    lg = np.array(LOG_GAMMA, np.float32)
    n = np.arange(L, dtype=np.float32)
    diff = n[:, None] - n[None, :]
    intra = np.where(diff[None] >= 0, np.exp(np.maximum(diff, 0.0)[None] * lg[:, None, None]), 0.0)
    q_decay = np.exp((n + 1.0)[None, :, None] * lg[:, None, None])
    k_decay = np.exp((L - 1.0 - n)[None, :, None] * lg[:, None, None])
    return (jnp.asarray(intra, F32), jnp.asarray(np.broadcast_to(q_decay, (RET_HEADS, L, RET_DV)), F32),
            jnp.asarray(np.broadcast_to(k_decay, (RET_HEADS, L, RET_DK)), F32))


def _retention_chunk(z_ref, rows, cos, sin, intra_ref, qd_ref, kd_ref, st_ref, og_ref, interleave):
    def zcols(start, width):
        first = start // MXU_COLS
        return jnp.concatenate([z_ref[first + i, rows, :] for i in range(width // MXU_COLS)], axis=-1)

    qs, ks, vs, scores = [], [], [], []
    for h in range(RET_HEADS):
        if h % 2 == 0:
            interleave(h // 2)
        q = _rotate(zcols(COL_Q + h * RET_DK, RET_DK).astype(F32), cos, sin)
        k = _rotate(zcols(COL_K + h * RET_DK, RET_DK).astype(F32), cos, sin) * (RET_DK ** -0.5)
        qs.append(q.astype(BF16))
        ks.append((k * kd_ref[h]).astype(BF16))
        vs.append(zcols(COL_V + h * RET_DV, RET_DV))
        scores.append((lax.dot_general(qs[h], k.astype(BF16), (((1,), (1,)), ((), ())),
                                       preferred_element_type=F32) * intra_ref[h]).astype(BF16))
    for h in range(RET_HEADS):
        if h % 2 == 0:
            interleave(RET_HEADS // 2 + h // 2)
        state = st_ref[0, h]
        o = _dot(scores[h], vs[h]) + _dot(qs[h], state.astype(BF16)) * qd_ref[h]
        st_ref[0, h] = math.exp(RET_CHUNK * LOG_GAMMA[h]) * state + lax.dot_general(
            ks[h], vs[h], (((0,), (0,)), ((), ())), preferred_element_type=F32)
        gate = zcols(COL_G + h * RET_DV, RET_DV).astype(F32)
        og_ref[:, h * RET_DV:(h + 1) * RET_DV] = (gate * jax.nn.sigmoid(gate) * _rms(o)).astype(og_ref.dtype)


SC_LANES = 16
SC_ROWS = 32


def _sample_qk_kernel(z_ref, cos_ref, sin_ref, q_ref, k_ref):
    for h in range(RET_HEADS):
        cols = slice(h * RET_DK, (h + 1) * RET_DK)
        q_ref[:, cols] = _rotate(z_ref[:, COL_Q + h * RET_DK:COL_Q + (h + 1) * RET_DK], cos_ref[...], sin_ref[...])
        k_ref[:, cols] = _rotate(z_ref[:, COL_K + h * RET_DK:COL_K + (h + 1) * RET_DK], cos_ref[...],
                                 sin_ref[...]) * (RET_DK ** -0.5)


def _sample_qk(z, cos, sin):
    m = z.shape[0]
    full = lambda a: pl.BlockSpec(a.shape, lambda i: (0,) * a.ndim)
    out = pl.BlockSpec((m, RET_QK), lambda i: (0, 0))
    return pl.pallas_call(
        _sample_qk_kernel,
        grid=(1,),
        in_specs=[pl.BlockSpec((m, 2 * RET_QK), lambda i: (0, 0)), full(cos), full(sin)],
        out_specs=[out, out],
        out_shape=[jax.ShapeDtypeStruct((m, RET_QK), F32)] * 2,
        compiler_params=_params("arbitrary"),
        name="sample_qk",
    )(z, cos, sin)


def _state_update(state, q, k, z):
    n_seq = state.shape[0]
    n_pairs = n_seq * RET_HEADS
    blocks = RET_DK // SC_ROWS
    state2d = state.reshape(n_pairs * RET_DK, RET_DV)
    gamma = jnp.asarray(np.tile(np.exp(np.array(LOG_GAMMA, np.float32))[None, :, None],
                                (n_seq, 1, SC_LANES)).reshape(n_pairs, SC_LANES))
    mesh = plsc.VectorSubcoreMesh(core_axis_name="core", subcore_axis_name="subcore")

    @pl.kernel(out_type=[jax.ShapeDtypeStruct(state2d.shape, F32),
                         jax.ShapeDtypeStruct((n_pairs * blocks, RET_DV), F32)],
               mesh=mesh, scratch_types=[])
    def state_update(st_hbm, q_hbm, k_hbm, v_hbm, g_hbm, new_hbm, o_hbm):
        def body(st, q, k, v, g, new, o):
            lane0 = pl.ds(0, SC_LANES)
            row0 = pl.ds(0, 1)
            gam = g.at[row0, lane0][...]

            @pl.loop(0, RET_DV, step=SC_LANES)
            def _(c):
                cols = pl.ds(c, SC_LANES)
                vv = v.at[row0, cols][...]
                acc = jnp.zeros((1, SC_LANES), F32)
                for r0 in range(0, SC_ROWS, SC_LANES):
                    kv = k.at[row0, pl.ds(r0, SC_LANES)][...]
                    qv = q.at[row0, pl.ds(r0, SC_LANES)][...]
                    for i in range(SC_LANES):
                        row = pl.ds(r0 + i, 1)
                        n = gam * st.at[row, cols][...] + kv[0, i] * vv
                        new.at[row, cols][...] = n
                        acc = acc + qv[0, i] * n
                o.at[row0, cols][...] = acc

        row_block = lambda p, b: (p * blocks + b, 0)
        pair = lambda p, b: (p, 0)
        v_cols = lambda p, b: (p // RET_HEADS, COL_V // RET_DV + p % RET_HEADS)
        pltpu.emit_pipeline(
            body,
            grid=(n_pairs, blocks),
            in_specs=[pl.BlockSpec((SC_ROWS, RET_DV), row_block), pl.BlockSpec((1, SC_ROWS), row_block),
                      pl.BlockSpec((1, SC_ROWS), row_block), pl.BlockSpec((1, RET_DV), v_cols),
                      pl.BlockSpec((1, SC_LANES), pair)],
            out_specs=[pl.BlockSpec((SC_ROWS, RET_DV), row_block), pl.BlockSpec((1, RET_DV), row_block)],
            core_axis_name=("core", "subcore"),
            dimension_semantics=(pltpu.PARALLEL, pltpu.PARALLEL),
        )(st_hbm, q_hbm, k_hbm, v_hbm, g_hbm, new_hbm, o_hbm)

    q2d = q.reshape(n_pairs * blocks, SC_ROWS)
    k2d = k.reshape(n_pairs * blocks, SC_ROWS)
    new_state, o_parts = state_update(state2d, q2d, k2d, z, gamma)
    return new_state.reshape(state.shape), o_parts.reshape(n_seq, RET_HEADS, blocks, RET_DV)


def _proj_ret_kernel(*refs, blocks_per_seq, n_side):
    x_ref, g_ref, w_ref, cos_ref, sin_ref, intra_ref, qd_ref, kd_ref = refs[:8]
    side_in = refs[8:8 + n_side]
    og_ref, st_ref, xn_ref = refs[8 + n_side:11 + n_side]
    side_out = refs[11 + n_side:11 + 2 * n_side]
    za_ref, zb_ref = refs[11 + 2 * n_side:]
    t = pl.program_id(0)
    j = pl.program_id(1)
    ret_block = jnp.maximum(t - 1, 0)
    pos_rows = pl.ds(pl.multiple_of(((ret_block % blocks_per_seq) * PR_CHUNKS + j) * RET_CHUNK, RET_CHUNK),
                     RET_CHUNK)

    @pl.when(j == 0)
    def _():
        xn_ref[...] = (_rms(x_ref[...]) * g_ref[...]).astype(BF16)

    @pl.when(jnp.logical_and(t == 0, j == 0))
    def _():
        zb_ref[...] = jnp.zeros_like(zb_ref)

    @pl.when(jnp.logical_and(ret_block % blocks_per_seq == 0, j == 0))
    def _():
        st_ref[...] = jnp.zeros_like(st_ref)

    rows = pl.ds(pl.multiple_of(j * RET_CHUNK, RET_CHUNK), RET_CHUNK)

    def step(z_write, z_read):
        def interleave(slot):
            for p in range(slot * PR_TILE_PIECES // RET_HEADS, (slot + 1) * PR_TILE_PIECES // RET_HEADS):
                piece = j * PR_TILE_PIECES + p
                z_write[piece] = _dot(xn_ref[...], w_ref[piece]).astype(BF16)

        _retention_chunk(z_read, rows, cos_ref[pos_rows, :], sin_ref[pos_rows, :], intra_ref, qd_ref, kd_ref,
                         st_ref, og_ref, interleave)

    @pl.when(t % 2 == 0)
    def _():
        step(za_ref, zb_ref)

    @pl.when(t % 2 == 1)
    def _():
        step(zb_ref, za_ref)

    _side_cast(side_in, side_out)


def _proj_ret(x, g, w_pieces, cos, sin, side_casts, *, batch, seq):
    n_blocks = batch * seq // PR_BLOCK
    blocks_per_seq = seq // PR_BLOCK
    intra, q_decay, k_decay = _decay_tables()
    full = lambda a: pl.BlockSpec(a.shape, lambda t, j: (0,) * a.ndim, pipeline_mode=pl.Buffered(1))
    proj_block = lambda t: jnp.minimum(t, n_blocks - 1)
    ret_block = lambda t: jnp.maximum(t - 1, 0)
    og_block = lambda t: jnp.where(t == 0, n_blocks, t - 1)
    side_in, side_out, side_shapes = _side_cast_specs(
        side_casts, (n_blocks + 1) * PR_CHUNKS, lambda t, j: t * PR_CHUNKS + j, pieces=True)
    return pl.pallas_call(
        functools.partial(_proj_ret_kernel, blocks_per_seq=blocks_per_seq, n_side=len(side_casts)),
        grid=(n_blocks + 1, PR_CHUNKS),
        in_specs=[
            pl.BlockSpec((PR_BLOCK, D_MODEL), lambda t, j: (proj_block(t), 0)),
            pl.BlockSpec((1, D_MODEL), lambda t, j: (0, 0)),
            pl.BlockSpec(_pieces_shape(D_MODEL, RET_COLS), lambda t, j: (0, 0, 0), pipeline_mode=pl.Buffered(1)),
            full(cos), full(sin), full(intra), full(q_decay), full(k_decay),
        ] + side_in,
        out_specs=[
            pl.BlockSpec((RET_CHUNK, RET_V), lambda t, j: (og_block(t) * PR_CHUNKS + j, 0)),
            pl.BlockSpec((1, RET_HEADS, RET_DK, RET_DV), lambda t, j: (ret_block(t) // blocks_per_seq, 0, 0, 0)),
            pl.BlockSpec((PR_BLOCK, D_MODEL), lambda t, j: (proj_block(t), 0)),
        ] + side_out,
        out_shape=[
            jax.ShapeDtypeStruct(((n_blocks + 1) * PR_BLOCK, RET_V), BF16),
            jax.ShapeDtypeStruct((batch, RET_HEADS, RET_DK, RET_DV), F32),
            jax.ShapeDtypeStruct((n_blocks * PR_BLOCK, D_MODEL), BF16),
        ] + side_shapes,
        scratch_shapes=[
            pltpu.VMEM(_pieces_shape(PR_BLOCK, RET_COLS), BF16),
            pltpu.VMEM(_pieces_shape(PR_BLOCK, RET_COLS), BF16),
        ],
        compiler_params=_params("arbitrary", "arbitrary"),
        name="proj_ret",
    )(x, g.reshape(1, D_MODEL), w_pieces, cos, sin, intra, q_decay, k_decay, *side_casts)


def _gelu_tanh(x):
    c = math.sqrt(2.0 / math.pi)
    return x * (0.5 + 0.5 * jnp.tanh(x * (c + (c * 0.044715) * (x * x))))


def _layer_norm(x, g, b):
    xc = x - jnp.mean(x, axis=-1, keepdims=True)
    return xc * lax.rsqrt(jnp.mean(xc * xc, axis=-1, keepdims=True) + EPS) * g + b


def _dot_pieces(a, w_ref):
    return jnp.concatenate([_dot(a, w_ref[p]) for p in range(w_ref.shape[0])], axis=-1)


def _merge(x_ref, ar_ref, ag_ref, og_ref, gm, wret_ref, wgm_ref, wo_ref, h_ref):
    branch_ret = _dot_pieces(og_ref[...].astype(BF16), wret_ref)
    branch_gm = _dot_pieces(gm, wgm_ref)
    m = (jax.nn.sigmoid(ar_ref[...]) * branch_ret + jax.nn.sigmoid(ag_ref[...]) * branch_gm)
    h_ref[...] = x_ref[...] + _dot_pieces(m.astype(BF16), wo_ref)


REST_COLS = D_IN - RET_COLS
REST_GU, REST_GV, REST_AR, REST_AG = 0, GM_WIDTH, 2 * GM_WIDTH, 2 * GM_WIDTH + D_MODEL


def _proj_mix_kernel(*refs, n_side):
    (x_ref, xn_ref, og_ref, wuv_ref, waa_ref, lng_ref, lnb_ref, ws_ref, bs_ref,
     wret_ref, wgm_ref, wo_ref) = refs[:12]
    side_in = refs[12:12 + n_side]
    h_ref = refs[12 + n_side]
    side_out = refs[13 + n_side:13 + 2 * n_side]
    zr_ref, gm_ref, br_ref, m_ref = refs[13 + 2 * n_side:]

    L = GM_CHUNK
    n_chunks = x_ref.shape[0] // L
    n_pieces = D_MODEL // MXU_COLS
    piece = lambda p, base=0: slice(base + p * MXU_COLS, base + (p + 1) * MXU_COLS)

    for p in range(n_pieces):
        zr_ref[:, piece(p, REST_GV)] = _dot(xn_ref[...], wuv_ref[n_pieces + p])
    for p in range(n_pieces):
        zr_ref[:, piece(p, REST_GU)] = _dot(xn_ref[...], wuv_ref[p])

    causal = lax.broadcasted_iota(jnp.int32, (L, L), 0) >= lax.broadcasted_iota(jnp.int32, (L, L), 1)
    w_causal = [jnp.where(causal, ws_ref[g], 0.0).astype(BF16) for g in range(GM_GROUPS)]
    for c in range(n_chunks):
        for p in range(c * n_pieces // n_chunks, (c + 1) * n_pieces // n_chunks):
            zr_ref[:, piece(p, REST_AR)] = _dot(xn_ref[...], waa_ref[p])
            zr_ref[:, piece(p, REST_AG)] = _dot(xn_ref[...], waa_ref[n_pieces + p])
            br_ref[:, piece(p)] = _dot(og_ref[...], wret_ref[p])
        rows = slice(c * L, (c + 1) * L)
        u = _gelu_tanh(zr_ref[rows, REST_GU:REST_GU + GM_WIDTH])
        v = _layer_norm(_gelu_tanh(zr_ref[rows, REST_GV:REST_GV + GM_WIDTH]), lng_ref[...],
                        lnb_ref[...]).astype(BF16)
        for g in range(GM_GROUPS):
            cols = slice(g * GM_CG, (g + 1) * GM_CG)
            mixed = _dot(w_causal[g], v[:, cols]) + bs_ref[:, g:g + 1]
            gm_ref[rows, cols] = (u[:, cols] * mixed).astype(BF16)
    for p in range(n_pieces):
        branch_gm = _dot(gm_ref[...], wgm_ref[p])
        m_ref[:, piece(p)] = (jax.nn.sigmoid(zr_ref[:, piece(p, REST_AR)]) * br_ref[:, piece(p)]
                              + jax.nn.sigmoid(zr_ref[:, piece(p, REST_AG)]) * branch_gm).astype(BF16)
    for p in range(n_pieces):
        h_ref[:, piece(p)] = x_ref[:, piece(p)] + _dot(m_ref[...], wo_ref[p])
    _side_cast(side_in, side_out)


def _proj_mix(x, xn, og, w_pieces, ln_g, ln_b, ws, bs, wret, wgm, wo, side_casts, *, tm):
    m = x.shape[0]
    half = REST_COLS // 2
    assert RET_COLS % half == 0
    full = lambda a: pl.BlockSpec(a.shape, lambda i: (0,) * a.ndim, pipeline_mode=pl.Buffered(1))
    wcols = lambda k: pl.BlockSpec(_pieces_shape(D_MODEL, half), lambda i: (RET_COLS // half + k, 0, 0),
                                   pipeline_mode=pl.Buffered(1))
    ln_g = ln_g.reshape(1, GM_WIDTH)
    ln_b = ln_b.reshape(1, GM_WIDTH)
    bs = bs.T
    side_in, side_out, side_shapes = _side_cast_specs(side_casts, m // tm, lambda i: i, pieces=False)
    row_block = lambda width: pl.BlockSpec((tm, width), lambda i: (i, 0))
    return pl.pallas_call(
        functools.partial(_proj_mix_kernel, n_side=len(side_casts)),
        grid=(m // tm,),
        in_specs=[row_block(D_MODEL), row_block(D_MODEL), row_block(RET_V), wcols(0), wcols(1),
                  full(ln_g), full(ln_b), full(ws), full(bs), full(wret), full(wgm), full(wo)] + side_in,
        out_specs=[row_block(D_MODEL)] + side_out,
        out_shape=[jax.ShapeDtypeStruct((m, D_MODEL), F32)] + side_shapes,
        scratch_shapes=[pltpu.VMEM((tm, REST_COLS), F32), pltpu.VMEM((tm, GM_WIDTH), BF16),
                        pltpu.VMEM((tm, D_MODEL), F32), pltpu.VMEM((tm, D_MODEL), BF16)],
        compiler_params=_params("parallel"),
        name="proj_mix",
    )(x, xn, og, w_pieces, w_pieces, ln_g, ln_b, ws, bs, wret, wgm, wo, *side_casts)


def _mix_step_kernel(x_ref, gu_ref, gv_ref, ar_ref, ag_ref, gate_ref, op_ref, lng_ref, lnb_ref, ws_ref, bs_ref,
                     wret_ref, wgm_ref, wo_ref, h_ref, v_ref, og_ref):
    for hd in range(RET_HEADS):
        cols = slice(hd * RET_DV, (hd + 1) * RET_DV)
        o = jnp.sum(op_ref[:, hd], axis=1)
        gate = gate_ref[:, cols]
        og_ref[:, cols] = gate * jax.nn.sigmoid(gate) * _rms(o)
    u = _gelu_tanh(gu_ref[...])
    v = _layer_norm(_gelu_tanh(gv_ref[...]), lng_ref[...], lnb_ref[...])
    v_ref[...] = v
    gm = (u * (ws_ref[...] * v + bs_ref[...])).astype(BF16)
    _merge(x_ref, ar_ref, ag_ref, og_ref, gm, wret_ref, wgm_ref, wo_ref, h_ref)


def _mix_step(x, z, o_parts, ln_g, ln_b, ws, bs, wret, wgm, wo):
    m = x.shape[0]
    zcol = lambda off: pl.BlockSpec((m, D_MODEL), lambda i: (0, off // D_MODEL))
    full = lambda a: pl.BlockSpec(a.shape, lambda i: (0,) * a.ndim)
    ln_g = ln_g.reshape(1, GM_WIDTH)
    ln_b = ln_b.reshape(1, GM_WIDTH)
    ws = jnp.repeat(ws[:, 0, 0], GM_CG).reshape(1, GM_WIDTH)
    bs = jnp.repeat(bs[:, 0], GM_CG).reshape(1, GM_WIDTH)
    return pl.pallas_call(
        _mix_step_kernel,
        grid=(1,),
        in_specs=[full(x), zcol(COL_GU), zcol(COL_GV), zcol(COL_AR), zcol(COL_AG),
                  pl.BlockSpec((m, RET_V), lambda i: (0, COL_G // RET_V)), full(o_parts),
                  full(ln_g), full(ln_b), full(ws), full(bs), full(wret), full(wgm), full(wo)],
        out_specs=[pl.BlockSpec((m, D_MODEL), lambda i: (0, 0)), pl.BlockSpec((m, GM_WIDTH), lambda i: (0, 0))],
        out_shape=[jax.ShapeDtypeStruct((m, D_MODEL), F32), jax.ShapeDtypeStruct((m, GM_WIDTH), F32)],
        scratch_shapes=[pltpu.VMEM((m, RET_V), F32)],
        compiler_params=_params("arbitrary"),
        name="mix_step",
    )(x, z, z, z, z, z, o_parts, ln_g, ln_b, ws, bs, wret, wgm, wo)


FFN_PART_ROWS = 256


def _ffn_kernel(h_ref, g_ref, win_ref, wdown_ref, gf_ref, y_ref, *, final_norm):
    tm = h_ref.shape[0]
    n_parts = max(1, tm // FFN_PART_ROWS)
    for r in range(n_parts):
        rows = slice(r * tm // n_parts, (r + 1) * tm // n_parts)
        h = h_ref[rows, :]
        hn = (_rms(h) * g_ref[...]).astype(BF16)
        f = _dot(hn, win_ref[...])
        f_gate = f[:, :D_FF]
        f_up = f[:, D_FF:]
        act = (f_gate * jax.nn.sigmoid(f_gate) * f_up).astype(BF16)
        out = h + _dot(act, wdown_ref[...])
        if final_norm:
            out = _rms(out) * gf_ref[...]
        y_ref[rows, :] = out


def _ffn(h, g, win, wdown, g_final, *, tm, final_norm):
    m = h.shape[0]
    full = lambda a: pl.BlockSpec(a.shape, lambda i: (0,) * a.ndim, pipeline_mode=pl.Buffered(1))
    g = g.reshape(1, D_MODEL)
    g_final = g_final.reshape(1, D_MODEL)
    return pl.pallas_call(
        functools.partial(_ffn_kernel, final_norm=final_norm),
        grid=(m // tm,),
        in_specs=[pl.BlockSpec((tm, D_MODEL), lambda i: (i, 0)), full(g), full(win), full(wdown),
                  full(g_final)],
        out_specs=pl.BlockSpec((tm, D_MODEL), lambda i: (i, 0)),
        out_shape=jax.ShapeDtypeStruct((m, D_MODEL), F32),
        compiler_params=_params("parallel"),
        name="ffn",
    )(h, g, win, wdown, g_final)


def _rope_tables(pos):
    inv = ROPE_BASE ** (-jnp.arange(ROPE_HALF, dtype=F32) / ROPE_HALF)
    ang = pos[:, None] * inv[None, :]
    return jnp.cos(ang), jnp.sin(ang)


def kernel(x_prompt, x_sample, state_ret, norm_mix_g, w_in, w_ret_o, gm_ln_g, gm_ln_b, gm_ws, gm_bs,
           w_gm_o, w_o, norm_ffn_g, w_ffn_in, w_ffn_down, norm_final_g):
    batch, seq, _ = x_prompt.shape
    dec_batch, dec_seq, _ = x_sample.shape
    depth = w_in.shape[0]
    assert dec_seq == 1 and seq % PR_BLOCK == 0

    cos_p, sin_p = _rope_tables(jnp.arange(seq, dtype=F32))
    cos_s, sin_s = _rope_tables(PAST_LEN + jnp.arange(dec_seq, dtype=F32))

    hp = x_prompt.reshape(batch * seq, D_MODEL)
    hs = x_sample.reshape(dec_batch, D_MODEL)
    ret_p, ret_s, gmv_s = [], [], []
    for l in range(depth):
        last = l == depth - 1
        zs, win = _in_proj(hs, norm_mix_g[l], w_in[l], tn=IN_PROJ_TILE)
        qs, ks = _sample_qk(zs, cos_s, sin_s)
        ss, o_parts = _state_update(state_ret[l], qs, ks, zs)
        og, sp, xn, wret, wgm, wo = _proj_ret(
            hp, norm_mix_g[l], win, cos_p, sin_p, (w_ret_o[l], w_gm_o[l], w_o[l]), batch=batch, seq=seq)

        h, wfin, wfdown = _proj_mix(hp, xn, og, win, gm_ln_g[l], gm_ln_b[l], gm_ws[l], gm_bs[l],
                                    wret, wgm, wo, (w_ffn_in[l], w_ffn_down[l]), tm=MIX_BLOCK)
        hp = _ffn(h, norm_ffn_g[l], wfin, wfdown, norm_final_g, tm=FFN_BLOCK, final_norm=last)
        ret_p.append(sp)

        h, vs = _mix_step(hs, zs, o_parts, gm_ln_g[l], gm_ln_b[l], gm_ws[l], gm_bs[l], wret, wgm, wo)
        hs = _ffn(h, norm_ffn_g[l], wfin, wfdown, norm_final_g, tm=dec_batch, final_norm=last)
        ret_s.append(ss)
        gmv_s.append(vs.reshape(dec_batch, dec_seq, GM_WIDTH))

    return (hp.reshape(batch, seq, D_MODEL), hs.reshape(dec_batch, dec_seq, D_MODEL),
            jnp.stack(ret_p), jnp.stack(ret_s), jnp.stack(gmv_s))
```
